```python
import jax, jax.numpy as jnp
from jax import lax
import numpy as np

D_MODEL = 1024
BATCH = 2
SEQ = 8192
DEPTH = 2

ML_HEADS = 4
ML_HEAD_DIM = 256
ML_W = ML_HEADS * ML_HEAD_DIM
ML_CHUNK = 64
SC_W = D_MODEL
CONV_K = 3
SB_HEADS = 16
SB_HEAD_DIM = 64
SB_W = SB_HEADS * SB_HEAD_DIM
SB_BLOCK = 128
N_BRANCH = 3
D_FF = 4 * D_MODEL
EPS = 1e-6
SPLIT_SIZES = (ML_W, ML_W, ML_W, ML_W, ML_HEADS, ML_HEADS,
               SC_W, SC_W, SC_W,
               SB_W, SB_W, SB_W,
               D_MODEL, D_MODEL, D_MODEL)
N_IN = sum(SPLIT_SIZES)

kernel_name = "hybrid_mlstm_shortconv_stickbreaking_block"


def rmsnorm(x, g):
    xf = x.astype(jnp.float32)
    y = xf * lax.rsqrt(jnp.mean(xf * xf, axis=-1, keepdims=True) + EPS)
    return (y * g.astype(jnp.float32)).astype(x.dtype)


def _to_chunks(a, nc, L):
    a = a.reshape((a.shape[0], nc, L) + a.shape[2:])
    return jnp.swapaxes(jnp.moveaxis(a, 1, 0), 2, 3)


def mlstm(q, k, v, i_pre, f_pre):
    B_, S_, H, dh = q.shape
    L = ML_CHUNK
    nc = S_ // L
    f32 = jnp.float32
    qc = _to_chunks(q.astype(f32), nc, L)
    kc = _to_chunks(k.astype(f32) * (dh ** -0.5), nc, L)
    vc = _to_chunks(v.astype(f32), nc, L)
    ic = _to_chunks(i_pre.astype(f32), nc, L)
    lfc = _to_chunks(jax.nn.log_sigmoid(f_pre.astype(f32)), nc, L)
    causal = jnp.tril(jnp.ones((L, L), dtype=bool))

    def step(carry, xs):
        C, n, m = carry
        qb, kb, vb, ib, lfb = xs
        b = jnp.cumsum(lfb, axis=-1)
        dmat = b[..., :, None] - b[..., None, :] + ib[..., None, :]
        dmat = jnp.where(causal, dmat, -jnp.inf)
        inter = b + m[..., None]
        m_t = jnp.maximum(inter, jnp.max(dmat, axis=-1))
        w_intra = jnp.exp(dmat - m_t[..., None])
        a_inter = jnp.exp(inter - m_t)
        s = jnp.einsum('bhtd,bhsd->bhts', qb, kb) * w_intra
        num = (jnp.einsum('bhts,bhsd->bhtd', s, vb)
               + a_inter[..., None] * jnp.einsum('bhvk,bhtk->bhtv', C, qb))
        den = jnp.sum(s, axis=-1) + a_inter * jnp.einsum('bhk,bhtk->bht', n, qb)
        h = num / jnp.maximum(jnp.abs(den), jnp.exp(-m_t))[..., None]
        b_last = b[..., -1]
        g = b_last[..., None] - b + ib
        m_new = jnp.maximum(b_last + m, jnp.max(g, axis=-1))
        decay = jnp.exp(b_last + m - m_new)
        w_state = jnp.exp(g - m_new[..., None])
        C_new = decay[..., None, None] * C + jnp.einsum('bhs,bhsv,bhsk->bhvk', w_state, vb, kb)
        n_new = decay[..., None] * n + jnp.einsum('bhs,bhsk->bhk', w_state, kb)
        return (C_new, n_new, m_new), h

    init = (jnp.zeros((B_, H, dh, dh), f32), jnp.zeros((B_, H, dh), f32), jnp.zeros((B_, H), f32))
    _, hs = lax.scan(step, init, (qc, kc, vc, ic, lfc))
    hs = jnp.moveaxis(jnp.swapaxes(hs, 2, 3), 0, 1)
    return hs.reshape(B_, S_, H, dh)


def short_conv(gate_b, gate_c, u, w):
    z = gate_c * u
    y = lax.conv_general_dilated(
        z, w[:, None, :].astype(z.dtype), window_strides=(1,),
        padding=[(CONV_K - 1, 0)], dimension_numbers=('NWC', 'WIO', 'NWC'),
        feature_group_count=z.shape[-1])
    return gate_b * y


def stick_breaking(q, k, v):
    B_, S_, H, d = q.shape
    nb = S_ // SB_BLOCK
    f32 = jnp.float32
    qf = q.astype(f32) * (d ** -0.5)
    kf = k.astype(f32)
    vf = v.astype(f32)
    q_blocks = jnp.transpose(qf.reshape(B_, nb, SB_BLOCK, H, d), (1, 0, 3, 2, 4))
    starts = jnp.arange(nb, dtype=jnp.int32) * SB_BLOCK
    key_pos = jnp.arange(S_, dtype=jnp.int32)

    def block(args):
        qblk, start = args
        q_pos = start + jnp.arange(SB_BLOCK, dtype=jnp.int32)
        mask = key_pos[None, :] < q_pos[:, None]
        z = jnp.einsum('bhqd,bshd->bhqs', qblk, kf)
        log_beta = jax.nn.log_sigmoid(z)
        log_1m = jnp.where(mask, log_beta - z, 0.0)
        cum = jnp.cumsum(log_1m, axis=-1)
        log_a = log_beta + cum[..., -1:] - cum
        a = jnp.where(mask, jnp.exp(log_a), 0.0)
        return jnp.einsum('bhqs,bshd->bqhd', a, vf)

    out = lax.map(block, (q_blocks, starts))
    return jnp.moveaxis(out, 0, 1).reshape(B_, S_, H * d)


def head_rmsnorm(h, g, n_heads):
    B_, S_, W = h.shape
    hh = h.reshape(B_, S_, n_heads, W // n_heads)
    hh = hh * lax.rsqrt(jnp.mean(hh * hh, axis=-1, keepdims=True) + EPS)
    return hh.reshape(B_, S_, W) * g.astype(jnp.float32)


def setup_inputs(seed: int = 0) -> dict:
    key = jax.random.key(seed)
    ks = jax.random.split(key, 16)
    nrm = jax.random.normal
    x = nrm(ks[0], (BATCH, SEQ, D_MODEL), jnp.float32)
    norm_mix_g = 1.0 + 0.02 * nrm(ks[1], (DEPTH, D_MODEL), jnp.float32)
    w_in = nrm(ks[2], (DEPTH, D_MODEL, N_IN), jnp.float32) * D_MODEL ** -0.5
    b_if = jnp.concatenate([
        0.1 * nrm(ks[3], (DEPTH, ML_HEADS), jnp.float32),
        jnp.linspace(3.0, 6.0, ML_HEADS, dtype=jnp.float32)[None, :]
        + 0.1 * nrm(ks[4], (DEPTH, ML_HEADS), jnp.float32)], axis=-1)
    ml_norm_g = 1.0 + 0.02 * nrm(ks[5], (DEPTH, ML_W), jnp.float32)
    conv_w = nrm(ks[6], (DEPTH, CONV_K, SC_W), jnp.float32) * CONV_K ** -0.5
    w_ml_proj = nrm(ks[7], (DEPTH, ML_W, D_MODEL), jnp.float32) * ML_W ** -0.5
    w_sc_proj = nrm(ks[8], (DEPTH, SC_W, D_MODEL), jnp.float32) * SC_W ** -0.5
    w_sb_proj = nrm(ks[9], (DEPTH, SB_W, D_MODEL), jnp.float32) * SB_W ** -0.5
    w_out = nrm(ks[10], (DEPTH, D_MODEL, D_MODEL), jnp.float32) * D_MODEL ** -0.5
    norm_mlp_g = 1.0 + 0.02 * nrm(ks[11], (DEPTH, D_MODEL), jnp.float32)
    w_up = nrm(ks[12], (DEPTH, D_MODEL, D_FF), jnp.float32) * D_MODEL ** -0.5
    w_down = nrm(ks[13], (DEPTH, D_FF, D_MODEL), jnp.float32) * D_FF ** -0.5
    norm_final_g = 1.0 + 0.02 * nrm(ks[14], (D_MODEL,), jnp.float32)
    return {"x": x, "norm_mix_g": norm_mix_g, "w_in": w_in, "b_if": b_if,
            "ml_norm_g": ml_norm_g, "conv_w": conv_w, "w_ml_proj": w_ml_proj,
            "w_sc_proj": w_sc_proj, "w_sb_proj": w_sb_proj, "w_out": w_out,
            "norm_mlp_g": norm_mlp_g, "w_up": w_up, "w_down": w_down,
            "norm_final_g": norm_final_g}


def reference(x, norm_mix_g, w_in, b_if, ml_norm_g, conv_w, w_ml_proj, w_sc_proj,
              w_sb_proj, w_out, norm_mlp_g, w_up, w_down, norm_final_g):
    B_, S_, _ = x.shape
    offsets = []
    acc = 0
    for sz in SPLIT_SIZES[:-1]:
        acc += sz
        offsets.append(acc)
    for l in range(DEPTH):
        h = rmsnorm(x, norm_mix_g[l])
        proj = h @ w_in[l]
        (ml_q, ml_k, ml_v, ml_o, ml_i, ml_f,
         sc_b, sc_c, sc_u,
         sb_q, sb_k, sb_v,
         g_ml, g_sc, g_sb) = jnp.split(proj, offsets, axis=-1)
        i_pre = ml_i + b_if[l, :ML_HEADS]
        f_pre = ml_f + b_if[l, ML_HEADS:]
        h_tilde = mlstm(ml_q.reshape(B_, S_, ML_HEADS, ML_HEAD_DIM),
                        ml_k.reshape(B_, S_, ML_HEADS, ML_HEAD_DIM),
                        ml_v.reshape(B_, S_, ML_HEADS, ML_HEAD_DIM),
                        i_pre, f_pre).reshape(B_, S_, ML_W)
        y_ml = head_rmsnorm(jax.nn.sigmoid(ml_o.astype(jnp.float32)) * h_tilde,
                            ml_norm_g[l], ML_HEADS).astype(x.dtype)
        y_sc = short_conv(sc_b, sc_c, sc_u, conv_w[l])
        y_sb = stick_breaking(sb_q.reshape(B_, S_, SB_HEADS, SB_HEAD_DIM),
                              sb_k.reshape(B_, S_, SB_HEADS, SB_HEAD_DIM),
                              sb_v.reshape(B_, S_, SB_HEADS, SB_HEAD_DIM)).astype(x.dtype)
        merged = (jax.nn.sigmoid(g_ml) * (y_ml @ w_ml_proj[l])
                  + jax.nn.sigmoid(g_sc) * (y_sc @ w_sc_proj[l])
                  + jax.nn.sigmoid(g_sb) * (y_sb @ w_sb_proj[l]))
        x = x + merged @ w_out[l]
        h2 = rmsnorm(x, norm_mlp_g[l])
        x = x + jnp.square(jax.nn.relu(h2 @ w_up[l])) @ w_down[l]
    return rmsnorm(x, norm_final_g)
```

```python
import functools

import jax
import jax.numpy as jnp
from jax import lax
from jax.experimental import pallas as pl
from jax.experimental.pallas import tpu as pltpu

D_MODEL = 1024
ML_HEADS = 4
ML_HEAD_DIM = 256
SB_HEADS = 16
SB_HEAD_DIM = 64
D_FF = 4 * D_MODEL
EPS = 1e-6

VMEM_LIMIT_BYTES = 56 * 1024 * 1024

ML_CHUNK = 256
SB_BLOCK = 256
TM_PROJ = 1024
TM_MERGE = 256
TM_MLP = 512
GATE_ROWS = 8

BF16 = jnp.bfloat16
F32 = jnp.float32


def _params(*sem):
    return pltpu.CompilerParams(dimension_semantics=sem, vmem_limit_bytes=VMEM_LIMIT_BYTES)


def _dot(a, b):
    return jnp.dot(a, b, preferred_element_type=F32)


def _dot_nt(a, b):
    return lax.dot_general(a, b, (((1,), (1,)), ((), ())), preferred_element_type=F32)


def _dot_tn(a, b):
    return lax.dot_general(a, b, (((0,), (0,)), ((), ())), preferred_element_type=F32)


def _rmsnorm_rows(x, g):
    ms = jnp.mean(x * x, axis=-1, keepdims=True)
    return x * lax.rsqrt(ms + EPS) * g


def _log_sigmoid(x):
    return jnp.minimum(x, 0.0) - jnp.log1p(jnp.exp(-jnp.abs(x)))


def _split_bf16(x):
    hi = x.astype(BF16)
    lo = (x - hi.astype(F32)).astype(BF16)
    return hi, lo


def _norm_proj_kernel(x_ref, g_ref, w_ref, o_ref, h_ref):
    @pl.when(pl.program_id(1) == 0)
    def _():
        h_ref[...] = _rmsnorm_rows(x_ref[...], g_ref[...]).astype(BF16)

    o_ref[...] = _dot(h_ref[...], w_ref[...]).astype(o_ref.dtype)


def _norm_proj(x, g, w, out_dtype, tn=1024):
    T, D = x.shape
    N = w.shape[1]
    tm = min(TM_PROJ, T)
    return pl.pallas_call(
        _norm_proj_kernel,
        grid=(T // tm, N // tn),
        in_specs=[
            pl.BlockSpec((tm, D), lambda i, j: (i, 0)),
            pl.BlockSpec((1, D), lambda i, j: (0, 0)),
            pl.BlockSpec((D, tn), lambda i, j: (0, j)),
        ],
        out_specs=pl.BlockSpec((tm, tn), lambda i, j: (i, j)),
        out_shape=jax.ShapeDtypeStruct((T, N), out_dtype),
        scratch_shapes=[pltpu.VMEM((tm, D), BF16)],
        compiler_params=_params("parallel", "arbitrary"),
        name="norm_proj",
    )(x, g, w)


def _norm_proj_t_kernel(x_ref, g_ref, wt_ref, wg_hi_ref, wg_lo_ref, bg_ref, o_ref, gates_ref, h_ref):
    @pl.when(pl.program_id(1) == 0)
    def _():
        h = _rmsnorm_rows(x_ref[...], g_ref[...])
        h_hi, h_lo = _split_bf16(h)
        h_ref[...] = h_hi
        wg_hi = wg_hi_ref[...]
        gates = _dot_nt(wg_hi, h_hi) + (_dot_nt(wg_hi, h_lo) + _dot_nt(wg_lo_ref[...], h_hi))
        gates_ref[...] = gates + bg_ref[...]

    o_ref[...] = _dot_nt(wt_ref[...], h_ref[...]).astype(o_ref.dtype)


def _norm_proj_t(x, g, wt, wg, bg, tn=1024):
    T, D = x.shape
    N = wt.shape[0]
    G = wg.shape[0]
    tm = min(TM_PROJ, T)
    wg_hi, wg_lo = _split_bf16(wg)
    return pl.pallas_call(
        _norm_proj_t_kernel,
        grid=(T // tm, N // tn),
        in_specs=[
            pl.BlockSpec((tm, D), lambda i, j: (i, 0)),
            pl.BlockSpec((1, D), lambda i, j: (0, 0)),
            pl.BlockSpec((tn, D), lambda i, j: (j, 0)),
            pl.BlockSpec((G, D), lambda i, j: (0, 0)),
            pl.BlockSpec((G, D), lambda i, j: (0, 0)),
            pl.BlockSpec((G, 1), lambda i, j: (0, 0)),
        ],
        out_specs=[
            pl.BlockSpec((tn, tm), lambda i, j: (j, i)),
            pl.BlockSpec((G, tm), lambda i, j: (0, i)),
        ],
        out_shape=[
            jax.ShapeDtypeStruct((N, T), BF16),
            jax.ShapeDtypeStruct((G, T), F32),
        ],
        scratch_shapes=[pltpu.VMEM((tm, D), BF16)],
        compiler_params=_params("parallel", "arbitrary"),
        name="norm_proj_t",
    )(x, g, wt, wg_hi, wg_lo, bg)


def _cumsum_lanes(x):
    n = x.shape[-1]
    lane = lax.broadcasted_iota(jnp.int32, x.shape, x.ndim - 1)
    k = 1
    while k < n:
        x = x + jnp.where(lane >= k, pltpu.roll(x, k, axis=x.ndim - 1), 0.0)
        k *= 2
    return x


def _mlstm_kernel(q_ref, kT_ref, v_ref, o_ref, gates_ref, gain_ref, y_ref, c_ref, n_ref, m_ref):
    @pl.when(pl.program_id(2) == 0)
    def _():
        c_ref[...] = jnp.zeros_like(c_ref)
        n_ref[...] = jnp.zeros_like(n_ref)
        m_ref[...] = jnp.zeros_like(m_ref)

    L, dh = q_ref.shape
    q = q_ref[...]
    kT = kT_ref[...]
    v = v_ref[...]
    gates = gates_ref[...]
    m_prev = m_ref[...]

    i_row = gates[0:1, :]
    lf_row = _log_sigmoid(gates)[1:2, :]
    b_row = _cumsum_lanes(_log_sigmoid(gates))[1:2, :]
    t_idx = lax.broadcasted_iota(jnp.int32, (L, L), 0)
    s_idx = lax.broadcasted_iota(jnp.int32, (L, L), 1)
    causal = s_idx <= t_idx
    b_col = jnp.sum(jnp.where(causal, lf_row, 0.0), axis=1, keepdims=True)

    dmat = jnp.where(causal, b_col - b_row + i_row, -jnp.inf)
    inter = b_col + m_prev
    m_t = jnp.maximum(inter, jnp.max(dmat, axis=1, keepdims=True))
    w_intra = jnp.exp(dmat - m_t)
    a_inter = jnp.exp(inter - m_t)

    s = _dot(q, kT) * w_intra
    c_prev = c_ref[...]
    n_prev = n_ref[...]
    n_mat = jnp.broadcast_to(n_prev, (dh, 128)).astype(BF16)
    nq = _dot(q, n_mat)[:, 0:1]
    num = _dot(s.astype(BF16), v) + a_inter * _dot(q, c_prev.astype(BF16))
    den = jnp.sum(s, axis=1, keepdims=True) + a_inter * nq
    h = num / jnp.maximum(jnp.abs(den), jnp.exp(-m_t))

    b_last = b_row[:, L - 1:L]
    g_row = b_last - b_row + i_row
    m_new = jnp.maximum(b_last + m_prev, jnp.max(g_row, axis=1, keepdims=True))
    decay = jnp.exp(b_last + m_prev - m_new)
    w_state = jnp.exp(g_row - m_new)
    kw = kT.astype(F32) * w_state
    c_ref[...] = decay * c_prev + _dot(kw.astype(BF16), v)
    n_ref[...] = decay * n_prev + jnp.sum(kw, axis=1, keepdims=True)
    m_ref[...] = m_new

    hg = jax.nn.sigmoid(o_ref[...]) * h
    ms = jnp.mean(hg * hg, axis=-1, keepdims=True)
    y_ref[...] = (hg * lax.rsqrt(ms + EPS) * gain_ref[...]).astype(y_ref.dtype)


def _mlstm(pn_b, pn_f, pt, gates, gain, *, batch, seq, q_col, v_col, o_col, kT_row):
    T = batch * seq
    L = min(ML_CHUNK, seq)
    nc = seq // L
    dh = ML_HEAD_DIM
    tok = lambda b, h, c: b * nc + c
    return pl.pallas_call(
        _mlstm_kernel,
        grid=(batch, ML_HEADS, nc),
        in_specs=[
            pl.BlockSpec((L, dh), lambda b, h, c: (tok(b, h, c), q_col + h)),
            pl.BlockSpec((dh, L), lambda b, h, c: (kT_row + h, tok(b, h, c))),
            pl.BlockSpec((L, dh), lambda b, h, c: (tok(b, h, c), v_col + h)),
            pl.BlockSpec((L, dh), lambda b, h, c: (tok(b, h, c), o_col + h)),
            pl.BlockSpec((GATE_ROWS, L), lambda b, h, c: (h, tok(b, h, c))),
            pl.BlockSpec((1, dh), lambda b, h, c: (0, h)),
        ],
        out_specs=pl.BlockSpec((L, dh), lambda b, h, c: (tok(b, h, c), h)),
        out_shape=jax.ShapeDtypeStruct((T, ML_HEADS * dh), BF16),
        scratch_shapes=[
            pltpu.VMEM((dh, dh), F32),
            pltpu.VMEM((dh, 1), F32),
            pltpu.VMEM((1, 1), F32),
        ],
        compiler_params=_params("parallel", "parallel", "arbitrary"),
        name="mlstm",
    )(pn_b, pt, pn_b, pn_f, gates, gain)


def _sb_kernel(qT_ref, k_ref, vT_ref, o_ref, *, blk):
    h = pl.program_id(1)
    i = pl.program_id(2)
    d = qT_ref.shape[0]

    qT = qT_ref[...]
    q2 = jnp.concatenate([qT, qT], axis=0)
    row_head = lax.broadcasted_iota(jnp.int32, q2.shape, 0) // d
    q_pad = jnp.where(row_head == h % 2, q2, jnp.zeros_like(q2))

    s_idx = lax.broadcasted_iota(jnp.int32, (blk, blk), 0)
    t_idx = lax.broadcasted_iota(jnp.int32, (blk, blk), 1)
    later = (t_idx > s_idx)
    upper = jnp.where(later, 1.0, 0.0).astype(BF16)

    def visit(j, carry, diagonal):
        r, acc = carry
        start = pl.multiple_of(j * blk, blk)
        z = _dot(k_ref[pl.ds(start, blk), :], q_pad)
        log_beta = _log_sigmoid(z)
        log_1m = log_beta - z
        if diagonal:
            log_1m = jnp.where(later, log_1m, 0.0)
        hi, lo = _split_bf16(log_1m)
        suffix = _dot(upper, hi) + _dot(upper, lo)
        a = jnp.exp(log_beta + suffix + r)
        if diagonal:
            a = jnp.where(later, a, 0.0)
        acc = acc + _dot(vT_ref[:, pl.ds(start, blk)], a.astype(BF16))
        r = r + (suffix[0:1, :] + log_1m[0:1, :])
        return r, acc

    carry = (jnp.zeros((1, blk), F32), jnp.zeros((d, blk), F32))
    carry = visit(i, carry, True)
    carry = lax.fori_loop(0, i, lambda n, c: visit(i - 1 - n, c, False), carry)
    o_ref[...] = carry[1].astype(o_ref.dtype)


def _stick_breaking(pn_b, pt, *, batch, seq, k_col, qT_row, vT_row):
    T = batch * seq
    blk = min(SB_BLOCK, seq)
    nq = seq // blk
    d = SB_HEAD_DIM
    return pl.pallas_call(
        functools.partial(_sb_kernel, blk=blk),
        grid=(batch, SB_HEADS, nq),
        in_specs=[
            pl.BlockSpec((d, blk), lambda b, h, i: (qT_row + h, b * nq + i)),
            pl.BlockSpec((seq, 2 * d), lambda b, h, i: (b, k_col + h // 2)),
            pl.BlockSpec((d, seq), lambda b, h, i: (vT_row + h, b)),
        ],
        out_specs=pl.BlockSpec((d, blk), lambda b, h, i: (h, b * nq + i)),
        out_shape=jax.ShapeDtypeStruct((SB_HEADS * d, T), BF16),
        compiler_params=_params("parallel", "parallel", "arbitrary"),
        name="stick_breaking",
    )(pt, pn_b, pt)


def _merge_kernel(x_ref, yml_ref, ysbT_ref, b_ref, c_ref, u_ref, cp_ref, up_ref,
                  gml_ref, gsc_ref, gsb_ref, cw_ref, wml_ref, wsc_ref, wsb_ref, wout_ref,
                  o_ref, *, tiles_per_seq):
    tm = x_ref.shape[0]
    first = (pl.program_id(0) % tiles_per_seq) == 0
    z = c_ref[...] * u_ref[...]
    z_prev = jnp.where(first, 0.0, cp_ref[...] * up_ref[...])
    row = lax.broadcasted_iota(jnp.int32, z.shape, 0)
    z1 = jnp.where(row == 0, z_prev[7:8, :], pltpu.roll(z, 1, axis=0))
    z2 = jnp.where(row == 0, z_prev[6:7, :],
                   jnp.where(row == 1, z_prev[7:8, :], pltpu.roll(z, 2, axis=0)))
    cw = cw_ref[...]
    y_sc = b_ref[...] * (cw[0:1, :] * z2 + cw[1:2, :] * z1 + cw[2:3, :] * z)

    merged = (jax.nn.sigmoid(gml_ref[...]) * _dot(yml_ref[...], wml_ref[...])
              + jax.nn.sigmoid(gsc_ref[...]) * _dot(y_sc.astype(BF16), wsc_ref[...])
              + jax.nn.sigmoid(gsb_ref[...]) * _dot_tn(ysbT_ref[...], wsb_ref[...]))
    o_ref[...] = x_ref[...] + _dot(merged.astype(BF16), wout_ref[...])


def _merge(x, y_ml, y_sbT, pn_f, conv_w, w_ml, w_sc, w_sb, w_out, *, seq,
           b_col, c_col, u_col, gml_col, gsc_col, gsb_col):
    T, D = x.shape
    tm = min(TM_MERGE, seq)
    prev = lambda col: (lambda i: (jnp.maximum(i * (tm // 8) - 1, 0), col))
    tile = lambda col: (lambda i: (i, col))
    const = lambda i: (0, 0)
    wspec = pl.BlockSpec((D, D), const)
    return pl.pallas_call(
        functools.partial(_merge_kernel, tiles_per_seq=seq // tm),
        grid=(T // tm,),
        in_specs=[
            pl.BlockSpec((tm, D), tile(0)),
            pl.BlockSpec((tm, D), tile(0)),
            pl.BlockSpec((D, tm), lambda i: (0, i)),
            pl.BlockSpec((tm, D), tile(b_col)),
            pl.BlockSpec((tm, D), tile(c_col)),
            pl.BlockSpec((tm, D), tile(u_col)),
            pl.BlockSpec((8, D), prev(c_col)),
            pl.BlockSpec((8, D), prev(u_col)),
            pl.BlockSpec((tm, D), tile(gml_col)),
            pl.BlockSpec((tm, D), tile(gsc_col)),
            pl.BlockSpec((tm, D), tile(gsb_col)),
            pl.BlockSpec((3, D), const),
            wspec, wspec, wspec, wspec,
        ],
        out_specs=pl.BlockSpec((tm, D), tile(0)),
        out_shape=jax.ShapeDtypeStruct((T, D), F32),
        compiler_params=_params("parallel"),
        name="merge",
    )(x, y_ml, y_sbT, pn_f, pn_f, pn_f, pn_f, pn_f, pn_f, pn_f, pn_f, conv_w,
      w_ml, w_sc, w_sb, w_out)


def _mlp_kernel(x_ref, g_ref, wup_ref, wdown_ref, gf_ref, o_ref, *, ff_chunk, final_norm):
    x = x_ref[...]
    h = _rmsnorm_rows(x, g_ref[...]).astype(BF16)
    acc = x
    for c in range(wup_ref.shape[1] // ff_chunk):
        cols = slice(c * ff_chunk, (c + 1) * ff_chunk)
        up = jnp.maximum(_dot(h, wup_ref[:, cols]), 0.0)
        acc = acc + _dot((up * up).astype(BF16), wdown_ref[cols, :])
    if final_norm:
        acc = _rmsnorm_rows(acc, gf_ref[...])
    o_ref[...] = acc


def _mlp(x, g, w_up, w_down, g_final, *, final_norm):
    T, D = x.shape
    F = w_up.shape[1]
    tm = min(TM_MLP, T)
    const = lambda i: (0, 0)
    return pl.pallas_call(
        functools.partial(_mlp_kernel, ff_chunk=1024, final_norm=final_norm),
        grid=(T // tm,),
        in_specs=[
            pl.BlockSpec((tm, D), lambda i: (i, 0)),
            pl.BlockSpec((1, D), const),
            pl.BlockSpec((D, F), const, pipeline_mode=pl.Buffered(1)),
            pl.BlockSpec((F, D), const, pipeline_mode=pl.Buffered(1)),
            pl.BlockSpec((1, D), const),
        ],
        out_specs=pl.BlockSpec((tm, D), lambda i: (i, 0)),
        out_shape=jax.ShapeDtypeStruct((T, D), F32),
        compiler_params=_params("parallel"),
        name="mlp",
    )(x, g, w_up, w_down, g_final)


def _layer(x, p, *, batch, seq, final_norm, g_final):
    D = D_MODEL
    ml_w = ML_HEADS * ML_HEAD_DIM
    w_in = p["w_in"]
    off = 0

    def take(n):
        nonlocal off
        w = w_in[:, off:off + n]
        off += n
        return w

    w_mq, w_mk, w_mv, w_mo = take(ml_w), take(ml_w), take(ml_w), take(ml_w)
    w_mi, w_mf = take(ML_HEADS), take(ML_HEADS)
    w_b, w_c, w_u = take(D), take(D), take(D)
    w_sq, w_sk, w_sv = take(D), take(D), take(D)
    w_gml, w_gsc, w_gsb = take(D), take(D), take(D)

    wn_b = jnp.concatenate([w_mq, w_mv, w_sk], axis=1).astype(BF16)
    wn_f = jnp.concatenate([w_mo, w_b, w_c, w_u, w_gml, w_gsc, w_gsb], axis=1).astype(BF16)
    wt = jnp.concatenate([w_mk.T * (ML_HEAD_DIM ** -0.5), w_sq.T * (SB_HEAD_DIM ** -0.5), w_sv.T],
                         axis=0).astype(BF16)
    wg = jnp.zeros((ML_HEADS, GATE_ROWS, D), F32)
    wg = wg.at[:, 0, :].set(w_mi.T).at[:, 1, :].set(w_mf.T).reshape(ML_HEADS * GATE_ROWS, D)
    bg = jnp.zeros((ML_HEADS, GATE_ROWS), F32)
    bg = bg.at[:, 0].set(p["b_if"][:ML_HEADS]).at[:, 1].set(p["b_if"][ML_HEADS:])
    bg = bg.reshape(ML_HEADS * GATE_ROWS, 1)

    g_mix = p["norm_mix_g"].reshape(1, D)
    pn_b = _norm_proj(x, g_mix, wn_b, BF16)
    pn_f = _norm_proj(x, g_mix, wn_f, F32)
    pt, gates = _norm_proj_t(x, g_mix, wt, wg, bg)

    y_ml = _mlstm(pn_b, pn_f, pt, gates, p["ml_norm_g"].reshape(1, ml_w), batch=batch, seq=seq,
                  q_col=0, v_col=ML_HEADS, o_col=0, kT_row=0)
    y_sbT = _stick_breaking(pn_b, pt, batch=batch, seq=seq,
                            k_col=(2 * ml_w) // (2 * SB_HEAD_DIM),
                            qT_row=ml_w // SB_HEAD_DIM,
                            vT_row=(ml_w + D) // SB_HEAD_DIM)
    x = _merge(x, y_ml, y_sbT, pn_f, p["conv_w"],
               p["w_ml_proj"].astype(BF16), p["w_sc_proj"].astype(BF16),
               p["w_sb_proj"].astype(BF16), p["w_out"].astype(BF16), seq=seq,
               b_col=1, c_col=2, u_col=3, gml_col=4, gsc_col=5, gsb_col=6)
    return _mlp(x, p["norm_mlp_g"].reshape(1, D), p["w_up"].astype(BF16), p["w_down"].astype(BF16),
                g_final.reshape(1, D), final_norm=final_norm)


def kernel(x, norm_mix_g, w_in, b_if, ml_norm_g, conv_w, w_ml_proj, w_sc_proj, w_sb_proj, w_out,
           norm_mlp_g, w_up, w_down, norm_final_g):
    batch, seq, D = x.shape
    depth = w_in.shape[0]
    xt = x.reshape(batch * seq, D)
    for l in range(depth):
        p = dict(norm_mix_g=norm_mix_g[l], w_in=w_in[l], b_if=b_if[l], ml_norm_g=ml_norm_g[l],
                 conv_w=conv_w[l], w_ml_proj=w_ml_proj[l], w_sc_proj=w_sc_proj[l],
                 w_sb_proj=w_sb_proj[l], w_out=w_out[l], norm_mlp_g=norm_mlp_g[l],
                 w_up=w_up[l], w_down=w_down[l])
        xt = _layer(xt, p, batch=batch, seq=seq, final_norm=(l == depth - 1), g_final=norm_final_g)
    return xt.reshape(batch, seq, D)
```

```python
import functools

import jax
import jax.numpy as jnp
from jax import lax
from jax.experimental import pallas as pl
from jax.experimental.pallas import tpu as pltpu

D_MODEL = 1024
ML_HEADS = 4
ML_HEAD_DIM = 256
SB_HEADS = 16
SB_HEAD_DIM = 64
D_FF = 4 * D_MODEL
EPS = 1e-6

VMEM_LIMIT_BYTES = 56 * 1024 * 1024

ML_CHUNK = 256
SB_BLOCK = 256
SB_GROUP = 4
TM_PROJ = 1024
TM_MERGE = 256
TM_MLP = 512
GATE_ROWS = 8

BF16 = jnp.bfloat16
F32 = jnp.float32


def _params(*sem):
    return pltpu.CompilerParams(dimension_semantics=sem, vmem_limit_bytes=VMEM_LIMIT_BYTES)


def _dot(a, b):
    return jnp.dot(a, b, preferred_element_type=F32)


def _dot_nt(a, b):
    return lax.dot_general(a, b, (((1,), (1,)), ((), ())), preferred_element_type=F32)


def _dot_tn(a, b):
    return lax.dot_general(a, b, (((0,), (0,)), ((), ())), preferred_element_type=F32)


def _rmsnorm_rows(x, g):
    ms = jnp.mean(x * x, axis=-1, keepdims=True)
    return x * lax.rsqrt(ms + EPS) * g


def _log_sigmoid(x):
    return jnp.minimum(x, 0.0) - jnp.log(1.0 + jnp.exp(-jnp.abs(x)))


def _split_bf16(x):
    hi = x.astype(BF16)
    lo = (x - hi.astype(F32)).astype(BF16)
    return hi, lo


def _norm_proj_kernel(x_ref, g_ref, w_ref, o_ref, h_ref):
    @pl.when(pl.program_id(1) == 0)
    def _():
        h_ref[...] = _rmsnorm_rows(x_ref[...], g_ref[...]).astype(BF16)

    o_ref[...] = _dot(h_ref[...], w_ref[...]).astype(o_ref.dtype)


def _norm_proj(x, g, w, out_dtype, tn=1024):
    T, D = x.shape
    N = w.shape[1]
    tm = min(TM_PROJ, T)
    return pl.pallas_call(
        _norm_proj_kernel,
        grid=(T // tm, N // tn),
        in_specs=[
            pl.BlockSpec((tm, D), lambda i, j: (i, 0)),
            pl.BlockSpec((1, D), lambda i, j: (0, 0)),
            pl.BlockSpec((D, tn), lambda i, j: (0, j)),
        ],
        out_specs=pl.BlockSpec((tm, tn), lambda i, j: (i, j)),
        out_shape=jax.ShapeDtypeStruct((T, N), out_dtype),
        scratch_shapes=[pltpu.VMEM((tm, D), BF16)],
        compiler_params=_params("parallel", "arbitrary"),
        name="norm_proj",
    )(x, g, w)


def _norm_proj_t_kernel(x_ref, g_ref, wt_ref, wg_hi_ref, wg_lo_ref, bg_ref, o_ref, gates_ref, h_ref):
    @pl.when(pl.program_id(1) == 0)
    def _():
        h = _rmsnorm_rows(x_ref[...], g_ref[...])
        h_hi, h_lo = _split_bf16(h)
        h_ref[...] = h_hi
        wg_hi = wg_hi_ref[...]
        gates = _dot_nt(wg_hi, h_hi) + (_dot_nt(wg_hi, h_lo) + _dot_nt(wg_lo_ref[...], h_hi))
        gates_ref[...] = gates + bg_ref[...]

    o_ref[...] = _dot_nt(wt_ref[...], h_ref[...]).astype(o_ref.dtype)


def _norm_proj_t(x, g, wt, wg, bg, tn=1024):
    T, D = x.shape
    N = wt.shape[0]
    G = wg.shape[0]
    tm = min(TM_PROJ, T)
    wg_hi, wg_lo = _split_bf16(wg)
    return pl.pallas_call(
        _norm_proj_t_kernel,
        grid=(T // tm, N // tn),
        in_specs=[
            pl.BlockSpec((tm, D), lambda i, j: (i, 0)),
            pl.BlockSpec((1, D), lambda i, j: (0, 0)),
            pl.BlockSpec((tn, D), lambda i, j: (j, 0)),
            pl.BlockSpec((G, D), lambda i, j: (0, 0)),
            pl.BlockSpec((G, D), lambda i, j: (0, 0)),
            pl.BlockSpec((G, 1), lambda i, j: (0, 0)),
        ],
        out_specs=[
            pl.BlockSpec((tn, tm), lambda i, j: (j, i)),
            pl.BlockSpec((G, tm), lambda i, j: (0, i)),
        ],
        out_shape=[
            jax.ShapeDtypeStruct((N, T), BF16),
            jax.ShapeDtypeStruct((G, T), F32),
        ],
        scratch_shapes=[pltpu.VMEM((tm, D), BF16)],
        compiler_params=_params("parallel", "arbitrary"),
        name="norm_proj_t",
    )(x, g, wt, wg_hi, wg_lo, bg)


def _cumsum_lanes(x):
    n = x.shape[-1]
    lane = lax.broadcasted_iota(jnp.int32, x.shape, x.ndim - 1)
    k = 1
    while k < n:
        x = x + jnp.where(lane >= k, pltpu.roll(x, k, axis=x.ndim - 1), 0.0)
        k *= 2
    return x


def _mlstm_kernel(q_ref, kT_ref, v_ref, o_ref, gates_ref, gain_ref, y_ref, c_ref, n_ref, m_ref):
    @pl.when(pl.program_id(2) == 0)
    def _():
        c_ref[...] = jnp.zeros_like(c_ref)
        n_ref[...] = jnp.zeros_like(n_ref)
        m_ref[...] = jnp.zeros_like(m_ref)

    L, dh = q_ref.shape
    q = q_ref[...]
    kT = kT_ref[...]
    v = v_ref[...]
    gates = gates_ref[...]
    m_prev = m_ref[...]

    i_row = gates[0:1, :]
    lf_row = _log_sigmoid(gates)[1:2, :]
    b_row = _cumsum_lanes(_log_sigmoid(gates))[1:2, :]
    t_idx = lax.broadcasted_iota(jnp.int32, (L, L), 0)
    s_idx = lax.broadcasted_iota(jnp.int32, (L, L), 1)
    causal = s_idx <= t_idx
    b_col = jnp.sum(jnp.where(causal, lf_row, 0.0), axis=1, keepdims=True)

    dmat = jnp.where(causal, b_col - b_row + i_row, -jnp.inf)
    inter = b_col + m_prev
    m_t = jnp.maximum(inter, jnp.max(dmat, axis=1, keepdims=True))
    w_intra = jnp.exp(dmat - m_t)
    a_inter = jnp.exp(inter - m_t)

    s = _dot(q, kT) * w_intra
    c_prev = c_ref[...]
    n_prev = n_ref[...]
    n_mat = jnp.broadcast_to(n_prev, (dh, 128)).astype(BF16)
    nq = _dot(q, n_mat)[:, 0:1]
    num = _dot(s.astype(BF16), v) + a_inter * _dot(q, c_prev.astype(BF16))
    den = jnp.sum(s, axis=1, keepdims=True) + a_inter * nq
    h = num / jnp.maximum(jnp.abs(den), jnp.exp(-m_t))

    b_last = b_row[:, L - 1:L]
    g_row = b_last - b_row + i_row
    m_new = jnp.maximum(b_last + m_prev, jnp.max(g_row, axis=1, keepdims=True))
    decay = jnp.exp(b_last + m_prev - m_new)
    w_state = jnp.exp(g_row - m_new)
    kw = kT.astype(F32) * w_state
    c_ref[...] = decay * c_prev + _dot(kw.astype(BF16), v)
    n_ref[...] = decay * n_prev + jnp.sum(kw, axis=1, keepdims=True)
    m_ref[...] = m_new

    hg = jax.nn.sigmoid(o_ref[...]) * h
    ms = jnp.mean(hg * hg, axis=-1, keepdims=True)
    y_ref[...] = (hg * lax.rsqrt(ms + EPS) * gain_ref[...]).astype(y_ref.dtype)


def _mlstm(pn_b, pn_f, pt, gates, gain, *, batch, seq, q_col, v_col, o_col, kT_row):
    T = batch * seq
    L = min(ML_CHUNK, seq)
    nc = seq // L
    dh = ML_HEAD_DIM
    tok = lambda b, h, c: b * nc + c
    return pl.pallas_call(
        _mlstm_kernel,
        grid=(batch, ML_HEADS, nc),
        in_specs=[
            pl.BlockSpec((L, dh), lambda b, h, c: (tok(b, h, c), q_col + h)),
            pl.BlockSpec((dh, L), lambda b, h, c: (kT_row + h, tok(b, h, c))),
            pl.BlockSpec((L, dh), lambda b, h, c: (tok(b, h, c), v_col + h)),
            pl.BlockSpec((L, dh), lambda b, h, c: (tok(b, h, c), o_col + h)),
            pl.BlockSpec((GATE_ROWS, L), lambda b, h, c: (h, tok(b, h, c))),
            pl.BlockSpec((1, dh), lambda b, h, c: (0, h)),
        ],
        out_specs=pl.BlockSpec((L, dh), lambda b, h, c: (tok(b, h, c), h)),
        out_shape=jax.ShapeDtypeStruct((T, ML_HEADS * dh), BF16),
        scratch_shapes=[
            pltpu.VMEM((dh, dh), F32),
            pltpu.VMEM((dh, 1), F32),
            pltpu.VMEM((1, 1), F32),
        ],
        compiler_params=_params("parallel", "parallel", "arbitrary"),
        name="mlstm",
    )(pn_b, pt, pn_b, pn_f, gates, gain)


def _sb_kernel(qT_ref, k_ref, vT_ref, o_ref, acc_ref, *, blk, heads):
    i = pl.program_id(2)
    d = SB_HEAD_DIM

    s_idx = lax.broadcasted_iota(jnp.int32, (blk, blk), 0)
    t_idx = lax.broadcasted_iota(jnp.int32, (blk, blk), 1)
    later = (t_idx > s_idx)
    upper = jnp.where(later, 1.0, 0.0).astype(BF16)

    row_head = lax.broadcasted_iota(jnp.int32, (2 * d, blk), 0) // d
    q_pads = []
    for g in range(heads):
        q2 = qT_ref[(g // 2) * 2 * d:(g // 2 + 1) * 2 * d, :]
        q_pads.append(jnp.where(row_head == g % 2, q2, jnp.zeros_like(q2)))

    def visit(j, rs, diagonal):
        start = pl.multiple_of(j * blk, blk)
        z = [_dot(k_ref[pl.ds(start, blk), (g // 2) * 2 * d:(g // 2 + 1) * 2 * d], q_pads[g])
             for g in range(heads)]
        log_beta, log_1m, suffix = [], [], []
        for g in range(heads):
            lb = _log_sigmoid(z[g])
            l1 = lb - z[g]
            if diagonal:
                l1 = jnp.where(later, l1, 0.0)
            hi, lo = _split_bf16(l1)
            log_beta.append(lb)
            log_1m.append(l1[0:1, :])
            suffix.append(_dot(upper, hi) + _dot(upper, lo))
        new_rs = []
        for g in range(heads):
            a = jnp.exp(log_beta[g] + suffix[g] + rs[g])
            if diagonal:
                a = jnp.where(later, a, 0.0)
            contrib = _dot(vT_ref[g * d:(g + 1) * d, pl.ds(start, blk)], a.astype(BF16))
            if diagonal:
                acc_ref[g * d:(g + 1) * d, :] = contrib
            else:
                acc_ref[g * d:(g + 1) * d, :] += contrib
            new_rs.append(rs[g] + (suffix[g][0:1, :] + log_1m[g]))
        return tuple(new_rs)

    rs = visit(i, tuple(jnp.zeros((1, blk), F32) for _ in range(heads)), True)
    lax.fori_loop(0, i, lambda n, c: visit(i - 1 - n, c, False), rs)
    o_ref[...] = acc_ref[...].astype(o_ref.dtype)


def _stick_breaking(pn_b, pt, *, batch, seq, k_col, qT_row, vT_row):
    T = batch * seq
    blk = min(SB_BLOCK, seq)
    nq = seq // blk
    gd = SB_GROUP * SB_HEAD_DIM
    return pl.pallas_call(
        functools.partial(_sb_kernel, blk=blk, heads=SB_GROUP),
        grid=(batch, SB_HEADS // SB_GROUP, nq),
        in_specs=[
            pl.BlockSpec((gd, blk), lambda b, h, i: (qT_row + h, b * nq + i)),
            pl.BlockSpec((seq, gd), lambda b, h, i: (b, k_col + h)),
            pl.BlockSpec((gd, seq), lambda b, h, i: (vT_row + h, b)),
        ],
        out_specs=pl.BlockSpec((gd, blk), lambda b, h, i: (h, b * nq + i)),
        out_shape=jax.ShapeDtypeStruct((SB_HEADS * SB_HEAD_DIM, T), BF16),
        scratch_shapes=[pltpu.VMEM((gd, blk), F32)],
        compiler_params=_params("parallel", "parallel", "arbitrary"),
        name="stick_breaking",
    )(pt, pn_b, pt)


def _merge_kernel(x_ref, yml_ref, ysbT_ref, b_ref, c_ref, u_ref, cp_ref, up_ref,
                  gml_ref, gsc_ref, gsb_ref, cw_ref, wml_ref, wsc_ref, wsb_ref, wout_ref,
                  o_ref, *, tiles_per_seq):
    tm = x_ref.shape[0]
    first = (pl.program_id(0) % tiles_per_seq) == 0
    z = c_ref[...] * u_ref[...]
    z_prev = jnp.where(first, 0.0, cp_ref[...] * up_ref[...])
    row = lax.broadcasted_iota(jnp.int32, z.shape, 0)
    z1 = jnp.where(row == 0, z_prev[7:8, :], pltpu.roll(z, 1, axis=0))
    z2 = jnp.where(row == 0, z_prev[6:7, :],
                   jnp.where(row == 1, z_prev[7:8, :], pltpu.roll(z, 2, axis=0)))
    cw = cw_ref[...]
    y_sc = b_ref[...] * (cw[0:1, :] * z2 + cw[1:2, :] * z1 + cw[2:3, :] * z)

    merged = (jax.nn.sigmoid(gml_ref[...]) * _dot(yml_ref[...], wml_ref[...])
              + jax.nn.sigmoid(gsc_ref[...]) * _dot(y_sc.astype(BF16), wsc_ref[...])
              + jax.nn.sigmoid(gsb_ref[...]) * _dot_tn(ysbT_ref[...], wsb_ref[...]))
    o_ref[...] = x_ref[...] + _dot(merged.astype(BF16), wout_ref[...])


def _merge(x, y_ml, y_sbT, pn_f, conv_w, w_ml, w_sc, w_sb, w_out, *, seq,
           b_col, c_col, u_col, gml_col, gsc_col, gsb_col):
    T, D = x.shape
    tm = min(TM_MERGE, seq)
    prev = lambda col: (lambda i: (jnp.maximum(i * (tm // 8) - 1, 0), col))
    tile = lambda col: (lambda i: (i, col))
    const = lambda i: (0, 0)
    wspec = pl.BlockSpec((D, D), const)
    return pl.pallas_call(
        functools.partial(_merge_kernel, tiles_per_seq=seq // tm),
        grid=(T // tm,),
        in_specs=[
            pl.BlockSpec((tm, D), tile(0)),
            pl.BlockSpec((tm, D), tile(0)),
            pl.BlockSpec((D, tm), lambda i: (0, i)),
            pl.BlockSpec((tm, D), tile(b_col)),
            pl.BlockSpec((tm, D), tile(c_col)),
            pl.BlockSpec((tm, D), tile(u_col)),
            pl.BlockSpec((8, D), prev(c_col)),
            pl.BlockSpec((8, D), prev(u_col)),
            pl.BlockSpec((tm, D), tile(gml_col)),
            pl.BlockSpec((tm, D), tile(gsc_col)),
            pl.BlockSpec((tm, D), tile(gsb_col)),
            pl.BlockSpec((3, D), const),
            wspec, wspec, wspec, wspec,
        ],
        out_specs=pl.BlockSpec((tm, D), tile(0)),
        out_shape=jax.ShapeDtypeStruct((T, D), F32),
        compiler_params=_params("parallel"),
        name="merge",
    )(x, y_ml, y_sbT, pn_f, pn_f, pn_f, pn_f, pn_f, pn_f, pn_f, pn_f, conv_w,
      w_ml, w_sc, w_sb, w_out)


def _mlp_kernel(x_ref, g_ref, wup_ref, wdown_ref, gf_ref, o_ref, *, ff_chunk, final_norm):
    x = x_ref[...]
    h = _rmsnorm_rows(x, g_ref[...]).astype(BF16)
    acc = x
    for c in range(wup_ref.shape[1] // ff_chunk):
        cols = slice(c * ff_chunk, (c + 1) * ff_chunk)
        up = jnp.maximum(_dot(h, wup_ref[:, cols]), 0.0)
        acc = acc + _dot((up * up).astype(BF16), wdown_ref[cols, :])
    if final_norm:
        acc = _rmsnorm_rows(acc, gf_ref[...])
    o_ref[...] = acc


def _mlp(x, g, w_up, w_down, g_final, *, final_norm):
    T, D = x.shape
    F = w_up.shape[1]
    tm = min(TM_MLP, T)
    const = lambda i: (0, 0)
    return pl.pallas_call(
        functools.partial(_mlp_kernel, ff_chunk=1024, final_norm=final_norm),
        grid=(T // tm,),
        in_specs=[
            pl.BlockSpec((tm, D), lambda i: (i, 0)),
            pl.BlockSpec((1, D), const),
            pl.BlockSpec((D, F), const, pipeline_mode=pl.Buffered(1)),
            pl.BlockSpec((F, D), const, pipeline_mode=pl.Buffered(1)),
            pl.BlockSpec((1, D), const),
        ],
        out_specs=pl.BlockSpec((tm, D), lambda i: (i, 0)),
        out_shape=jax.ShapeDtypeStruct((T, D), F32),
        compiler_params=_params("parallel"),
        name="mlp",
    )(x, g, w_up, w_down, g_final)


def _layer(x, p, *, batch, seq, final_norm, g_final):
    D = D_MODEL
    ml_w = ML_HEADS * ML_HEAD_DIM
    w_in = p["w_in"]
    off = 0

    def take(n):
        nonlocal off
        w = w_in[:, off:off + n]
        off += n
        return w

    w_mq, w_mk, w_mv, w_mo = take(ml_w), take(ml_w), take(ml_w), take(ml_w)
    w_mi, w_mf = take(ML_HEADS), take(ML_HEADS)
    w_b, w_c, w_u = take(D), take(D), take(D)
    w_sq, w_sk, w_sv = take(D), take(D), take(D)
    w_gml, w_gsc, w_gsb = take(D), take(D), take(D)

    wn_b = jnp.concatenate([w_mq, w_mv, w_sk], axis=1).astype(BF16)
    wn_f = jnp.concatenate([w_mo, w_b, w_c, w_u, w_gml, w_gsc, w_gsb], axis=1).astype(BF16)
    wt = jnp.concatenate([w_mk.T * (ML_HEAD_DIM ** -0.5), w_sq.T * (SB_HEAD_DIM ** -0.5), w_sv.T],
                         axis=0).astype(BF16)
    wg = jnp.zeros((ML_HEADS, GATE_ROWS, D), F32)
    wg = wg.at[:, 0, :].set(w_mi.T).at[:, 1, :].set(w_mf.T).reshape(ML_HEADS * GATE_ROWS, D)
    bg = jnp.zeros((ML_HEADS, GATE_ROWS), F32)
    bg = bg.at[:, 0].set(p["b_if"][:ML_HEADS]).at[:, 1].set(p["b_if"][ML_HEADS:])
    bg = bg.reshape(ML_HEADS * GATE_ROWS, 1)

    g_mix = p["norm_mix_g"].reshape(1, D)
    pn_b = _norm_proj(x, g_mix, wn_b, BF16)
    pn_f = _norm_proj(x, g_mix, wn_f, F32)
    pt, gates = _norm_proj_t(x, g_mix, wt, wg, bg)

    y_ml = _mlstm(pn_b, pn_f, pt, gates, p["ml_norm_g"].reshape(1, ml_w), batch=batch, seq=seq,
                  q_col=0, v_col=ML_HEADS, o_col=0, kT_row=0)
    y_sbT = _stick_breaking(pn_b, pt, batch=batch, seq=seq,
                            k_col=(2 * ml_w) // (SB_GROUP * SB_HEAD_DIM),
                            qT_row=ml_w // (SB_GROUP * SB_HEAD_DIM),
                            vT_row=(ml_w + D) // (SB_GROUP * SB_HEAD_DIM))
    x = _merge(x, y_ml, y_sbT, pn_f, p["conv_w"],
               p["w_ml_proj"].astype(BF16), p["w_sc_proj"].astype(BF16),
               p["w_sb_proj"].astype(BF16), p["w_out"].astype(BF16), seq=seq,
               b_col=1, c_col=2, u_col=3, gml_col=4, gsc_col=5, gsb_col=6)
    return _mlp(x, p["norm_mlp_g"].reshape(1, D), p["w_up"].astype(BF16), p["w_down"].astype(BF16),
                g_final.reshape(1, D), final_norm=final_norm)


def kernel(x, norm_mix_g, w_in, b_if, ml_norm_g, conv_w, w_ml_proj, w_sc_proj, w_sb_proj, w_out,
           norm_mlp_g, w_up, w_down, norm_final_g):
    batch, seq, D = x.shape
    depth = w_in.shape[0]
    xt = x.reshape(batch * seq, D)
    for l in range(depth):
        p = dict(norm_mix_g=norm_mix_g[l], w_in=w_in[l], b_if=b_if[l], ml_norm_g=ml_norm_g[l],
                 conv_w=conv_w[l], w_ml_proj=w_ml_proj[l], w_sc_proj=w_sc_proj[l],
                 w_sb_proj=w_sb_proj[l], w_out=w_out[l], norm_mlp_g=norm_mlp_g[l],
                 w_up=w_up[l], w_down=w_down[l])
        xt = _layer(xt, p, batch=batch, seq=seq, final_norm=(l == depth - 1), g_final=norm_final_g)
    return xt.reshape(batch, seq, D)
```

```python
import functools

import jax
import jax.numpy as jnp
from jax import lax
from jax.experimental import pallas as pl
from jax.experimental.pallas import tpu as pltpu

D_MODEL = 1024
ML_HEADS = 4
ML_HEAD_DIM = 256
SB_HEADS = 16
SB_HEAD_DIM = 64
D_FF = 4 * D_MODEL
EPS = 1e-6

VMEM_LIMIT_BYTES = 56 * 1024 * 1024

ML_CHUNK = 256
SB_BLOCK = 256
SB_GROUP = 4
SB_DEAD_LOG = -105.0
TM_PROJ = 1024
TM_MERGE = 256
TM_MLP = 512
GATE_ROWS = 8

BF16 = jnp.bfloat16
F32 = jnp.float32


def _params(*sem):
    return pltpu.CompilerParams(dimension_semantics=sem, vmem_limit_bytes=VMEM_LIMIT_BYTES)


def _dot(a, b):
    return jnp.dot(a, b, preferred_element_type=F32)


def _dot_nt(a, b):
    return lax.dot_general(a, b, (((1,), (1,)), ((), ())), preferred_element_type=F32)


def _dot_tn(a, b):
    return lax.dot_general(a, b, (((0,), (0,)), ((), ())), preferred_element_type=F32)


def _rmsnorm_rows(x, g):
    ms = jnp.mean(x * x, axis=-1, keepdims=True)
    return x * lax.rsqrt(ms + EPS) * g


def _log_sigmoid(x):
    return jnp.minimum(x, 0.0) - jnp.log(1.0 + jnp.exp(-jnp.abs(x)))


def _split_bf16(x):
    hi = x.astype(BF16)
    lo = (x - hi.astype(F32)).astype(BF16)
    return hi, lo


def _norm_proj_kernel(x_ref, g_ref, w_ref, o_ref, h_ref):
    @pl.when(pl.program_id(1) == 0)
    def _():
        h_ref[...] = _rmsnorm_rows(x_ref[...], g_ref[...]).astype(BF16)

    o_ref[...] = _dot(h_ref[...], w_ref[...]).astype(o_ref.dtype)


def _norm_proj(x, g, w, out_dtype, tn=1024):
    T, D = x.shape
    N = w.shape[1]
    tm = min(TM_PROJ, T)
    return pl.pallas_call(
        _norm_proj_kernel,
        grid=(T // tm, N // tn),
        in_specs=[
            pl.BlockSpec((tm, D), lambda i, j: (i, 0)),
            pl.BlockSpec((1, D), lambda i, j: (0, 0)),
            pl.BlockSpec((D, tn), lambda i, j: (0, j)),
        ],
        out_specs=pl.BlockSpec((tm, tn), lambda i, j: (i, j)),
        out_shape=jax.ShapeDtypeStruct((T, N), out_dtype),
        scratch_shapes=[pltpu.VMEM((tm, D), BF16)],
        compiler_params=_params("parallel", "arbitrary"),
        name="norm_proj",
    )(x, g, w)


def _norm_proj_t_kernel(x_ref, g_ref, wt_ref, wg_hi_ref, wg_lo_ref, bg_ref, o_ref, gates_ref, h_ref):
    @pl.when(pl.program_id(1) == 0)
    def _():
        h = _rmsnorm_rows(x_ref[...], g_ref[...])
        h_hi, h_lo = _split_bf16(h)
        h_ref[...] = h_hi
        wg_hi = wg_hi_ref[...]
        gates = _dot_nt(wg_hi, h_hi) + (_dot_nt(wg_hi, h_lo) + _dot_nt(wg_lo_ref[...], h_hi))
        gates_ref[...] = gates + bg_ref[...]

    o_ref[...] = _dot_nt(wt_ref[...], h_ref[...]).astype(o_ref.dtype)


def _norm_proj_t(x, g, wt, wg, bg, tn=1024):
    T, D = x.shape
    N = wt.shape[0]
    G = wg.shape[0]
    tm = min(TM_PROJ, T)
    wg_hi, wg_lo = _split_bf16(wg)
    return pl.pallas_call(
        _norm_proj_t_kernel,
        grid=(T // tm, N // tn),
        in_specs=[
            pl.BlockSpec((tm, D), lambda i, j: (i, 0)),
            pl.BlockSpec((1, D), lambda i, j: (0, 0)),
            pl.BlockSpec((tn, D), lambda i, j: (j, 0)),
            pl.BlockSpec((G, D), lambda i, j: (0, 0)),
            pl.BlockSpec((G, D), lambda i, j: (0, 0)),
            pl.BlockSpec((G, 1), lambda i, j: (0, 0)),
        ],
        out_specs=[
            pl.BlockSpec((tn, tm), lambda i, j: (j, i)),
            pl.BlockSpec((G, tm), lambda i, j: (0, i)),
        ],
        out_shape=[
            jax.ShapeDtypeStruct((N, T), BF16),
            jax.ShapeDtypeStruct((G, T), F32),
        ],
        scratch_shapes=[pltpu.VMEM((tm, D), BF16)],
        compiler_params=_params("parallel", "arbitrary"),
        name="norm_proj_t",
    )(x, g, wt, wg_hi, wg_lo, bg)


def _cumsum_lanes(x):
    n = x.shape[-1]
    lane = lax.broadcasted_iota(jnp.int32, x.shape, x.ndim - 1)
    k = 1
    while k < n:
        x = x + jnp.where(lane >= k, pltpu.roll(x, k, axis=x.ndim - 1), 0.0)
        k *= 2
    return x


def _mlstm_kernel(q_ref, kT_ref, v_ref, o_ref, gates_ref, gain_ref, y_ref, c_ref, n_ref, m_ref):
    @pl.when(pl.program_id(2) == 0)
    def _():
        c_ref[...] = jnp.zeros_like(c_ref)
        n_ref[...] = jnp.zeros_like(n_ref)
        m_ref[...] = jnp.zeros_like(m_ref)

    L, dh = q_ref.shape
    q = q_ref[...]
    kT = kT_ref[...]
    v = v_ref[...]
    gates = gates_ref[...]
    m_prev = m_ref[...]

    i_row = gates[0:1, :]
    lf_row = _log_sigmoid(gates)[1:2, :]
    b_row = _cumsum_lanes(_log_sigmoid(gates))[1:2, :]
    t_idx = lax.broadcasted_iota(jnp.int32, (L, L), 0)
    s_idx = lax.broadcasted_iota(jnp.int32, (L, L), 1)
    causal = s_idx <= t_idx
    b_col = jnp.sum(jnp.where(causal, lf_row, 0.0), axis=1, keepdims=True)

    dmat = jnp.where(causal, b_col - b_row + i_row, -jnp.inf)
    inter = b_col + m_prev
    m_t = jnp.maximum(inter, jnp.max(dmat, axis=1, keepdims=True))
    w_intra = jnp.exp(dmat - m_t)
    a_inter = jnp.exp(inter - m_t)

    s = _dot(q, kT) * w_intra
    c_prev = c_ref[...]
    n_prev = n_ref[...]
    n_mat = jnp.broadcast_to(n_prev, (dh, 128)).astype(BF16)
    nq = _dot(q, n_mat)[:, 0:1]
    num = _dot(s.astype(BF16), v) + a_inter * _dot(q, c_prev.astype(BF16))
    den = jnp.sum(s, axis=1, keepdims=True) + a_inter * nq
    h = num / jnp.maximum(jnp.abs(den), jnp.exp(-m_t))

    b_last = b_row[:, L - 1:L]
    g_row = b_last - b_row + i_row
    m_new = jnp.maximum(b_last + m_prev, jnp.max(g_row, axis=1, keepdims=True))
    decay = jnp.exp(b_last + m_prev - m_new)
    w_state = jnp.exp(g_row - m_new)
    kw = kT.astype(F32) * w_state
    c_ref[...] = decay * c_prev + _dot(kw.astype(BF16), v)
    n_ref[...] = decay * n_prev + jnp.sum(kw, axis=1, keepdims=True)
    m_ref[...] = m_new

    hg = jax.nn.sigmoid(o_ref[...]) * h
    ms = jnp.mean(hg * hg, axis=-1, keepdims=True)
    y_ref[...] = (hg * lax.rsqrt(ms + EPS) * gain_ref[...]).astype(y_ref.dtype)


def _mlstm(pn_b, pn_f, pt, gates, gain, *, batch, seq, q_col, v_col, o_col, kT_row):
    T = batch * seq
    L = min(ML_CHUNK, seq)
    nc = seq // L
    dh = ML_HEAD_DIM
    tok = lambda b, h, c: b * nc + c
    return pl.pallas_call(
        _mlstm_kernel,
        grid=(batch, ML_HEADS, nc),
        in_specs=[
            pl.BlockSpec((L, dh), lambda b, h, c: (tok(b, h, c), q_col + h)),
            pl.BlockSpec((dh, L), lambda b, h, c: (kT_row + h, tok(b, h, c))),
            pl.BlockSpec((L, dh), lambda b, h, c: (tok(b, h, c), v_col + h)),
            pl.BlockSpec((L, dh), lambda b, h, c: (tok(b, h, c), o_col + h)),
            pl.BlockSpec((GATE_ROWS, L), lambda b, h, c: (h, tok(b, h, c))),
            pl.BlockSpec((1, dh), lambda b, h, c: (0, h)),
        ],
        out_specs=pl.BlockSpec((L, dh), lambda b, h, c: (tok(b, h, c), h)),
        out_shape=jax.ShapeDtypeStruct((T, ML_HEADS * dh), BF16),
        scratch_shapes=[
            pltpu.VMEM((dh, dh), F32),
            pltpu.VMEM((dh, 1), F32),
            pltpu.VMEM((1, 1), F32),
        ],
        compiler_params=_params("parallel", "parallel", "arbitrary"),
        name="mlstm",
    )(pn_b, pt, pn_b, pn_f, gates, gain)


def _sb_kernel(qT_ref, k_ref, vT_ref, o_ref, acc_ref, *, blk, heads):
    i = pl.program_id(2)
    d = SB_HEAD_DIM

    s_idx = lax.broadcasted_iota(jnp.int32, (blk, blk), 0)
    t_idx = lax.broadcasted_iota(jnp.int32, (blk, blk), 1)
    later = (t_idx > s_idx)
    upper = jnp.where(later, 1.0, 0.0).astype(BF16)

    row_head = lax.broadcasted_iota(jnp.int32, (2 * d, blk), 0) // d
    q_pads = []
    for g in range(heads):
        q2 = qT_ref[(g // 2) * 2 * d:(g // 2 + 1) * 2 * d, :]
        q_pads.append(jnp.where(row_head == g % 2, q2, jnp.zeros_like(q2)))

    def visit(j, rs, diagonal):
        start = pl.multiple_of(j * blk, blk)
        z = [_dot(k_ref[pl.ds(start, blk), (g // 2) * 2 * d:(g // 2 + 1) * 2 * d], q_pads[g])
             for g in range(heads)]
        log_beta, log_1m, suffix = [], [], []
        for g in range(heads):
            lb = _log_sigmoid(z[g])
            l1 = lb - z[g]
            if diagonal:
                l1 = jnp.where(later, l1, 0.0)
            hi, lo = _split_bf16(l1)
            log_beta.append(lb)
            log_1m.append(l1[0:1, :])
            suffix.append(_dot(upper, hi) + _dot(upper, lo))
        new_rs = []
        for g in range(heads):
            a = jnp.exp(log_beta[g] + suffix[g] + rs[g])
            if diagonal:
                a = jnp.where(later, a, 0.0)
            contrib = _dot(vT_ref[g * d:(g + 1) * d, pl.ds(start, blk)], a.astype(BF16))
            if diagonal:
                acc_ref[g * d:(g + 1) * d, :] = contrib
            else:
                acc_ref[g * d:(g + 1) * d, :] += contrib
            new_rs.append(rs[g] + (suffix[g][0:1, :] + log_1m[g]))
        return tuple(new_rs)

    def live(rs):
        r_max = functools.reduce(jnp.maximum, rs)
        return jnp.max(r_max) > SB_DEAD_LOG

    def cond(c):
        n, rs = c
        return jnp.logical_and(n < i, live(rs))

    def body(c):
        n, rs = c
        return n + 1, visit(i - 1 - n, rs, False)

    rs = visit(i, tuple(jnp.zeros((1, blk), F32) for _ in range(heads)), True)
    lax.while_loop(cond, body, (jnp.int32(0), rs))
    o_ref[...] = acc_ref[...].astype(o_ref.dtype)


def _stick_breaking(pn_b, pt, *, batch, seq, k_col, qT_row, vT_row):
    T = batch * seq
    blk = min(SB_BLOCK, seq)
    nq = seq // blk
    gd = SB_GROUP * SB_HEAD_DIM
    return pl.pallas_call(
        functools.partial(_sb_kernel, blk=blk, heads=SB_GROUP),
        grid=(batch, SB_HEADS // SB_GROUP, nq),
        in_specs=[
            pl.BlockSpec((gd, blk), lambda b, h, i: (qT_row + h, b * nq + i)),
            pl.BlockSpec((seq, gd), lambda b, h, i: (b, k_col + h)),
            pl.BlockSpec((gd, seq), lambda b, h, i: (vT_row + h, b)),
        ],
        out_specs=pl.BlockSpec((gd, blk), lambda b, h, i: (h, b * nq + i)),
        out_shape=jax.ShapeDtypeStruct((SB_HEADS * SB_HEAD_DIM, T), BF16),
        scratch_shapes=[pltpu.VMEM((gd, blk), F32)],
        compiler_params=_params("parallel", "parallel", "arbitrary"),
        name="stick_breaking",
    )(pt, pn_b, pt)


def _merge_kernel(x_ref, yml_ref, ysbT_ref, b_ref, c_ref, u_ref, cp_ref, up_ref,
                  gml_ref, gsc_ref, gsb_ref, cw_ref, wml_ref, wsc_ref, wsb_ref, wout_ref,
                  o_ref, *, tiles_per_seq):
    tm = x_ref.shape[0]
    first = (pl.program_id(0) % tiles_per_seq) == 0
    z = c_ref[...] * u_ref[...]
    z_prev = jnp.where(first, 0.0, cp_ref[...] * up_ref[...])
    row = lax.broadcasted_iota(jnp.int32, z.shape, 0)
    z1 = jnp.where(row == 0, z_prev[7:8, :], pltpu.roll(z, 1, axis=0))
    z2 = jnp.where(row == 0, z_prev[6:7, :],
                   jnp.where(row == 1, z_prev[7:8, :], pltpu.roll(z, 2, axis=0)))
    cw = cw_ref[...]
    y_sc = b_ref[...] * (cw[0:1, :] * z2 + cw[1:2, :] * z1 + cw[2:3, :] * z)

    merged = (jax.nn.sigmoid(gml_ref[...]) * _dot(yml_ref[...], wml_ref[...])
              + jax.nn.sigmoid(gsc_ref[...]) * _dot(y_sc.astype(BF16), wsc_ref[...])
              + jax.nn.sigmoid(gsb_ref[...]) * _dot_tn(ysbT_ref[...], wsb_ref[...]))
    o_ref[...] = x_ref[...] + _dot(merged.astype(BF16), wout_ref[...])


def _merge(x, y_ml, y_sbT, pn_f, conv_w, w_ml, w_sc, w_sb, w_out, *, seq,
           b_col, c_col, u_col, gml_col, gsc_col, gsb_col):
    T, D = x.shape
    tm = min(TM_MERGE, seq)
    prev = lambda col: (lambda i: (jnp.maximum(i * (tm // 8) - 1, 0), col))
    tile = lambda col: (lambda i: (i, col))
    const = lambda i: (0, 0)
    wspec = pl.BlockSpec((D, D), const)
    return pl.pallas_call(
        functools.partial(_merge_kernel, tiles_per_seq=seq // tm),
        grid=(T // tm,),
        in_specs=[
            pl.BlockSpec((tm, D), tile(0)),
            pl.BlockSpec((tm, D), tile(0)),
            pl.BlockSpec((D, tm), lambda i: (0, i)),
            pl.BlockSpec((tm, D), tile(b_col)),
            pl.BlockSpec((tm, D), tile(c_col)),
            pl.BlockSpec((tm, D), tile(u_col)),
            pl.BlockSpec((8, D), prev(c_col)),
            pl.BlockSpec((8, D), prev(u_col)),
            pl.BlockSpec((tm, D), tile(gml_col)),
            pl.BlockSpec((tm, D), tile(gsc_col)),
            pl.BlockSpec((tm, D), tile(gsb_col)),
            pl.BlockSpec((3, D), const),
            wspec, wspec, wspec, wspec,
        ],
        out_specs=pl.BlockSpec((tm, D), tile(0)),
        out_shape=jax.ShapeDtypeStruct((T, D), F32),
        compiler_params=_params("parallel"),
        name="merge",
    )(x, y_ml, y_sbT, pn_f, pn_f, pn_f, pn_f, pn_f, pn_f, pn_f, pn_f, conv_w,
      w_ml, w_sc, w_sb, w_out)


def _mlp_kernel(x_ref, g_ref, wup_ref, wdown_ref, gf_ref, o_ref, *, ff_chunk, final_norm):
    x = x_ref[...]
    h = _rmsnorm_rows(x, g_ref[...]).astype(BF16)
    acc = x
    for c in range(wup_ref.shape[1] // ff_chunk):
        cols = slice(c * ff_chunk, (c + 1) * ff_chunk)
        up = jnp.maximum(_dot(h, wup_ref[:, cols]), 0.0)
        acc = acc + _dot((up * up).astype(BF16), wdown_ref[cols, :])
    if final_norm:
        acc = _rmsnorm_rows(acc, gf_ref[...])
    o_ref[...] = acc


def _mlp(x, g, w_up, w_down, g_final, *, final_norm):
    T, D = x.shape
    F = w_up.shape[1]
    tm = min(TM_MLP, T)
    const = lambda i: (0, 0)
    return pl.pallas_call(
        functools.partial(_mlp_kernel, ff_chunk=1024, final_norm=final_norm),
        grid=(T // tm,),
        in_specs=[
            pl.BlockSpec((tm, D), lambda i: (i, 0)),
            pl.BlockSpec((1, D), const),
            pl.BlockSpec((D, F), const, pipeline_mode=pl.Buffered(1)),
            pl.BlockSpec((F, D), const, pipeline_mode=pl.Buffered(1)),
            pl.BlockSpec((1, D), const),
        ],
        out_specs=pl.BlockSpec((tm, D), lambda i: (i, 0)),
        out_shape=jax.ShapeDtypeStruct((T, D), F32),
        compiler_params=_params("parallel"),
        name="mlp",
    )(x, g, w_up, w_down, g_final)


def _layer(x, p, *, batch, seq, final_norm, g_final):
    D = D_MODEL
    ml_w = ML_HEADS * ML_HEAD_DIM
    w_in = p["w_in"]
    off = 0

    def take(n):
        nonlocal off
        w = w_in[:, off:off + n]
        off += n
        return w

    w_mq, w_mk, w_mv, w_mo = take(ml_w), take(ml_w), take(ml_w), take(ml_w)
    w_mi, w_mf = take(ML_HEADS), take(ML_HEADS)
    w_b, w_c, w_u = take(D), take(D), take(D)
    w_sq, w_sk, w_sv = take(D), take(D), take(D)
    w_gml, w_gsc, w_gsb = take(D), take(D), take(D)

    wn_b = jnp.concatenate([w_mq, w_mv, w_sk], axis=1).astype(BF16)
    wn_f = jnp.concatenate([w_mo, w_b, w_c, w_u, w_gml, w_gsc, w_gsb], axis=1).astype(BF16)
    wt = jnp.concatenate([w_mk.T * (ML_HEAD_DIM ** -0.5), w_sq.T * (SB_HEAD_DIM ** -0.5), w_sv.T],
                         axis=0).astype(BF16)
    wg = jnp.zeros((ML_HEADS, GATE_ROWS, D), F32)
    wg = wg.at[:, 0, :].set(w_mi.T).at[:, 1, :].set(w_mf.T).reshape(ML_HEADS * GATE_ROWS, D)
    bg = jnp.zeros((ML_HEADS, GATE_ROWS), F32)
    bg = bg.at[:, 0].set(p["b_if"][:ML_HEADS]).at[:, 1].set(p["b_if"][ML_HEADS:])
    bg = bg.reshape(ML_HEADS * GATE_ROWS, 1)

    g_mix = p["norm_mix_g"].reshape(1, D)
    pn_b = _norm_proj(x, g_mix, wn_b, BF16)
    pn_f = _norm_proj(x, g_mix, wn_f, F32)
    pt, gates = _norm_proj_t(x, g_mix, wt, wg, bg)

    y_ml = _mlstm(pn_b, pn_f, pt, gates, p["ml_norm_g"].reshape(1, ml_w), batch=batch, seq=seq,
                  q_col=0, v_col=ML_HEADS, o_col=0, kT_row=0)
    y_sbT = _stick_breaking(pn_b, pt, batch=batch, seq=seq,
                            k_col=(2 * ml_w) // (SB_GROUP * SB_HEAD_DIM),
                            qT_row=ml_w // (SB_GROUP * SB_HEAD_DIM),
                            vT_row=(ml_w + D) // (SB_GROUP * SB_HEAD_DIM))
    x = _merge(x, y_ml, y_sbT, pn_f, p["conv_w"],
               p["w_ml_proj"].astype(BF16), p["w_sc_proj"].astype(BF16),
               p["w_sb_proj"].astype(BF16), p["w_out"].astype(BF16), seq=seq,
               b_col=1, c_col=2, u_col=3, gml_col=4, gsc_col=5, gsb_col=6)
    return _mlp(x, p["norm_mlp_g"].reshape(1, D), p["w_up"].astype(BF16), p["w_down"].astype(BF16),
                g_final.reshape(1, D), final_norm=final_norm)


def kernel(x, norm_mix_g, w_in, b_if, ml_norm_g, conv_w, w_ml_proj, w_sc_proj, w_sb_proj, w_out,
           norm_mlp_g, w_up, w_down, norm_final_g):
    batch, seq, D = x.shape
    depth = w_in.shape[0]
    xt = x.reshape(batch * seq, D)
    for l in range(depth):
        p = dict(norm_mix_g=norm_mix_g[l], w_in=w_in[l], b_if=b_if[l], ml_norm_g=ml_norm_g[l],
                 conv_w=conv_w[l], w_ml_proj=w_ml_proj[l], w_sc_proj=w_sc_proj[l],
                 w_sb_proj=w_sb_proj[l], w_out=w_out[l], norm_mlp_g=norm_mlp_g[l],
                 w_up=w_up[l], w_down=w_down[l])
        xt = _layer(xt, p, batch=batch, seq=seq, final_norm=(l == depth - 1), g_final=norm_final_g)
    return xt.reshape(batch, seq, D)
```

```python
import functools

import jax
import jax.numpy as jnp
from jax import lax
from jax.experimental import pallas as pl
from jax.experimental.pallas import tpu as pltpu

D_MODEL = 1024
ML_HEADS = 4
ML_HEAD_DIM = 256
SB_HEADS = 16
SB_HEAD_DIM = 64
D_FF = 4 * D_MODEL
EPS = 1e-6

VMEM_LIMIT_BYTES = 56 * 1024 * 1024

ML_CHUNK = 256
SB_BLOCK = 256
SB_GROUP = 4
SB_DEAD_LOG = -105.0
TM_PROJ = 1024
TM_MERGE = 512
TM_MLP = 512

BF16 = jnp.bfloat16
F32 = jnp.float32


def _params(*sem):
    return pltpu.CompilerParams(dimension_semantics=sem, vmem_limit_bytes=VMEM_LIMIT_BYTES)


def _dot(a, b):
    return jnp.dot(a, b, preferred_element_type=F32)


def _dot_nt(a, b):
    return lax.dot_general(a, b, (((1,), (1,)), ((), ())), preferred_element_type=F32)


def _dot_tn(a, b):
    return lax.dot_general(a, b, (((0,), (0,)), ((), ())), preferred_element_type=F32)


def _rmsnorm_rows(x, g):
    ms = jnp.mean(x * x, axis=-1, keepdims=True)
    return x * lax.rsqrt(ms + EPS) * g


def _log_sigmoid(x):
    return jnp.minimum(x, 0.0) - jnp.log(1.0 + jnp.exp(-jnp.abs(x)))


def _split_bf16(x):
    hi = x.astype(BF16)
    lo = (x - hi.astype(F32)).astype(BF16)
    return hi, lo


def _norm_proj_kernel(x_ref, g_ref, w_ref, o_ref, h_ref):
    @pl.when(pl.program_id(1) == 0)
    def _():
        h_ref[...] = _rmsnorm_rows(x_ref[...], g_ref[...]).astype(BF16)

    o_ref[...] = _dot(h_ref[...], w_ref[...]).astype(o_ref.dtype)


def _norm_proj(x, g, w, out_dtype, tn=1024):
    T, D = x.shape
    N = w.shape[1]
    tm = min(TM_PROJ, T)
    return pl.pallas_call(
        _norm_proj_kernel,
        grid=(T // tm, N // tn),
        in_specs=[
            pl.BlockSpec((tm, D), lambda i, j: (i, 0)),
            pl.BlockSpec((1, D), lambda i, j: (0, 0)),
            pl.BlockSpec((D, tn), lambda i, j: (0, j)),
        ],
        out_specs=pl.BlockSpec((tm, tn), lambda i, j: (i, j)),
        out_shape=jax.ShapeDtypeStruct((T, N), out_dtype),
        scratch_shapes=[pltpu.VMEM((tm, D), BF16)],
        compiler_params=_params("parallel", "arbitrary"),
        name="norm_proj",
    )(x, g, w)


def _norm_proj_t_kernel(x_ref, g_ref, wt_ref, wg_hi_ref, wg_lo_ref, bg_ref, o_ref, gates_ref, h_ref):
    @pl.when(pl.program_id(1) == 0)
    def _():
        h = _rmsnorm_rows(x_ref[...], g_ref[...])
        h_hi, h_lo = _split_bf16(h)
        h_ref[...] = h_hi
        wg_hi = wg_hi_ref[...]
        gates = _dot_nt(wg_hi, h_hi) + (_dot_nt(wg_hi, h_lo) + _dot_nt(wg_lo_ref[...], h_hi))
        gates_ref[...] = gates + bg_ref[...]

    o_ref[...] = _dot_nt(wt_ref[...], h_ref[...]).astype(o_ref.dtype)


def _norm_proj_t(x, g, wt, wg, bg, tn=1024):
    T, D = x.shape
    N = wt.shape[0]
    G = wg.shape[0]
    tm = min(TM_PROJ, T)
    wg_hi, wg_lo = _split_bf16(wg)
    return pl.pallas_call(
        _norm_proj_t_kernel,
        grid=(T // tm, N // tn),
        in_specs=[
            pl.BlockSpec((tm, D), lambda i, j: (i, 0)),
            pl.BlockSpec((1, D), lambda i, j: (0, 0)),
            pl.BlockSpec((tn, D), lambda i, j: (j, 0)),
            pl.BlockSpec((G, D), lambda i, j: (0, 0)),
            pl.BlockSpec((G, D), lambda i, j: (0, 0)),
            pl.BlockSpec((G, 1), lambda i, j: (0, 0)),
        ],
        out_specs=[
            pl.BlockSpec((tn, tm), lambda i, j: (j, i)),
            pl.BlockSpec((G, tm), lambda i, j: (0, i)),
        ],
        out_shape=[
            jax.ShapeDtypeStruct((N, T), BF16),
            jax.ShapeDtypeStruct((G, T), F32),
        ],
        scratch_shapes=[pltpu.VMEM((tm, D), BF16)],
        compiler_params=_params("parallel", "arbitrary"),
        name="norm_proj_t",
    )(x, g, wt, wg_hi, wg_lo, bg)


def _cumsum_lanes(x):
    n = x.shape[-1]
    lane = lax.broadcasted_iota(jnp.int32, x.shape, x.ndim - 1)
    k = 1
    while k < n:
        x = x + jnp.where(lane >= k, pltpu.roll(x, k, axis=x.ndim - 1), 0.0)
        k *= 2
    return x


def _mlstm_kernel(q_ref, kT_ref, v_ref, o_ref, gates_ref, gain_ref, y_ref, c_ref, n_ref, m_ref):
    @pl.when(pl.program_id(1) == 0)
    def _():
        c_ref[...] = jnp.zeros_like(c_ref)
        n_ref[...] = jnp.zeros_like(n_ref)
        m_ref[...] = jnp.zeros_like(m_ref)

    L = q_ref.shape[0]
    dh = ML_HEAD_DIM
    heads = range(ML_HEADS)
    cols = lambda g: slice(g * dh, (g + 1) * dh)
    t_idx = lax.broadcasted_iota(jnp.int32, (L, L), 0)
    s_idx = lax.broadcasted_iota(jnp.int32, (L, L), 1)
    causal = s_idx <= t_idx

    q = [q_ref[:, cols(g)] for g in heads]
    c_prev = [c_ref[g] for g in heads]
    n_prev = [n_ref[g] for g in heads]
    m_prev = [m_ref[g] for g in heads]
    qk = [_dot(q[g], kT_ref[cols(g), :]) for g in heads]
    qc = [_dot(q[g], c_prev[g].astype(BF16)) for g in heads]
    nq = [_dot(q[g], jnp.broadcast_to(n_prev[g], (dh, 128)).astype(BF16))[:, 0:1] for g in heads]

    lf_all = _log_sigmoid(gates_ref[...])
    b_all = _cumsum_lanes(lf_all)
    m_t, a_inter, s, num = [], [], [], []
    b_row, i_row = [], []
    for g in heads:
        i_row.append(gates_ref[g:g + 1, :])
        lf_row = lf_all[ML_HEADS + g:ML_HEADS + g + 1, :]
        b_row.append(b_all[ML_HEADS + g:ML_HEADS + g + 1, :])
        b_col = jnp.sum(jnp.where(causal, lf_row, 0.0), axis=1, keepdims=True)
        dmat = jnp.where(causal, b_col - b_row[g] + i_row[g], -jnp.inf)
        inter = b_col + m_prev[g]
        m_t.append(jnp.maximum(inter, jnp.max(dmat, axis=1, keepdims=True)))
        a_inter.append(jnp.exp(inter - m_t[g]))
        s.append(qk[g] * jnp.exp(dmat - m_t[g]))
        num.append(_dot(s[g].astype(BF16), v_ref[:, cols(g)]))

    for g in heads:
        b_last = b_row[g][:, L - 1:L]
        g_row = b_last - b_row[g] + i_row[g]
        m_new = jnp.maximum(b_last + m_prev[g], jnp.max(g_row, axis=1, keepdims=True))
        decay = jnp.exp(b_last + m_prev[g] - m_new)
        kw = kT_ref[cols(g), :].astype(F32) * jnp.exp(g_row - m_new)
        c_ref[g] = decay * c_prev[g] + _dot(kw.astype(BF16), v_ref[:, cols(g)])
        n_ref[g] = decay * n_prev[g] + jnp.sum(kw, axis=1, keepdims=True)
        m_ref[g] = m_new

    for g in heads:
        den = jnp.sum(s[g], axis=1, keepdims=True) + a_inter[g] * nq[g]
        h = (num[g] + a_inter[g] * qc[g]) / jnp.maximum(jnp.abs(den), jnp.exp(-m_t[g]))
        hg = jax.nn.sigmoid(o_ref[:, cols(g)]) * h
        ms = jnp.mean(hg * hg, axis=-1, keepdims=True)
        y_ref[:, cols(g)] = (hg * lax.rsqrt(ms + EPS) * gain_ref[:, cols(g)]).astype(y_ref.dtype)


def _mlstm(pn_b, pn_f, pt, gates, gain, *, batch, seq, q_col, v_col, o_col, kT_row):
    T = batch * seq
    L = min(ML_CHUNK, seq)
    nc = seq // L
    dh = ML_HEAD_DIM
    W = ML_HEADS * dh
    tok = lambda b, c: b * nc + c
    return pl.pallas_call(
        _mlstm_kernel,
        grid=(batch, nc),
        in_specs=[
            pl.BlockSpec((L, W), lambda b, c: (tok(b, c), q_col)),
            pl.BlockSpec((W, L), lambda b, c: (kT_row, tok(b, c))),
            pl.BlockSpec((L, W), lambda b, c: (tok(b, c), v_col)),
            pl.BlockSpec((L, W), lambda b, c: (tok(b, c), o_col)),
            pl.BlockSpec((2 * ML_HEADS, L), lambda b, c: (0, tok(b, c))),
            pl.BlockSpec((1, W), lambda b, c: (0, 0)),
        ],
        out_specs=pl.BlockSpec((L, W), lambda b, c: (tok(b, c), 0)),
        out_shape=jax.ShapeDtypeStruct((T, W), BF16),
        scratch_shapes=[
            pltpu.VMEM((ML_HEADS, dh, dh), F32),
            pltpu.VMEM((ML_HEADS, dh, 1), F32),
            pltpu.VMEM((ML_HEADS, 1, 1), F32),
        ],
        compiler_params=_params("parallel", "arbitrary"),
        name="mlstm",
    )(pn_b, pt, pn_b, pn_f, gates, gain)


def _sb_kernel(qT_ref, k_ref, vT_ref, o_ref, acc_ref, *, blk, heads):
    i = pl.program_id(2)
    d = SB_HEAD_DIM

    s_idx = lax.broadcasted_iota(jnp.int32, (blk, blk), 0)
    t_idx = lax.broadcasted_iota(jnp.int32, (blk, blk), 1)
    later = (t_idx > s_idx)
    upper = jnp.where(later, 1.0, 0.0).astype(BF16)

    row_head = lax.broadcasted_iota(jnp.int32, (2 * d, blk), 0) // d
    q_pads = []
    for g in range(heads):
        q2 = qT_ref[(g // 2) * 2 * d:(g // 2 + 1) * 2 * d, :]
        q_pads.append(jnp.where(row_head == g % 2, q2, jnp.zeros_like(q2)))

    def visit(j, rs, diagonal):
        start = pl.multiple_of(j * blk, blk)
        z = [_dot(k_ref[pl.ds(start, blk), (g // 2) * 2 * d:(g // 2 + 1) * 2 * d], q_pads[g])
             for g in range(heads)]
        log_beta, log_1m, suffix = [], [], []
        for g in range(heads):
            lb = _log_sigmoid(z[g])
            l1 = lb - z[g]
            if diagonal:
                l1 = jnp.where(later, l1, 0.0)
            hi, lo = _split_bf16(l1)
            log_beta.append(lb)
            log_1m.append(l1[0:1, :])
            suffix.append(_dot(upper, hi) + _dot(upper, lo))
        new_rs = []
        for g in range(heads):
            a = jnp.exp(log_beta[g] + suffix[g] + rs[g])
            if diagonal:
                a = jnp.where(later, a, 0.0)
            contrib = _dot(vT_ref[g * d:(g + 1) * d, pl.ds(start, blk)], a.astype(BF16))
            if diagonal:
                acc_ref[g * d:(g + 1) * d, :] = contrib
            else:
                acc_ref[g * d:(g + 1) * d, :] += contrib
            new_rs.append(rs[g] + (suffix[g][0:1, :] + log_1m[g]))
        return tuple(new_rs)

    def live(rs):
        r_max = functools.reduce(jnp.maximum, rs)
        return jnp.max(r_max) > SB_DEAD_LOG

    def cond(c):
        n, rs = c
        return jnp.logical_and(n < i, live(rs))

    def body(c):
        n, rs = c
        return n + 1, visit(i - 1 - n, rs, False)

    rs = visit(i, tuple(jnp.zeros((1, blk), F32) for _ in range(heads)), True)
    lax.while_loop(cond, body, (jnp.int32(0), rs))
    o_ref[...] = acc_ref[...].astype(o_ref.dtype)


def _stick_breaking(pn_b, pt, *, batch, seq, k_col, qT_row, vT_row):
    T = batch * seq
    blk = min(SB_BLOCK, seq)
    nq = seq // blk
    gd = SB_GROUP * SB_HEAD_DIM
    return pl.pallas_call(
        functools.partial(_sb_kernel, blk=blk, heads=SB_GROUP),
        grid=(batch, SB_HEADS // SB_GROUP, nq),
        in_specs=[
            pl.BlockSpec((gd, blk), lambda b, h, i: (qT_row + h, b * nq + i)),
            pl.BlockSpec((seq, gd), lambda b, h, i: (b, k_col + h)),
            pl.BlockSpec((gd, seq), lambda b, h, i: (vT_row + h, b)),
        ],
        out_specs=pl.BlockSpec((gd, blk), lambda b, h, i: (h, b * nq + i)),
        out_shape=jax.ShapeDtypeStruct((SB_HEADS * SB_HEAD_DIM, T), BF16),
        scratch_shapes=[pltpu.VMEM((gd, blk), F32)],
        compiler_params=_params("parallel", "parallel", "arbitrary"),
        name="stick_breaking",
    )(pt, pn_b, pt)


def _merge_kernel(x_ref, xp_ref, g_ref, win_ref, yml_ref, ysbT_ref, cw_ref,
                  wml_ref, wsc_ref, wsb_ref, wout_ref, o_ref, *, tiles_per_seq):
    tm, D = x_ref.shape
    first = (pl.program_id(0) % tiles_per_seq) == 0
    x = x_ref[...]
    g = g_ref[...]
    h = _rmsnorm_rows(x, g).astype(BF16)
    h_ext = jnp.concatenate([_rmsnorm_rows(xp_ref[...], g).astype(BF16), h], axis=0)
    col = lambda n: slice(n * D, (n + 1) * D)

    z = _dot(h_ext, win_ref[:, col(1)]) * _dot(h_ext, win_ref[:, col(2)])
    row = lax.broadcasted_iota(jnp.int32, z.shape, 0)
    z = jnp.where(jnp.logical_and(first, row < 8), 0.0, z)
    cw = cw_ref[...]
    conv = cw[0:1, :] * z[6:6 + tm, :] + cw[1:2, :] * z[7:7 + tm, :] + cw[2:3, :] * z[8:8 + tm, :]
    y_sc = _dot(h, win_ref[:, col(0)]) * conv

    merged = jax.nn.sigmoid(_dot(h, win_ref[:, col(3)])) * _dot(yml_ref[...], wml_ref[...])
    merged += jax.nn.sigmoid(_dot(h, win_ref[:, col(4)])) * _dot(y_sc.astype(BF16), wsc_ref[...])
    merged += jax.nn.sigmoid(_dot(h, win_ref[:, col(5)])) * _dot_tn(ysbT_ref[...], wsb_ref[...])
    o_ref[...] = x + _dot(merged.astype(BF16), wout_ref[...])


def _merge(x, g, w_in6, y_ml, y_sbT, conv_w, w_ml, w_sc, w_sb, w_out, *, seq):
    T, D = x.shape
    tm = min(TM_MERGE, seq)
    const = lambda i: (0, 0)
    resident = lambda shape: pl.BlockSpec(shape, const, pipeline_mode=pl.Buffered(1))
    return pl.pallas_call(
        functools.partial(_merge_kernel, tiles_per_seq=seq // tm),
        grid=(T // tm,),
        in_specs=[
            pl.BlockSpec((tm, D), lambda i: (i, 0)),
            pl.BlockSpec((8, D), lambda i: (jnp.maximum(i * (tm // 8) - 1, 0), 0)),
            resident((1, D)),
            resident((D, 6 * D)),
            pl.BlockSpec((tm, D), lambda i: (i, 0)),
            pl.BlockSpec((D, tm), lambda i: (0, i)),
            resident((3, D)),
            resident((D, D)), resident((D, D)), resident((D, D)), resident((D, D)),
        ],
        out_specs=pl.BlockSpec((tm, D), lambda i: (i, 0)),
        out_shape=jax.ShapeDtypeStruct((T, D), F32),
        compiler_params=_params("parallel"),
        name="merge",
    )(x, x, g, w_in6, y_ml, y_sbT, conv_w, w_ml, w_sc, w_sb, w_out)


def _mlp_kernel(x_ref, g_ref, wup_ref, wdown_ref, gf_ref, o_ref, *, ff_chunk, final_norm):
    x = x_ref[...]
    h = _rmsnorm_rows(x, g_ref[...]).astype(BF16)
    acc = x
    for c in range(wup_ref.shape[1] // ff_chunk):
        cols = slice(c * ff_chunk, (c + 1) * ff_chunk)
        up = jnp.maximum(_dot(h, wup_ref[:, cols]), 0.0)
        acc = acc + _dot((up * up).astype(BF16), wdown_ref[cols, :])
    if final_norm:
        acc = _rmsnorm_rows(acc, gf_ref[...])
    o_ref[...] = acc


def _mlp(x, g, w_up, w_down, g_final, *, final_norm):
    T, D = x.shape
    F = w_up.shape[1]
    tm = min(TM_MLP, T)
    const = lambda i: (0, 0)
    return pl.pallas_call(
        functools.partial(_mlp_kernel, ff_chunk=1024, final_norm=final_norm),
        grid=(T // tm,),
        in_specs=[
            pl.BlockSpec((tm, D), lambda i: (i, 0)),
            pl.BlockSpec((1, D), const),
            pl.BlockSpec((D, F), const, pipeline_mode=pl.Buffered(1)),
            pl.BlockSpec((F, D), const, pipeline_mode=pl.Buffered(1)),
            pl.BlockSpec((1, D), const),
        ],
        out_specs=pl.BlockSpec((tm, D), lambda i: (i, 0)),
        out_shape=jax.ShapeDtypeStruct((T, D), F32),
        compiler_params=_params("parallel"),
        name="mlp",
    )(x, g, w_up, w_down, g_final)


def _layer(x, p, *, batch, seq, final_norm, g_final):
    D = D_MODEL
    ml_w = ML_HEADS * ML_HEAD_DIM
    w_in = p["w_in"]
    off = 0

    def take(n):
        nonlocal off
        w = w_in[:, off:off + n]
        off += n
        return w

    w_mq, w_mk, w_mv, w_mo = take(ml_w), take(ml_w), take(ml_w), take(ml_w)
    w_mi, w_mf = take(ML_HEADS), take(ML_HEADS)
    w_b, w_c, w_u = take(D), take(D), take(D)
    w_sq, w_sk, w_sv = take(D), take(D), take(D)
    w_gml, w_gsc, w_gsb = take(D), take(D), take(D)

    wn_b = jnp.concatenate([w_mq, w_mv, w_sk], axis=1).astype(BF16)
    wn_f = w_mo.astype(BF16)
    w_in6 = jnp.concatenate([w_b, w_c, w_u, w_gml, w_gsc, w_gsb], axis=1).astype(BF16)
    wt = jnp.concatenate([w_mk.T * (ML_HEAD_DIM ** -0.5), w_sq.T * (SB_HEAD_DIM ** -0.5), w_sv.T],
                         axis=0).astype(BF16)
    wg = jnp.concatenate([w_mi, w_mf], axis=1).T
    bg = p["b_if"].reshape(2 * ML_HEADS, 1)

    g_mix = p["norm_mix_g"].reshape(1, D)
    pn_b = _norm_proj(x, g_mix, wn_b, BF16)
    pn_f = _norm_proj(x, g_mix, wn_f, F32)
    pt, gates = _norm_proj_t(x, g_mix, wt, wg, bg)

    y_ml = _mlstm(pn_b, pn_f, pt, gates, p["ml_norm_g"].reshape(1, ml_w), batch=batch, seq=seq,
                  q_col=0, v_col=1, o_col=0, kT_row=0)
    y_sbT = _stick_breaking(pn_b, pt, batch=batch, seq=seq,
                            k_col=(2 * ml_w) // (SB_GROUP * SB_HEAD_DIM),
                            qT_row=ml_w // (SB_GROUP * SB_HEAD_DIM),
                            vT_row=(ml_w + D) // (SB_GROUP * SB_HEAD_DIM))
    x = _merge(x, g_mix, w_in6, y_ml, y_sbT, p["conv_w"],
               p["w_ml_proj"].astype(BF16), p["w_sc_proj"].astype(BF16),
               p["w_sb_proj"].astype(BF16), p["w_out"].astype(BF16), seq=seq)
    return _mlp(x, p["norm_mlp_g"].reshape(1, D), p["w_up"].astype(BF16), p["w_down"].astype(BF16),
                g_final.reshape(1, D), final_norm=final_norm)


def kernel(x, norm_mix_g, w_in, b_if, ml_norm_g, conv_w, w_ml_proj, w_sc_proj, w_sb_proj, w_out,
           norm_mlp_g, w_up, w_down, norm_final_g):
    batch, seq, D = x.shape
    depth = w_in.shape[0]
    xt = x.reshape(batch * seq, D)
    for l in range(depth):
        p = dict(norm_mix_g=norm_mix_g[l], w_in=w_in[l], b_if=b_if[l], ml_norm_g=ml_norm_g[l],
                 conv_w=conv_w[l], w_ml_proj=w_ml_proj[l], w_sc_proj=w_sc_proj[l],
                 w_sb_proj=w_sb_proj[l], w_out=w_out[l], norm_mlp_g=norm_mlp_g[l],
                 w_up=w_up[l], w_down=w_down[l])
        xt = _layer(xt, p, batch=batch, seq=seq, final_norm=(l == depth - 1), g_final=norm_final_g)
    return xt.reshape(batch, seq, D)
```

```python
import functools

import jax
import jax.numpy as jnp
from jax import lax
from jax.experimental import pallas as pl
from jax.experimental.pallas import tpu as pltpu

D_MODEL = 1024
ML_HEADS = 4
ML_HEAD_DIM = 256
SB_HEADS = 16
SB_HEAD_DIM = 64
D_FF = 4 * D_MODEL
EPS = 1e-6

VMEM_LIMIT_BYTES = 56 * 1024 * 1024

ML_CHUNK = 256
SB_BLOCK = 256
SB_GROUP = 4
LOG2E = 1.4426950408889634
SB_DEAD_LOG2 = -152.0
SB_ABSENT_LOG2 = -1e30
TM_PROJ = 1024
TM_MERGE = 512
TM_MLP = 512

BF16 = jnp.bfloat16
F32 = jnp.float32


def _params(*sem):
    return pltpu.CompilerParams(dimension_semantics=sem, vmem_limit_bytes=VMEM_LIMIT_BYTES)


def _dot(a, b):
    return jnp.dot(a, b, preferred_element_type=F32)


def _dot_nt(a, b):
    return lax.dot_general(a, b, (((1,), (1,)), ((), ())), preferred_element_type=F32)


def _dot_tn(a, b):
    return lax.dot_general(a, b, (((0,), (0,)), ((), ())), preferred_element_type=F32)


def _rmsnorm_rows(x, g):
    ms = jnp.mean(x * x, axis=-1, keepdims=True)
    return x * lax.rsqrt(ms + EPS) * g


def _log_sigmoid(x):
    return jnp.minimum(x, 0.0) - jnp.log(1.0 + jnp.exp(-jnp.abs(x)))


def _split_bf16(x):
    hi = x.astype(BF16)
    lo = (x - hi.astype(F32)).astype(BF16)
    return hi, lo


def _norm_proj_kernel(x_ref, g_ref, w_ref, o_ref, h_ref):
    @pl.when(pl.program_id(1) == 0)
    def _():
        h_ref[...] = _rmsnorm_rows(x_ref[...], g_ref[...]).astype(BF16)

    o_ref[...] = _dot(h_ref[...], w_ref[...]).astype(o_ref.dtype)


def _norm_proj(x, g, w, out_dtype, tn=1024):
    T, D = x.shape
    N = w.shape[1]
    tm = min(TM_PROJ, T)
    return pl.pallas_call(
        _norm_proj_kernel,
        grid=(T // tm, N // tn),
        in_specs=[
            pl.BlockSpec((tm, D), lambda i, j: (i, 0)),
            pl.BlockSpec((1, D), lambda i, j: (0, 0)),
            pl.BlockSpec((D, tn), lambda i, j: (0, j)),
        ],
        out_specs=pl.BlockSpec((tm, tn), lambda i, j: (i, j)),
        out_shape=jax.ShapeDtypeStruct((T, N), out_dtype),
        scratch_shapes=[pltpu.VMEM((tm, D), BF16)],
        compiler_params=_params("parallel", "arbitrary"),
        name="norm_proj",
    )(x, g, w)


def _norm_proj_t_kernel(scale_ref, x_ref, g_ref, wt_ref, wg_hi_ref, wg_lo_ref, bg_ref, o_ref, gates_ref,
                        h_ref):
    @pl.when(pl.program_id(1) == 0)
    def _():
        h = _rmsnorm_rows(x_ref[...], g_ref[...])
        h_hi, h_lo = _split_bf16(h)
        h_ref[...] = h_hi
        wg_hi = wg_hi_ref[...]
        gates = _dot_nt(wg_hi, h_hi) + (_dot_nt(wg_hi, h_lo) + _dot_nt(wg_lo_ref[...], h_hi))
        gates_ref[...] = gates + bg_ref[...]

    scale = scale_ref[pl.program_id(1)]
    o_ref[...] = (_dot_nt(wt_ref[...], h_ref[...]) * scale).astype(o_ref.dtype)


def _norm_proj_t(x, g, wt, scales, wg, bg, tn=1024):
    T, D = x.shape
    N = wt.shape[0]
    G = wg.shape[0]
    tm = min(TM_PROJ, T)
    wg_hi, wg_lo = _split_bf16(wg)
    return pl.pallas_call(
        _norm_proj_t_kernel,
        grid=(T // tm, N // tn),
        in_specs=[
            pl.BlockSpec(memory_space=pltpu.SMEM),
            pl.BlockSpec((tm, D), lambda i, j: (i, 0)),
            pl.BlockSpec((1, D), lambda i, j: (0, 0)),
            pl.BlockSpec((tn, D), lambda i, j: (j, 0)),
            pl.BlockSpec((G, D), lambda i, j: (0, 0)),
            pl.BlockSpec((G, D), lambda i, j: (0, 0)),
            pl.BlockSpec((G, 1), lambda i, j: (0, 0)),
        ],
        out_specs=[
            pl.BlockSpec((tn, tm), lambda i, j: (j, i)),
            pl.BlockSpec((G, tm), lambda i, j: (0, i)),
        ],
        out_shape=[
            jax.ShapeDtypeStruct((N, T), BF16),
            jax.ShapeDtypeStruct((G, T), F32),
        ],
        scratch_shapes=[pltpu.VMEM((tm, D), BF16)],
        compiler_params=_params("parallel", "arbitrary"),
        name="norm_proj_t",
    )(scales, x, g, wt, wg_hi, wg_lo, bg)


def _cumsum_lanes(x):
    n = x.shape[-1]
    lane = lax.broadcasted_iota(jnp.int32, x.shape, x.ndim - 1)
    k = 1
    while k < n:
        x = x + jnp.where(lane >= k, pltpu.roll(x, k, axis=x.ndim - 1), 0.0)
        k *= 2
    return x


def _mlstm_kernel(q_ref, kT_ref, v_ref, o_ref, gates_ref, gain_ref, y_ref, c_ref, n_ref, m_ref):
    @pl.when(pl.program_id(1) == 0)
    def _():
        c_ref[...] = jnp.zeros_like(c_ref)
        n_ref[...] = jnp.zeros_like(n_ref)
        m_ref[...] = jnp.zeros_like(m_ref)

    L = q_ref.shape[0]
    dh = ML_HEAD_DIM
    heads = range(ML_HEADS)
    cols = lambda g: slice(g * dh, (g + 1) * dh)
    t_idx = lax.broadcasted_iota(jnp.int32, (L, L), 0)
    s_idx = lax.broadcasted_iota(jnp.int32, (L, L), 1)
    causal = s_idx <= t_idx

    q = [q_ref[:, cols(g)] for g in heads]
    c_prev = [c_ref[g] for g in heads]
    n_prev = [n_ref[g] for g in heads]
    m_prev = [m_ref[g] for g in heads]
    qk = [_dot(q[g], kT_ref[cols(g), :]) for g in heads]
    qc = [_dot(q[g], c_prev[g].astype(BF16)) for g in heads]
    nq = [_dot(q[g], jnp.broadcast_to(n_prev[g], (dh, 128)).astype(BF16))[:, 0:1] for g in heads]

    lf_all = _log_sigmoid(gates_ref[...])
    b_all = _cumsum_lanes(lf_all)
    m_t, a_inter, s, num = [], [], [], []
    b_row, i_row = [], []
    for g in heads:
        i_row.append(gates_ref[g:g + 1, :])
        lf_row = lf_all[ML_HEADS + g:ML_HEADS + g + 1, :]
        b_row.append(b_all[ML_HEADS + g:ML_HEADS + g + 1, :])
        b_col = jnp.sum(jnp.where(causal, lf_row, 0.0), axis=1, keepdims=True)
        dmat = jnp.where(causal, b_col - b_row[g] + i_row[g], -jnp.inf)
        inter = b_col + m_prev[g]
        m_t.append(jnp.maximum(inter, jnp.max(dmat, axis=1, keepdims=True)))
        a_inter.append(jnp.exp(inter - m_t[g]))
        s.append(qk[g] * jnp.exp(dmat - m_t[g]))
        num.append(_dot(s[g].astype(BF16), v_ref[:, cols(g)]))

    for g in heads:
        b_last = b_row[g][:, L - 1:L]
        g_row = b_last - b_row[g] + i_row[g]
        m_new = jnp.maximum(b_last + m_prev[g], jnp.max(g_row, axis=1, keepdims=True))
        decay = jnp.exp(b_last + m_prev[g] - m_new)
        kw = kT_ref[cols(g), :].astype(F32) * jnp.exp(g_row - m_new)
        c_ref[g] = decay * c_prev[g] + _dot(kw.astype(BF16), v_ref[:, cols(g)])
        n_ref[g] = decay * n_prev[g] + jnp.sum(kw, axis=1, keepdims=True)
        m_ref[g] = m_new

    for g in heads:
        den = jnp.sum(s[g], axis=1, keepdims=True) + a_inter[g] * nq[g]
        h = (num[g] + a_inter[g] * qc[g]) / jnp.maximum(jnp.abs(den), jnp.exp(-m_t[g]))
        hg = jax.nn.sigmoid(o_ref[:, cols(g)]) * h
        ms = jnp.mean(hg * hg, axis=-1, keepdims=True)
        y_ref[:, cols(g)] = (hg * lax.rsqrt(ms + EPS) * gain_ref[:, cols(g)]).astype(y_ref.dtype)


def _mlstm(pn_b, pn_f, pt, gates, gain, *, batch, seq, q_col, v_col, o_col, kT_row):
    T = batch * seq
    L = min(ML_CHUNK, seq)
    nc = seq // L
    dh = ML_HEAD_DIM
    W = ML_HEADS * dh
    tok = lambda b, c: b * nc + c
    return pl.pallas_call(
        _mlstm_kernel,
        grid=(batch, nc),
        in_specs=[
            pl.BlockSpec((L, W), lambda b, c: (tok(b, c), q_col)),
            pl.BlockSpec((W, L), lambda b, c: (kT_row, tok(b, c))),
            pl.BlockSpec((L, W), lambda b, c: (tok(b, c), v_col)),
            pl.BlockSpec((L, W), lambda b, c: (tok(b, c), o_col)),
            pl.BlockSpec((2 * ML_HEADS, L), lambda b, c: (0, tok(b, c))),
            pl.BlockSpec((1, W), lambda b, c: (0, 0)),
        ],
        out_specs=pl.BlockSpec((L, W), lambda b, c: (tok(b, c), 0)),
        out_shape=jax.ShapeDtypeStruct((T, W), BF16),
        scratch_shapes=[
            pltpu.VMEM((ML_HEADS, dh, dh), F32),
            pltpu.VMEM((ML_HEADS, dh, 1), F32),
            pltpu.VMEM((ML_HEADS, 1, 1), F32),
        ],
        compiler_params=_params("parallel", "arbitrary"),
        name="mlstm",
    )(pn_b, pt, pn_b, pn_f, gates, gain)


def _sb_kernel(qT_ref, k_ref, vT_ref, o_ref, acc_ref, *, blk, heads):
    i = pl.program_id(2)
    d = SB_HEAD_DIM

    s_idx = lax.broadcasted_iota(jnp.int32, (blk, blk), 0)
    t_idx = lax.broadcasted_iota(jnp.int32, (blk, blk), 1)
    later = (t_idx > s_idx)
    upper = jnp.where(later, 1.0, 0.0).astype(BF16)

    row_head = lax.broadcasted_iota(jnp.int32, (2 * d, blk), 0) // d
    q_pads = []
    for g in range(heads):
        q2 = qT_ref[(g // 2) * 2 * d:(g // 2 + 1) * 2 * d, :]
        q_pads.append(jnp.where(row_head == g % 2, q2, jnp.zeros_like(q2)))

    def visit(blocks, rs, init_acc):
        starts = [pl.multiple_of(j * blk, blk) for j, _, _ in blocks]
        z = [[_dot(k_ref[pl.ds(st, blk), (g // 2) * 2 * d:(g // 2 + 1) * 2 * d], q_pads[g])
              for g in range(heads)] for st in starts]
        log_beta, block_sum, suffix = [], [], []
        for b, (_, diagonal, _) in enumerate(blocks):
            log_beta.append([])
            block_sum.append([])
            suffix.append([])
            for g in range(heads):
                zs = z[b][g]
                neg_abs = pltpu.bitcast(pltpu.bitcast(zs, jnp.uint32) | jnp.uint32(0x80000000), F32)
                lb = jnp.minimum(zs, 0.0) - jnp.log(1.0 + jnp.exp2(neg_abs)) * LOG2E
                l1 = lb - zs
                if diagonal:
                    l1 = jnp.where(later, l1, 0.0)
                hi, lo = _split_bf16(l1)
                sfx = _dot(upper, hi) + _dot(upper, lo)
                log_beta[b].append(lb)
                suffix[b].append(sfx)
                block_sum[b].append(sfx[0:1, :] + l1[0:1, :])
        rs = list(rs)
        for b, (_, diagonal, r_gate) in enumerate(blocks):
            for g in range(heads):
                r_in = rs[g] if r_gate is None else jnp.where(r_gate, rs[g], SB_ABSENT_LOG2)
                a = jnp.exp2(log_beta[b][g] + suffix[b][g] + r_in)
                if diagonal:
                    a = jnp.where(later, a, 0.0)
                contrib = _dot(vT_ref[g * d:(g + 1) * d, pl.ds(starts[b], blk)], a.astype(BF16))
                if init_acc and b == 0:
                    acc_ref[g * d:(g + 1) * d, :] = contrib
                else:
                    acc_ref[g * d:(g + 1) * d, :] += contrib
                rs[g] = r_in + block_sum[b][g]
        return tuple(rs)

    def live(rs):
        r_max = functools.reduce(jnp.maximum, rs)
        return jnp.max(r_max) > SB_DEAD_LOG2

    def cond(c):
        n, rs = c
        return jnp.logical_and(n < i - 1, live(rs))

    def body(c):
        n, rs = c
        return n + 1, visit([(i - 2 - n, False, None)], rs, False)

    rs = visit([(i, True, None), (jnp.maximum(i - 1, 0), False, i > 0)],
               tuple(jnp.zeros((1, blk), F32) for _ in range(heads)), True)
    lax.while_loop(cond, body, (jnp.int32(0), rs))
    o_ref[...] = acc_ref[...].astype(o_ref.dtype)


def _stick_breaking(pn_b, pt, *, batch, seq, k_col, qT_row, vT_row):
    T = batch * seq
    blk = min(SB_BLOCK, seq)
    nq = seq // blk
    gd = SB_GROUP * SB_HEAD_DIM
    return pl.pallas_call(
        functools.partial(_sb_kernel, blk=blk, heads=SB_GROUP),
        grid=(batch, SB_HEADS // SB_GROUP, nq),
        in_specs=[
            pl.BlockSpec((gd, blk), lambda b, h, i: (qT_row + h, b * nq + i)),
            pl.BlockSpec((seq, gd), lambda b, h, i: (b, k_col + h)),
            pl.BlockSpec((gd, seq), lambda b, h, i: (vT_row + h, b)),
        ],
        out_specs=pl.BlockSpec((gd, blk), lambda b, h, i: (h, b * nq + i)),
        out_shape=jax.ShapeDtypeStruct((SB_HEADS * SB_HEAD_DIM, T), BF16),
        scratch_shapes=[pltpu.VMEM((gd, blk), F32)],
        compiler_params=_params("parallel", "parallel", "arbitrary"),
        name="stick_breaking",
    )(pt, pn_b, pt)


def _merge_kernel(x_ref, xp_ref, g_ref, win_ref, yml_ref, ysbT_ref, cw_ref,
                  wml_ref, wsc_ref, wsb_ref, wout_ref, o_ref, *, tiles_per_seq):
    tm, D = x_ref.shape
    first = (pl.program_id(0) % tiles_per_seq) == 0
    x = x_ref[...]
    g = g_ref[...]
    h = _rmsnorm_rows(x, g).astype(BF16)
    h_ext = jnp.concatenate([_rmsnorm_rows(xp_ref[...], g).astype(BF16), h], axis=0)
    col = lambda n: slice(n * D, (n + 1) * D)

    z = _dot(h_ext, win_ref[:, col(1)]) * _dot(h_ext, win_ref[:, col(2)])
    row = lax.broadcasted_iota(jnp.int32, z.shape, 0)
    z = jnp.where(jnp.logical_and(first, row < 8), 0.0, z)
    cw = cw_ref[...]
    conv = cw[0:1, :] * z[6:6 + tm, :] + cw[1:2, :] * z[7:7 + tm, :] + cw[2:3, :] * z[8:8 + tm, :]
    y_sc = _dot(h, win_ref[:, col(0)]) * conv

    merged = jax.nn.sigmoid(_dot(h, win_ref[:, col(3)])) * _dot(yml_ref[...], wml_ref[...])
    merged += jax.nn.sigmoid(_dot(h, win_ref[:, col(4)])) * _dot(y_sc.astype(BF16), wsc_ref[...])
    merged += jax.nn.sigmoid(_dot(h, win_ref[:, col(5)])) * _dot_tn(ysbT_ref[...], wsb_ref[...])
    o_ref[...] = x + _dot(merged.astype(BF16), wout_ref[...])


def _merge(x, g, w_in6, y_ml, y_sbT, conv_w, w_ml, w_sc, w_sb, w_out, *, seq):
    T, D = x.shape
    tm = min(TM_MERGE, seq)
    const = lambda i: (0, 0)
    resident = lambda shape: pl.BlockSpec(shape, const, pipeline_mode=pl.Buffered(1))
    return pl.pallas_call(
        functools.partial(_merge_kernel, tiles_per_seq=seq // tm),
        grid=(T // tm,),
        in_specs=[
            pl.BlockSpec((tm, D), lambda i: (i, 0)),
            pl.BlockSpec((8, D), lambda i: (jnp.maximum(i * (tm // 8) - 1, 0), 0)),
            resident((1, D)),
            resident((D, 6 * D)),
            pl.BlockSpec((tm, D), lambda i: (i, 0)),
            pl.BlockSpec((D, tm), lambda i: (0, i)),
            resident((3, D)),
            resident((D, D)), resident((D, D)), resident((D, D)), resident((D, D)),
        ],
        out_specs=pl.BlockSpec((tm, D), lambda i: (i, 0)),
        out_shape=jax.ShapeDtypeStruct((T, D), F32),
        compiler_params=_params("parallel"),
        name="merge",
    )(x, x, g, w_in6, y_ml, y_sbT, conv_w, w_ml, w_sc, w_sb, w_out)


def _mlp_kernel(x_ref, g_ref, wup_ref, wdown_ref, gf_ref, o_ref, *, ff_chunk, final_norm):
    x = x_ref[...]
    h = _rmsnorm_rows(x, g_ref[...]).astype(BF16)
    acc = x
    for c in range(wup_ref.shape[1] // ff_chunk):
        cols = slice(c * ff_chunk, (c + 1) * ff_chunk)
        up = jnp.maximum(_dot(h, wup_ref[:, cols]), 0.0)
        acc = acc + _dot((up * up).astype(BF16), wdown_ref[cols, :])
    if final_norm:
        acc = _rmsnorm_rows(acc, gf_ref[...])
    o_ref[...] = acc


def _mlp(x, g, w_up, w_down, g_final, *, final_norm):
    T, D = x.shape
    F = w_up.shape[1]
    tm = min(TM_MLP, T)
    const = lambda i: (0, 0)
    return pl.pallas_call(
        functools.partial(_mlp_kernel, ff_chunk=1024, final_norm=final_norm),
        grid=(T // tm,),
        in_specs=[
            pl.BlockSpec((tm, D), lambda i: (i, 0)),
            pl.BlockSpec((1, D), const),
            pl.BlockSpec((D, F), const, pipeline_mode=pl.Buffered(1)),
            pl.BlockSpec((F, D), const, pipeline_mode=pl.Buffered(1)),
            pl.BlockSpec((1, D), const),
        ],
        out_specs=pl.BlockSpec((tm, D), lambda i: (i, 0)),
        out_shape=jax.ShapeDtypeStruct((T, D), F32),
        compiler_params=_params("parallel"),
        name="mlp",
    )(x, g, w_up, w_down, g_final)


def _layer(x, p, *, batch, seq, final_norm, g_final):
    D = D_MODEL
    ml_w = ML_HEADS * ML_HEAD_DIM
    w_in = p["w_in"]
    off = 0

    def take(n):
        nonlocal off
        w = w_in[:, off:off + n]
        off += n
        return w

    w_mq, w_mk, w_mv, w_mo = take(ml_w), take(ml_w), take(ml_w), take(ml_w)
    w_mi, w_mf = take(ML_HEADS), take(ML_HEADS)
    w_b, w_c, w_u = take(D), take(D), take(D)
    w_sq, w_sk, w_sv = take(D), take(D), take(D)
    w_gml, w_gsc, w_gsb = take(D), take(D), take(D)

    wn_b = jnp.concatenate([w_mq, w_mv, w_sk], axis=1).astype(BF16)
    wn_f = w_mo.astype(BF16)
    w_in6 = jnp.concatenate([w_b, w_c, w_u, w_gml, w_gsc, w_gsb], axis=1).astype(BF16)
    wt = jnp.concatenate([w_mk.T, w_sq.T, w_sv.T], axis=0).astype(BF16)
    scales = jnp.array([ML_HEAD_DIM ** -0.5, SB_HEAD_DIM ** -0.5 * LOG2E, 1.0], F32)
    wg = jnp.concatenate([w_mi, w_mf], axis=1).T
    bg = p["b_if"].reshape(2 * ML_HEADS, 1)

    g_mix = p["norm_mix_g"].reshape(1, D)
    pn_b = _norm_proj(x, g_mix, wn_b, BF16)
    pn_f = _norm_proj(x, g_mix, wn_f, F32)
    pt, gates = _norm_proj_t(x, g_mix, wt, scales, wg, bg)

    y_ml = _mlstm(pn_b, pn_f, pt, gates, p["ml_norm_g"].reshape(1, ml_w), batch=batch, seq=seq,
                  q_col=0, v_col=1, o_col=0, kT_row=0)
    y_sbT = _stick_breaking(pn_b, pt, batch=batch, seq=seq,
                            k_col=(2 * ml_w) // (SB_GROUP * SB_HEAD_DIM),
                            qT_row=ml_w // (SB_GROUP * SB_HEAD_DIM),
                            vT_row=(ml_w + D) // (SB_GROUP * SB_HEAD_DIM))
    x = _merge(x, g_mix, w_in6, y_ml, y_sbT, p["conv_w"],
               p["w_ml_proj"].astype(BF16), p["w_sc_proj"].astype(BF16),
               p["w_sb_proj"].astype(BF16), p["w_out"].astype(BF16), seq=seq)
    return _mlp(x, p["norm_mlp_g"].reshape(1, D), p["w_up"].astype(BF16), p["w_down"].astype(BF16),
                g_final.reshape(1, D), final_norm=final_norm)


def kernel(x, norm_mix_g, w_in, b_if, ml_norm_g, conv_w, w_ml_proj, w_sc_proj, w_sb_proj, w_out,
           norm_mlp_g, w_up, w_down, norm_final_g):
    batch, seq, D = x.shape
    depth = w_in.shape[0]
    xt = x.reshape(batch * seq, D)
    for l in range(depth):
        p = dict(norm_mix_g=norm_mix_g[l], w_in=w_in[l], b_if=b_if[l], ml_norm_g=ml_norm_g[l],
                 conv_w=conv_w[l], w_ml_proj=w_ml_proj[l], w_sc_proj=w_sc_proj[l],
                 w_sb_proj=w_sb_proj[l], w_out=w_out[l], norm_mlp_g=norm_mlp_g[l],
                 w_up=w_up[l], w_down=w_down[l])
        xt = _layer(xt, p, batch=batch, seq=seq, final_norm=(l == depth - 1), g_final=norm_final_g)
    return xt.reshape(batch, seq, D)
```

```python
import functools

import jax
import jax.numpy as jnp
from jax import lax
from jax.experimental import pallas as pl
from jax.experimental.pallas import tpu as pltpu

D_MODEL = 1024
ML_HEADS = 4
ML_HEAD_DIM = 256
SB_HEADS = 16
SB_HEAD_DIM = 64
D_FF = 4 * D_MODEL
EPS = 1e-6

VMEM_LIMIT_BYTES = 56 * 1024 * 1024

ML_CHUNK = 256
SB_BLOCK = 256
SB_GROUP = 4
LOG2E = 1.4426950408889634
SB_DEAD_LOG2 = -152.0
SB_ABSENT_LOG2 = -1e30
TM_PROJ = 1024
TM_MERGE = 512
TM_MLP = 512
PREV_ROWS = 16

BF16 = jnp.bfloat16
F32 = jnp.float32


def _params(*sem):
    return pltpu.CompilerParams(dimension_semantics=sem, vmem_limit_bytes=VMEM_LIMIT_BYTES)


def _dot(a, b):
    return jnp.dot(a, b, preferred_element_type=F32)


def _dot_nt(a, b):
    return lax.dot_general(a, b, (((1,), (1,)), ((), ())), preferred_element_type=F32)


def _dot_tn(a, b):
    return lax.dot_general(a, b, (((0,), (0,)), ((), ())), preferred_element_type=F32)


def _rmsnorm_rows(x, g):
    ms = jnp.mean(x * x, axis=-1, keepdims=True)
    return x * lax.rsqrt(ms + EPS) * g


def _log_sigmoid(x):
    return jnp.minimum(x, 0.0) - jnp.log(1.0 + jnp.exp(-jnp.abs(x)))


def _split_bf16(x):
    hi = x.astype(BF16)
    lo = (x - hi.astype(F32)).astype(BF16)
    return hi, lo


def _prenorm_outputs(x, g, wg_hi, wg_lo, bg):
    h_hi, h_lo = _split_bf16(_rmsnorm_rows(x, g))
    gates = _dot_nt(wg_hi, h_hi) + (_dot_nt(wg_hi, h_lo) + _dot_nt(wg_lo, h_hi)) + bg
    return h_hi, gates


def _prenorm_kernel(x_ref, g_ref, wg_hi_ref, wg_lo_ref, bg_ref, h_ref, gates_ref):
    h_ref[...], gates_ref[...] = _prenorm_outputs(x_ref[...], g_ref[...], wg_hi_ref[...],
                                                  wg_lo_ref[...], bg_ref[...])


def _prenorm(x, g, wg_hi, wg_lo, bg):
    T, D = x.shape
    G = wg_hi.shape[0]
    tm = min(TM_PROJ, T)
    const = lambda i: (0, 0)
    return pl.pallas_call(
        _prenorm_kernel,
        grid=(T // tm,),
        in_specs=[
            pl.BlockSpec((tm, D), lambda i: (i, 0)),
            pl.BlockSpec((1, D), const),
            pl.BlockSpec((G, D), const),
            pl.BlockSpec((G, D), const),
            pl.BlockSpec((G, 1), const),
        ],
        out_specs=[pl.BlockSpec((tm, D), lambda i: (i, 0)), pl.BlockSpec((G, tm), lambda i: (0, i))],
        out_shape=[jax.ShapeDtypeStruct((T, D), BF16), jax.ShapeDtypeStruct((G, T), F32)],
        compiler_params=_params("parallel"),
        name="prenorm",
    )(x, g, wg_hi, wg_lo, bg)


def _proj_kernel(h_ref, w_ref, o_ref):
    o_ref[...] = _dot(h_ref[...], w_ref[...]).astype(o_ref.dtype)


def _proj(h, w, out_dtype, tn=1024):
    T, D = h.shape
    N = w.shape[1]
    tm = min(TM_PROJ, T)
    return pl.pallas_call(
        _proj_kernel,
        grid=(T // tm, N // tn),
        in_specs=[
            pl.BlockSpec((tm, D), lambda i, j: (i, 0)),
            pl.BlockSpec((D, tn), lambda i, j: (0, j)),
        ],
        out_specs=pl.BlockSpec((tm, tn), lambda i, j: (i, j)),
        out_shape=jax.ShapeDtypeStruct((T, N), out_dtype),
        compiler_params=_params("parallel", "parallel"),
        name="proj",
    )(h, w)


def _proj_t_kernel(scale_ref, h_ref, wt_ref, o_ref):
    scale = scale_ref[pl.program_id(1)]
    o_ref[...] = (_dot_nt(wt_ref[...], h_ref[...]) * scale).astype(o_ref.dtype)


def _proj_t(h, wt, scales, tn=1024):
    T, D = h.shape
    N = wt.shape[0]
    tm = min(TM_PROJ, T)
    return pl.pallas_call(
        _proj_t_kernel,
        grid=(T // tm, N // tn),
        in_specs=[
            pl.BlockSpec(memory_space=pltpu.SMEM),
            pl.BlockSpec((tm, D), lambda i, j: (i, 0)),
            pl.BlockSpec((tn, D), lambda i, j: (j, 0)),
        ],
        out_specs=pl.BlockSpec((tn, tm), lambda i, j: (j, i)),
        out_shape=jax.ShapeDtypeStruct((N, T), BF16),
        compiler_params=_params("parallel", "parallel"),
        name="proj_t",
    )(scales, h, wt)


def _cumsum_lanes(x):
    n = x.shape[-1]
    lane = lax.broadcasted_iota(jnp.int32, x.shape, x.ndim - 1)
    k = 1
    while k < n:
        x = x + jnp.where(lane >= k, pltpu.roll(x, k, axis=x.ndim - 1), 0.0)
        k *= 2
    return x


def _mlstm_kernel(q_ref, kT_ref, v_ref, o_ref, gates_ref, gain_ref, y_ref, c_ref, n_ref, m_ref):
    @pl.when(pl.program_id(1) == 0)
    def _():
        c_ref[...] = jnp.zeros_like(c_ref)
        n_ref[...] = jnp.zeros_like(n_ref)
        m_ref[...] = jnp.zeros_like(m_ref)

    L = q_ref.shape[0]
    dh = ML_HEAD_DIM
    heads = range(ML_HEADS)
    cols = lambda g: slice(g * dh, (g + 1) * dh)
    t_idx = lax.broadcasted_iota(jnp.int32, (L, L), 0)
    s_idx = lax.broadcasted_iota(jnp.int32, (L, L), 1)
    causal = s_idx <= t_idx

    q = [q_ref[:, cols(g)] for g in heads]
    c_prev = [c_ref[g] for g in heads]
    n_prev = [n_ref[g] for g in heads]
    m_prev = [m_ref[g] for g in heads]
    qk = [_dot(q[g], kT_ref[cols(g), :]) for g in heads]
    qc = [_dot(q[g], c_prev[g].astype(BF16)) for g in heads]
    nq = [_dot(q[g], jnp.broadcast_to(n_prev[g], (dh, 128)).astype(BF16))[:, 0:1] for g in heads]

    lf_all = _log_sigmoid(gates_ref[...])
    b_all = _cumsum_lanes(lf_all)
    m_t, a_inter, s, num = [], [], [], []
    b_row, i_row = [], []
    for g in heads:
        i_row.append(gates_ref[g:g + 1, :])
        lf_row = lf_all[ML_HEADS + g:ML_HEADS + g + 1, :]
        b_row.append(b_all[ML_HEADS + g:ML_HEADS + g + 1, :])
        b_col = jnp.sum(jnp.where(causal, lf_row, 0.0), axis=1, keepdims=True)
        dmat = jnp.where(causal, b_col - b_row[g] + i_row[g], -jnp.inf)
        inter = b_col + m_prev[g]
        m_t.append(jnp.maximum(inter, jnp.max(dmat, axis=1, keepdims=True)))
        a_inter.append(jnp.exp(inter - m_t[g]))
        s.append(qk[g] * jnp.exp(dmat - m_t[g]))
        num.append(_dot(s[g].astype(BF16), v_ref[:, cols(g)]))

    for g in heads:
        b_last = b_row[g][:, L - 1:L]
        g_row = b_last - b_row[g] + i_row[g]
        m_new = jnp.maximum(b_last + m_prev[g], jnp.max(g_row, axis=1, keepdims=True))
        decay = jnp.exp(b_last + m_prev[g] - m_new)
        kw = kT_ref[cols(g), :].astype(F32) * jnp.exp(g_row - m_new)
        c_ref[g] = decay * c_prev[g] + _dot(kw.astype(BF16), v_ref[:, cols(g)])
        n_ref[g] = decay * n_prev[g] + jnp.sum(kw, axis=1, keepdims=True)
        m_ref[g] = m_new

    for g in heads:
        den = jnp.sum(s[g], axis=1, keepdims=True) + a_inter[g] * nq[g]
        h = (num[g] + a_inter[g] * qc[g]) / jnp.maximum(jnp.abs(den), jnp.exp(-m_t[g]))
        hg = jax.nn.sigmoid(o_ref[:, cols(g)]) * h
        ms = jnp.mean(hg * hg, axis=-1, keepdims=True)
        y_ref[:, cols(g)] = (hg * lax.rsqrt(ms + EPS) * gain_ref[:, cols(g)]).astype(y_ref.dtype)


def _mlstm(pn_b, pn_f, pt, gates, gain, *, batch, seq, q_col, v_col, o_col, kT_row):
    T = batch * seq
    L = min(ML_CHUNK, seq)
    nc = seq // L
    dh = ML_HEAD_DIM
    W = ML_HEADS * dh
    tok = lambda b, c: b * nc + c
    return pl.pallas_call(
        _mlstm_kernel,
        grid=(batch, nc),
        in_specs=[
            pl.BlockSpec((L, W), lambda b, c: (tok(b, c), q_col)),
            pl.BlockSpec((W, L), lambda b, c: (kT_row, tok(b, c))),
            pl.BlockSpec((L, W), lambda b, c: (tok(b, c), v_col)),
            pl.BlockSpec((L, W), lambda b, c: (tok(b, c), o_col)),
            pl.BlockSpec((2 * ML_HEADS, L), lambda b, c: (0, tok(b, c))),
            pl.BlockSpec((1, W), lambda b, c: (0, 0)),
        ],
        out_specs=pl.BlockSpec((L, W), lambda b, c: (tok(b, c), 0)),
        out_shape=jax.ShapeDtypeStruct((T, W), BF16),
        scratch_shapes=[
            pltpu.VMEM((ML_HEADS, dh, dh), F32),
            pltpu.VMEM((ML_HEADS, dh, 1), F32),
            pltpu.VMEM((ML_HEADS, 1, 1), F32),
        ],
        compiler_params=_params("parallel", "arbitrary"),
        name="mlstm",
    )(pn_b, pt, pn_b, pn_f, gates, gain)


def _sb_kernel(qT_ref, k_ref, vT_ref, o_ref, acc_ref, *, blk, heads):
    i = pl.program_id(2)
    d = SB_HEAD_DIM

    s_idx = lax.broadcasted_iota(jnp.int32, (blk, blk), 0)
    t_idx = lax.broadcasted_iota(jnp.int32, (blk, blk), 1)
    later = (t_idx > s_idx)
    upper = jnp.where(later, 1.0, 0.0).astype(BF16)

    row_head = lax.broadcasted_iota(jnp.int32, (2 * d, blk), 0) // d
    q_pads = []
    for g in range(heads):
        q2 = qT_ref[(g // 2) * 2 * d:(g // 2 + 1) * 2 * d, :]
        q_pads.append(jnp.where(row_head == g % 2, q2, jnp.zeros_like(q2)))

    def visit(blocks, rs, init_acc):
        starts = [pl.multiple_of(j * blk, blk) for j, _, _ in blocks]
        z = [[_dot(k_ref[pl.ds(st, blk), (g // 2) * 2 * d:(g // 2 + 1) * 2 * d], q_pads[g])
              for g in range(heads)] for st in starts]
        log_beta, block_sum, suffix = [], [], []
        for b, (_, diagonal, _) in enumerate(blocks):
            log_beta.append([])
            block_sum.append([])
            suffix.append([])
            for g in range(heads):
                zs = z[b][g]
                neg_abs = pltpu.bitcast(pltpu.bitcast(zs, jnp.uint32) | jnp.uint32(0x80000000), F32)
                lb = jnp.minimum(zs, 0.0) - jnp.log(1.0 + jnp.exp2(neg_abs)) * LOG2E
                l1 = lb - zs
                if diagonal:
                    l1 = jnp.where(later, l1, 0.0)
                hi, lo = _split_bf16(l1)
                sfx = _dot(upper, hi) + _dot(upper, lo)
                log_beta[b].append(lb)
                suffix[b].append(sfx)
                block_sum[b].append(sfx[0:1, :] + l1[0:1, :])
        rs = list(rs)
        for b, (_, diagonal, r_gate) in enumerate(blocks):
            for g in range(heads):
                r_in = rs[g] if r_gate is None else jnp.where(r_gate, rs[g], SB_ABSENT_LOG2)
                a = jnp.exp2(log_beta[b][g] + suffix[b][g] + r_in)
                if diagonal:
                    a = jnp.where(later, a, 0.0)
                contrib = _dot(vT_ref[g * d:(g + 1) * d, pl.ds(starts[b], blk)], a.astype(BF16))
                if init_acc and b == 0:
                    acc_ref[g * d:(g + 1) * d, :] = contrib
                else:
                    acc_ref[g * d:(g + 1) * d, :] += contrib
                rs[g] = r_in + block_sum[b][g]
        return tuple(rs)

    def live(rs):
        r_max = functools.reduce(jnp.maximum, rs)
        return jnp.max(r_max) > SB_DEAD_LOG2

    def cond(c):
        n, rs = c
        return jnp.logical_and(n < i - 1, live(rs))

    def body(c):
        n, rs = c
        return n + 1, visit([(i - 2 - n, False, None)], rs, False)

    rs = visit([(i, True, None), (jnp.maximum(i - 1, 0), False, i > 0)],
               tuple(jnp.zeros((1, blk), F32) for _ in range(heads)), True)
    lax.while_loop(cond, body, (jnp.int32(0), rs))
    o_ref[...] = acc_ref[...].astype(o_ref.dtype)


def _stick_breaking(pn_b, pt, *, batch, seq, k_col, qT_row, vT_row):
    T = batch * seq
    blk = min(SB_BLOCK, seq)
    nq = seq // blk
    gd = SB_GROUP * SB_HEAD_DIM
    return pl.pallas_call(
        functools.partial(_sb_kernel, blk=blk, heads=SB_GROUP),
        grid=(batch, SB_HEADS // SB_GROUP, nq),
        in_specs=[
            pl.BlockSpec((gd, blk), lambda b, h, i: (qT_row + h, b * nq + i)),
            pl.BlockSpec((seq, gd), lambda b, h, i: (b, k_col + h)),
            pl.BlockSpec((gd, seq), lambda b, h, i: (vT_row + h, b)),
        ],
        out_specs=pl.BlockSpec((gd, blk), lambda b, h, i: (h, b * nq + i)),
        out_shape=jax.ShapeDtypeStruct((SB_HEADS * SB_HEAD_DIM, T), BF16),
        scratch_shapes=[pltpu.VMEM((gd, blk), F32)],
        compiler_params=_params("parallel", "parallel", "arbitrary"),
        name="stick_breaking",
    )(pt, pn_b, pt)


def _merge_kernel(x_ref, h_ref, hp_ref, win_ref, yml_ref, ysbT_ref, cw_ref,
                  wml_ref, wsc_ref, wsb_ref, wout_ref, o_ref, *, tiles_per_seq):
    tm, D = x_ref.shape
    P = hp_ref.shape[0]
    first = (pl.program_id(0) % tiles_per_seq) == 0
    h = h_ref[...]
    h_ext = jnp.concatenate([hp_ref[...], h], axis=0)
    col = lambda n: slice(n * D, (n + 1) * D)

    z = _dot(h_ext, win_ref[:, col(1)]) * _dot(h_ext, win_ref[:, col(2)])
    row = lax.broadcasted_iota(jnp.int32, z.shape, 0)
    z = jnp.where(jnp.logical_and(first, row < P), 0.0, z)
    cw = cw_ref[...]
    conv = (cw[0:1, :] * z[P - 2:P - 2 + tm, :] + cw[1:2, :] * z[P - 1:P - 1 + tm, :]
            + cw[2:3, :] * z[P:P + tm, :])
    y_sc = _dot(h, win_ref[:, col(0)]) * conv

    merged = jax.nn.sigmoid(_dot(h, win_ref[:, col(3)])) * _dot(yml_ref[...], wml_ref[...])
    merged += jax.nn.sigmoid(_dot(h, win_ref[:, col(4)])) * _dot(y_sc.astype(BF16), wsc_ref[...])
    merged += jax.nn.sigmoid(_dot(h, win_ref[:, col(5)])) * _dot_tn(ysbT_ref[...], wsb_ref[...])
    o_ref[...] = x_ref[...] + _dot(merged.astype(BF16), wout_ref[...])


def _merge(x, h, w_in6, y_ml, y_sbT, conv_w, w_ml, w_sc, w_sb, w_out, *, seq):
    T, D = x.shape
    tm = min(TM_MERGE, seq)
    const = lambda i: (0, 0)
    resident = lambda shape: pl.BlockSpec(shape, const, pipeline_mode=pl.Buffered(1))
    return pl.pallas_call(
        functools.partial(_merge_kernel, tiles_per_seq=seq // tm),
        grid=(T // tm,),
        in_specs=[
            pl.BlockSpec((tm, D), lambda i: (i, 0)),
            pl.BlockSpec((tm, D), lambda i: (i, 0)),
            pl.BlockSpec((PREV_ROWS, D), lambda i: (jnp.maximum(i * (tm // PREV_ROWS) - 1, 0), 0)),
            resident((D, 6 * D)),
            pl.BlockSpec((tm, D), lambda i: (i, 0)),
            pl.BlockSpec((D, tm), lambda i: (0, i)),
            resident((3, D)),
            resident((D, D)), resident((D, D)), resident((D, D)), resident((D, D)),
        ],
        out_specs=pl.BlockSpec((tm, D), lambda i: (i, 0)),
        out_shape=jax.ShapeDtypeStruct((T, D), F32),
        compiler_params=_params("parallel"),
        name="merge",
    )(x, h, h, w_in6, y_ml, y_sbT, conv_w, w_ml, w_sc, w_sb, w_out)


def _mlp_kernel(x_ref, g_ref, wup_ref, wdown_ref, gn_ref, wg_hi_ref, wg_lo_ref, bg_ref, *out_refs,
                ff_chunk, last):
    x = x_ref[...]
    h = _rmsnorm_rows(x, g_ref[...]).astype(BF16)
    acc = x
    for c in range(wup_ref.shape[1] // ff_chunk):
        cols = slice(c * ff_chunk, (c + 1) * ff_chunk)
        up = jnp.maximum(_dot(h, wup_ref[:, cols]), 0.0)
        acc = acc + _dot((up * up).astype(BF16), wdown_ref[cols, :])
    if last:
        out_refs[0][...] = _rmsnorm_rows(acc, gn_ref[...])
    else:
        out_refs[0][...] = acc
        out_refs[1][...], out_refs[2][...] = _prenorm_outputs(acc, gn_ref[...], wg_hi_ref[...],
                                                              wg_lo_ref[...], bg_ref[...])


def _mlp(x, g, w_up, w_down, g_next, wg_hi, wg_lo, bg, *, last):
    T, D = x.shape
    F = w_up.shape[1]
    G = wg_hi.shape[0]
    tm = min(TM_MLP, T)
    const = lambda i: (0, 0)
    resident = lambda shape: pl.BlockSpec(shape, const, pipeline_mode=pl.Buffered(1))
    out_specs = [pl.BlockSpec((tm, D), lambda i: (i, 0))]
    out_shape = [jax.ShapeDtypeStruct((T, D), F32)]
    if not last:
        out_specs += [pl.BlockSpec((tm, D), lambda i: (i, 0)), pl.BlockSpec((G, tm), lambda i: (0, i))]
        out_shape += [jax.ShapeDtypeStruct((T, D), BF16), jax.ShapeDtypeStruct((G, T), F32)]
    return pl.pallas_call(
        functools.partial(_mlp_kernel, ff_chunk=1024, last=last),
        grid=(T // tm,),
        in_specs=[
            pl.BlockSpec((tm, D), lambda i: (i, 0)),
            resident((1, D)),
            resident((D, F)),
            resident((F, D)),
            resident((1, D)),
            resident((G, D)), resident((G, D)), resident((G, 1)),
        ],
        out_specs=out_specs,
        out_shape=out_shape,
        compiler_params=_params("parallel"),
        name="mlp",
    )(x, g, w_up, w_down, g_next, wg_hi, wg_lo, bg)


def _gate_params(w_in, b_if):
    off = 4 * ML_HEADS * ML_HEAD_DIM
    wg_hi, wg_lo = _split_bf16(w_in[:, off:off + 2 * ML_HEADS].T)
    return wg_hi, wg_lo, b_if.reshape(2 * ML_HEADS, 1)


def _layer(x, h, gates, p, nxt, *, batch, seq):
    D = D_MODEL
    ml_w = ML_HEADS * ML_HEAD_DIM
    w_in = p["w_in"]
    off = 0

    def take(n):
        nonlocal off
        w = w_in[:, off:off + n]
        off += n
        return w

    w_mq, w_mk, w_mv, w_mo = take(ml_w), take(ml_w), take(ml_w), take(ml_w)
    take(2 * ML_HEADS)
    w_b, w_c, w_u = take(D), take(D), take(D)
    w_sq, w_sk, w_sv = take(D), take(D), take(D)
    w_gml, w_gsc, w_gsb = take(D), take(D), take(D)

    wn_b = jnp.concatenate([w_mq, w_mv, w_sk], axis=1).astype(BF16)
    wn_f = w_mo.astype(BF16)
    w_in6 = jnp.concatenate([w_b, w_c, w_u, w_gml, w_gsc, w_gsb], axis=1).astype(BF16)
    wt = jnp.concatenate([w_mk.T, w_sq.T, w_sv.T], axis=0).astype(BF16)
    scales = jnp.array([ML_HEAD_DIM ** -0.5, SB_HEAD_DIM ** -0.5 * LOG2E, 1.0], F32)

    pn_b = _proj(h, wn_b, BF16)
    pn_f = _proj(h, wn_f, F32)
    pt = _proj_t(h, wt, scales)

    y_ml = _mlstm(pn_b, pn_f, pt, gates, p["ml_norm_g"].reshape(1, ml_w), batch=batch, seq=seq,
                  q_col=0, v_col=1, o_col=0, kT_row=0)
    y_sbT = _stick_breaking(pn_b, pt, batch=batch, seq=seq,
                            k_col=(2 * ml_w) // (SB_GROUP * SB_HEAD_DIM),
                            qT_row=ml_w // (SB_GROUP * SB_HEAD_DIM),
                            vT_row=(ml_w + D) // (SB_GROUP * SB_HEAD_DIM))
    x = _merge(x, h, w_in6, y_ml, y_sbT, p["conv_w"],
               p["w_ml_proj"].astype(BF16), p["w_sc_proj"].astype(BF16),
               p["w_sb_proj"].astype(BF16), p["w_out"].astype(BF16), seq=seq)
    return _mlp(x, p["norm_mlp_g"].reshape(1, D), p["w_up"].astype(BF16), p["w_down"].astype(BF16),
                nxt["g"].reshape(1, D), nxt["wg_hi"], nxt["wg_lo"], nxt["bg"], last=nxt["last"])


def kernel(x, norm_mix_g, w_in, b_if, ml_norm_g, conv_w, w_ml_proj, w_sc_proj, w_sb_proj, w_out,
           norm_mlp_g, w_up, w_down, norm_final_g):
    batch, seq, D = x.shape
    depth = w_in.shape[0]
    xt = x.reshape(batch * seq, D)
    gate_params = [_gate_params(w_in[l], b_if[l]) for l in range(depth)]
    h, gates = _prenorm(xt, norm_mix_g[0].reshape(1, D), *gate_params[0])
    for l in range(depth):
        p = dict(w_in=w_in[l], ml_norm_g=ml_norm_g[l], conv_w=conv_w[l], w_ml_proj=w_ml_proj[l],
                 w_sc_proj=w_sc_proj[l], w_sb_proj=w_sb_proj[l], w_out=w_out[l],
                 norm_mlp_g=norm_mlp_g[l], w_up=w_up[l], w_down=w_down[l])
        last = l == depth - 1
        nl = l if last else l + 1
        nxt = dict(g=norm_final_g if last else norm_mix_g[nl], wg_hi=gate_params[nl][0],
                   wg_lo=gate_params[nl][1], bg=gate_params[nl][2], last=last)
        outs = _layer(xt, h, gates, p, nxt, batch=batch, seq=seq)
        if last:
            xt = outs[0]
        else:
            xt, h, gates = outs
    return xt.reshape(batch, seq, D)
```

```python
import functools

import jax
import jax.numpy as jnp
from jax import lax
from jax.experimental import pallas as pl
from jax.experimental.pallas import tpu as pltpu

D_MODEL = 1024
ML_HEADS = 4
ML_HEAD_DIM = 256
SB_HEADS = 16
SB_HEAD_DIM = 64
D_FF = 4 * D_MODEL
EPS = 1e-6

VMEM_LIMIT_BYTES = 56 * 1024 * 1024

ML_CHUNK = 256
SB_BLOCK = 256
SB_GROUP = 4
LOG2E = 1.4426950408889634
SB_DEAD_LOG2 = -152.0
SB_ABSENT_LOG2 = -1e30
TM_PROJ = 1024
TM_MERGE = 512
TM_MLP = 512
PREV_ROWS = 16

BF16 = jnp.bfloat16
F32 = jnp.float32


def _params(*sem):
    return pltpu.CompilerParams(dimension_semantics=sem, vmem_limit_bytes=VMEM_LIMIT_BYTES)


def _dot(a, b):
    return jnp.dot(a, b, preferred_element_type=F32)


def _dot_nt(a, b):
    return lax.dot_general(a, b, (((1,), (1,)), ((), ())), preferred_element_type=F32)


def _dot_tn(a, b):
    return lax.dot_general(a, b, (((0,), (0,)), ((), ())), preferred_element_type=F32)


def _rmsnorm_rows(x, g):
    ms = jnp.mean(x * x, axis=-1, keepdims=True)
    return x * lax.rsqrt(ms + EPS) * g


def _log_sigmoid(x):
    return jnp.minimum(x, 0.0) - jnp.log(1.0 + jnp.exp(-jnp.abs(x)))


def _split_bf16(x):
    hi = x.astype(BF16)
    lo = (x - hi.astype(F32)).astype(BF16)
    return hi, lo


def _prenorm_outputs(x, g, wg_hi, wg_lo, bg):
    h_hi, h_lo = _split_bf16(_rmsnorm_rows(x, g))
    gates = _dot_nt(wg_hi, h_hi) + (_dot_nt(wg_hi, h_lo) + _dot_nt(wg_lo, h_hi)) + bg
    return h_hi, gates


def _prenorm_kernel(x_ref, g_ref, wg_hi_ref, wg_lo_ref, bg_ref, h_ref, gates_ref):
    h_ref[...], gates_ref[...] = _prenorm_outputs(x_ref[...], g_ref[...], wg_hi_ref[...],
                                                  wg_lo_ref[...], bg_ref[...])


def _prenorm(x, g, wg_hi, wg_lo, bg):
    T, D = x.shape
    G = wg_hi.shape[0]
    tm = min(TM_PROJ, T)
    const = lambda i: (0, 0)
    return pl.pallas_call(
        _prenorm_kernel,
        grid=(T // tm,),
        in_specs=[
            pl.BlockSpec((tm, D), lambda i: (i, 0)),
            pl.BlockSpec((1, D), const),
            pl.BlockSpec((G, D), const),
            pl.BlockSpec((G, D), const),
            pl.BlockSpec((G, 1), const),
        ],
        out_specs=[pl.BlockSpec((tm, D), lambda i: (i, 0)), pl.BlockSpec((G, tm), lambda i: (0, i))],
        out_shape=[jax.ShapeDtypeStruct((T, D), BF16), jax.ShapeDtypeStruct((G, T), F32)],
        compiler_params=_params("parallel"),
        name="prenorm",
    )(x, g, wg_hi, wg_lo, bg)


def _proj_kernel(h_ref, w_ref, o_ref):
    o_ref[...] = _dot(h_ref[...], w_ref[...]).astype(o_ref.dtype)


def _proj(h, w, out_dtype, tn=1024):
    T, D = h.shape
    N = w.shape[1]
    tm = min(TM_PROJ, T)
    return pl.pallas_call(
        _proj_kernel,
        grid=(T // tm, N // tn),
        in_specs=[
            pl.BlockSpec((tm, D), lambda i, j: (i, 0)),
            pl.BlockSpec((D, tn), lambda i, j: (0, j)),
        ],
        out_specs=pl.BlockSpec((tm, tn), lambda i, j: (i, j)),
        out_shape=jax.ShapeDtypeStruct((T, N), out_dtype),
        compiler_params=_params("parallel", "parallel"),
        name="proj",
    )(h, w)


def _proj_t_kernel(scale_ref, h_ref, w_ref, o_ref):
    scale = scale_ref[pl.program_id(1)]
    out = lax.dot_general(w_ref[...], h_ref[...], (((0,), (1,)), ((), ())), preferred_element_type=F32)
    o_ref[...] = (out * scale).astype(o_ref.dtype)


def _proj_t(h, w, scales, tn=1024):
    T, D = h.shape
    N = w.shape[1]
    tm = min(TM_PROJ, T)
    return pl.pallas_call(
        _proj_t_kernel,
        grid=(T // tm, N // tn),
        in_specs=[
            pl.BlockSpec(memory_space=pltpu.SMEM),
            pl.BlockSpec((tm, D), lambda i, j: (i, 0)),
            pl.BlockSpec((D, tn), lambda i, j: (0, j)),
        ],
        out_specs=pl.BlockSpec((tn, tm), lambda i, j: (j, i)),
        out_shape=jax.ShapeDtypeStruct((N, T), BF16),
        compiler_params=_params("parallel", "parallel"),
        name="proj_t",
    )(scales, h, w)


def _cumsum_lanes(x):
    n = x.shape[-1]
    lane = lax.broadcasted_iota(jnp.int32, x.shape, x.ndim - 1)
    k = 1
    while k < n:
        x = x + jnp.where(lane >= k, pltpu.roll(x, k, axis=x.ndim - 1), 0.0)
        k *= 2
    return x


def _mlstm_kernel(q_ref, kT_ref, v_ref, o_ref, gates_ref, gain_ref, y_ref, c_ref, n_ref, m_ref):
    @pl.when(pl.program_id(1) == 0)
    def _():
        c_ref[...] = jnp.zeros_like(c_ref)
        n_ref[...] = jnp.zeros_like(n_ref)
        m_ref[...] = jnp.zeros_like(m_ref)

    L = q_ref.shape[0]
    dh = ML_HEAD_DIM
    heads = range(ML_HEADS)
    cols = lambda g: slice(g * dh, (g + 1) * dh)
    t_idx = lax.broadcasted_iota(jnp.int32, (L, L), 0)
    s_idx = lax.broadcasted_iota(jnp.int32, (L, L), 1)
    causal = s_idx <= t_idx

    q = [q_ref[:, cols(g)] for g in heads]
    c_prev = [c_ref[g] for g in heads]
    n_prev = [n_ref[g] for g in heads]
    m_prev = [m_ref[g] for g in heads]
    qk = [_dot(q[g], kT_ref[cols(g), :]) for g in heads]
    qc = [_dot(q[g], c_prev[g].astype(BF16)) for g in heads]
    nq = [_dot(q[g], jnp.broadcast_to(n_prev[g], (dh, 128)).astype(BF16))[:, 0:1] for g in heads]

    lf_all = _log_sigmoid(gates_ref[...])
    b_all = _cumsum_lanes(lf_all)
    m_t, a_inter, s, num = [], [], [], []
    b_row, i_row = [], []
    for g in heads:
        i_row.append(gates_ref[g:g + 1, :])
        lf_row = lf_all[ML_HEADS + g:ML_HEADS + g + 1, :]
        b_row.append(b_all[ML_HEADS + g:ML_HEADS + g + 1, :])
        b_col = jnp.sum(jnp.where(causal, lf_row, 0.0), axis=1, keepdims=True)
        dmat = jnp.where(causal, b_col - b_row[g] + i_row[g], -jnp.inf)
        inter = b_col + m_prev[g]
        m_t.append(jnp.maximum(inter, jnp.max(dmat, axis=1, keepdims=True)))
        a_inter.append(jnp.exp(inter - m_t[g]))
        s.append(qk[g] * jnp.exp(dmat - m_t[g]))
        num.append(_dot(s[g].astype(BF16), v_ref[:, cols(g)]))

    for g in heads:
        b_last = b_row[g][:, L - 1:L]
        g_row = b_last - b_row[g] + i_row[g]
        m_new = jnp.maximum(b_last + m_prev[g], jnp.max(g_row, axis=1, keepdims=True))
        decay = jnp.exp(b_last + m_prev[g] - m_new)
        kw = kT_ref[cols(g), :].astype(F32) * jnp.exp(g_row - m_new)
        c_ref[g] = decay * c_prev[g] + _dot(kw.astype(BF16), v_ref[:, cols(g)])
        n_ref[g] = decay * n_prev[g] + jnp.sum(kw, axis=1, keepdims=True)
        m_ref[g] = m_new

    for g in heads:
        den = jnp.sum(s[g], axis=1, keepdims=True) + a_inter[g] * nq[g]
        h = (num[g] + a_inter[g] * qc[g]) / jnp.maximum(jnp.abs(den), jnp.exp(-m_t[g]))
        hg = jax.nn.sigmoid(o_ref[:, cols(g)]) * h
        ms = jnp.mean(hg * hg, axis=-1, keepdims=True)
        y_ref[:, cols(g)] = (hg * lax.rsqrt(ms + EPS) * gain_ref[:, cols(g)]).astype(y_ref.dtype)


def _mlstm(pn_b, pn_f, pt, gates, gain, *, batch, seq, q_col, v_col, o_col, kT_row):
    T = batch * seq
    L = min(ML_CHUNK, seq)
    nc = seq // L
    dh = ML_HEAD_DIM
    W = ML_HEADS * dh
    tok = lambda b, c: b * nc + c
    return pl.pallas_call(
        _mlstm_kernel,
        grid=(batch, nc),
        in_specs=[
            pl.BlockSpec((L, W), lambda b, c: (tok(b, c), q_col)),
            pl.BlockSpec((W, L), lambda b, c: (kT_row, tok(b, c))),
            pl.BlockSpec((L, W), lambda b, c: (tok(b, c), v_col)),
            pl.BlockSpec((L, W), lambda b, c: (tok(b, c), o_col)),
            pl.BlockSpec((2 * ML_HEADS, L), lambda b, c: (0, tok(b, c))),
            pl.BlockSpec((1, W), lambda b, c: (0, 0)),
        ],
        out_specs=pl.BlockSpec((L, W), lambda b, c: (tok(b, c), 0)),
        out_shape=jax.ShapeDtypeStruct((T, W), BF16),
        scratch_shapes=[
            pltpu.VMEM((ML_HEADS, dh, dh), F32),
            pltpu.VMEM((ML_HEADS, dh, 1), F32),
            pltpu.VMEM((ML_HEADS, 1, 1), F32),
        ],
        compiler_params=_params("parallel", "arbitrary"),
        name="mlstm",
    )(pn_b, pt, pn_b, pn_f, gates, gain)


def _sb_kernel(qT_ref, k_ref, vT_ref, o_ref, acc_ref, *, blk, heads):
    i = pl.program_id(2)
    d = SB_HEAD_DIM

    s_idx = lax.broadcasted_iota(jnp.int32, (blk, blk), 0)
    t_idx = lax.broadcasted_iota(jnp.int32, (blk, blk), 1)
    later = (t_idx > s_idx)
    upper = jnp.where(later, 1.0, 0.0).astype(BF16)

    row_head = lax.broadcasted_iota(jnp.int32, (2 * d, blk), 0) // d
    q_pads = []
    for g in range(heads):
        q2 = qT_ref[(g // 2) * 2 * d:(g // 2 + 1) * 2 * d, :]
        q_pads.append(jnp.where(row_head == g % 2, q2, jnp.zeros_like(q2)))

    def visit(blocks, rs, init_acc):
        starts = [pl.multiple_of(j * blk, blk) for j, _, _ in blocks]
        z = [[_dot(k_ref[pl.ds(st, blk), (g // 2) * 2 * d:(g // 2 + 1) * 2 * d], q_pads[g])
              for g in range(heads)] for st in starts]
        log_beta, block_sum, suffix = [], [], []
        for b, (_, diagonal, _) in enumerate(blocks):
            log_beta.append([])
            block_sum.append([])
            suffix.append([])
            for g in range(heads):
                zs = z[b][g]
                neg_abs = pltpu.bitcast(pltpu.bitcast(zs, jnp.uint32) | jnp.uint32(0x80000000), F32)
                lb = jnp.minimum(zs, 0.0) - jnp.log(1.0 + jnp.exp2(neg_abs)) * LOG2E
                l1 = lb - zs
                if diagonal:
                    l1 = jnp.where(later, l1, 0.0)
                hi, lo = _split_bf16(l1)
                sfx = _dot(upper, hi) + _dot(upper, lo)
                log_beta[b].append(lb)
                suffix[b].append(sfx)
                block_sum[b].append(sfx[0:1, :] + l1[0:1, :])
        rs = list(rs)
        for b, (_, diagonal, r_gate) in enumerate(blocks):
            for g in range(heads):
                r_in = rs[g] if r_gate is None else jnp.where(r_gate, rs[g], SB_ABSENT_LOG2)
                a = jnp.exp2(log_beta[b][g] + suffix[b][g] + r_in)
                if diagonal:
                    a = jnp.where(later, a, 0.0)
                contrib = _dot(vT_ref[g * d:(g + 1) * d, pl.ds(starts[b], blk)], a.astype(BF16))
                if init_acc and b == 0:
                    acc_ref[g * d:(g + 1) * d, :] = contrib
                else:
                    acc_ref[g * d:(g + 1) * d, :] += contrib
                rs[g] = r_in + block_sum[b][g]
        return tuple(rs)

    def live(rs):
        r_max = functools.reduce(jnp.maximum, rs)
        return jnp.max(r_max) > SB_DEAD_LOG2

    def cond(c):
        n, rs = c
        return jnp.logical_and(n < i - 1, live(rs))

    def body(c):
        n, rs = c
        return n + 1, visit([(i - 2 - n, False, None)], rs, False)

    rs = visit([(i, True, None), (jnp.maximum(i - 1, 0), False, i > 0)],
               tuple(jnp.zeros((1, blk), F32) for _ in range(heads)), True)
    lax.while_loop(cond, body, (jnp.int32(0), rs))
    o_ref[...] = acc_ref[...].astype(o_ref.dtype)


def _stick_breaking(pn_b, pt, *, batch, seq, k_col, qT_row, vT_row):
    T = batch * seq
    blk = min(SB_BLOCK, seq)
    nq = seq // blk
    gd = SB_GROUP * SB_HEAD_DIM
    return pl.pallas_call(
        functools.partial(_sb_kernel, blk=blk, heads=SB_GROUP),
        grid=(batch, SB_HEADS // SB_GROUP, nq),
        in_specs=[
            pl.BlockSpec((gd, blk), lambda b, h, i: (qT_row + h, b * nq + i)),
            pl.BlockSpec((seq, gd), lambda b, h, i: (b, k_col + h)),
            pl.BlockSpec((gd, seq), lambda b, h, i: (vT_row + h, b)),
        ],
        out_specs=pl.BlockSpec((gd, blk), lambda b, h, i: (h, b * nq + i)),
        out_shape=jax.ShapeDtypeStruct((SB_HEADS * SB_HEAD_DIM, T), BF16),
        scratch_shapes=[pltpu.VMEM((gd, blk), F32)],
        compiler_params=_params("parallel", "parallel", "arbitrary"),
        name="stick_breaking",
    )(pt, pn_b, pt)


def _merge_kernel(x_ref, h_ref, hp_ref, win_ref, yml_ref, ysbT_ref, cw_ref,
                  wml_ref, wsc_ref, wsb_ref, wout_ref, o_ref, *, tiles_per_seq):
    tm, D = x_ref.shape
    P = hp_ref.shape[0]
    first = (pl.program_id(0) % tiles_per_seq) == 0
    h = h_ref[...]
    h_ext = jnp.concatenate([hp_ref[...], h], axis=0)
    col = lambda n: slice(n * D, (n + 1) * D)

    z = _dot(h_ext, win_ref[:, col(1)]) * _dot(h_ext, win_ref[:, col(2)])
    row = lax.broadcasted_iota(jnp.int32, z.shape, 0)
    z = jnp.where(jnp.logical_and(first, row < P), 0.0, z)
    cw = cw_ref[...]
    conv = (cw[0:1, :] * z[P - 2:P - 2 + tm, :] + cw[1:2, :] * z[P - 1:P - 1 + tm, :]
            + cw[2:3, :] * z[P:P + tm, :])
    y_sc = _dot(h, win_ref[:, col(0)]) * conv

    merged = jax.nn.sigmoid(_dot(h, win_ref[:, col(3)])) * _dot(yml_ref[...], wml_ref[...])
    merged += jax.nn.sigmoid(_dot(h, win_ref[:, col(4)])) * _dot(y_sc.astype(BF16), wsc_ref[...])
    merged += jax.nn.sigmoid(_dot(h, win_ref[:, col(5)])) * _dot_tn(ysbT_ref[...], wsb_ref[...])
    o_ref[...] = x_ref[...] + _dot(merged.astype(BF16), wout_ref[...])


def _merge(x, h, w_in6, y_ml, y_sbT, conv_w, w_ml, w_sc, w_sb, w_out, *, seq):
    T, D = x.shape
    tm = min(TM_MERGE, seq)
    const = lambda i: (0, 0)
    resident = lambda shape: pl.BlockSpec(shape, const, pipeline_mode=pl.Buffered(1))
    return pl.pallas_call(
        functools.partial(_merge_kernel, tiles_per_seq=seq // tm),
        grid=(T // tm,),
        in_specs=[
            pl.BlockSpec((tm, D), lambda i: (i, 0)),
            pl.BlockSpec((tm, D), lambda i: (i, 0)),
            pl.BlockSpec((PREV_ROWS, D), lambda i: (jnp.maximum(i * (tm // PREV_ROWS) - 1, 0), 0)),
            resident((D, 6 * D)),
            pl.BlockSpec((tm, D), lambda i: (i, 0)),
            pl.BlockSpec((D, tm), lambda i: (0, i)),
            resident((3, D)),
            resident((D, D)), resident((D, D)), resident((D, D)), resident((D, D)),
        ],
        out_specs=pl.BlockSpec((tm, D), lambda i: (i, 0)),
        out_shape=jax.ShapeDtypeStruct((T, D), F32),
        compiler_params=_params("parallel"),
        name="merge",
    )(x, h, h, w_in6, y_ml, y_sbT, conv_w, w_ml, w_sc, w_sb, w_out)


def _mlp_kernel(x_ref, g_ref, wup_ref, wdown_ref, gn_ref, wg_hi_ref, wg_lo_ref, bg_ref, *out_refs,
                ff_chunk, last):
    x = x_ref[...]
    h = _rmsnorm_rows(x, g_ref[...]).astype(BF16)
    acc = x
    for c in range(wup_ref.shape[1] // ff_chunk):
        cols = slice(c * ff_chunk, (c + 1) * ff_chunk)
        up = jnp.maximum(_dot(h, wup_ref[:, cols]), 0.0)
        acc = acc + _dot((up * up).astype(BF16), wdown_ref[cols, :])
    if last:
        out_refs[0][...] = _rmsnorm_rows(acc, gn_ref[...])
    else:
        out_refs[0][...] = acc
        out_refs[1][...], out_refs[2][...] = _prenorm_outputs(acc, gn_ref[...], wg_hi_ref[...],
                                                              wg_lo_ref[...], bg_ref[...])


def _mlp(x, g, w_up, w_down, g_next, wg_hi, wg_lo, bg, *, last):
    T, D = x.shape
    F = w_up.shape[1]
    G = wg_hi.shape[0]
    tm = min(TM_MLP, T)
    const = lambda i: (0, 0)
    resident = lambda shape: pl.BlockSpec(shape, const, pipeline_mode=pl.Buffered(1))
    out_specs = [pl.BlockSpec((tm, D), lambda i: (i, 0))]
    out_shape = [jax.ShapeDtypeStruct((T, D), F32)]
    if not last:
        out_specs += [pl.BlockSpec((tm, D), lambda i: (i, 0)), pl.BlockSpec((G, tm), lambda i: (0, i))]
        out_shape += [jax.ShapeDtypeStruct((T, D), BF16), jax.ShapeDtypeStruct((G, T), F32)]
    return pl.pallas_call(
        functools.partial(_mlp_kernel, ff_chunk=1024, last=last),
        grid=(T // tm,),
        in_specs=[
            pl.BlockSpec((tm, D), lambda i: (i, 0)),
            resident((1, D)),
            resident((D, F)),
            resident((F, D)),
            resident((1, D)),
            resident((G, D)), resident((G, D)), resident((G, 1)),
        ],
        out_specs=out_specs,
        out_shape=out_shape,
        compiler_params=_params("parallel"),
        name="mlp",
    )(x, g, w_up, w_down, g_next, wg_hi, wg_lo, bg)


def _gate_params(w_in, b_if):
    off = 4 * ML_HEADS * ML_HEAD_DIM
    wg_hi, wg_lo = _split_bf16(w_in[:, off:off + 2 * ML_HEADS].T)
    return wg_hi, wg_lo, b_if.reshape(2 * ML_HEADS, 1)


def _layer(x, h, gates, p, nxt, *, batch, seq):
    D = D_MODEL
    ml_w = ML_HEADS * ML_HEAD_DIM
    w_in = p["w_in"]
    off = 0

    def take(n):
        nonlocal off
        w = w_in[:, off:off + n]
        off += n
        return w

    w_mq, w_mk, w_mv, w_mo = take(ml_w), take(ml_w), take(ml_w), take(ml_w)
    take(2 * ML_HEADS)
    w_b, w_c, w_u = take(D), take(D), take(D)
    w_sq, w_sk, w_sv = take(D), take(D), take(D)
    w_gml, w_gsc, w_gsb = take(D), take(D), take(D)

    wn_b = jnp.concatenate([w_mq, w_mv, w_sk], axis=1)
    wn_f = w_mo
    w_in6 = jnp.concatenate([w_b, w_c, w_u, w_gml, w_gsc, w_gsb], axis=1)
    wt = jnp.concatenate([w_mk, w_sq, w_sv], axis=1)
    scales = jnp.array([ML_HEAD_DIM ** -0.5, SB_HEAD_DIM ** -0.5 * LOG2E, 1.0], F32)

    pn_b = _proj(h, wn_b, BF16)
    pn_f = _proj(h, wn_f, F32)
    pt = _proj_t(h, wt, scales)

    y_ml = _mlstm(pn_b, pn_f, pt, gates, p["ml_norm_g"].reshape(1, ml_w), batch=batch, seq=seq,
                  q_col=0, v_col=1, o_col=0, kT_row=0)
    y_sbT = _stick_breaking(pn_b, pt, batch=batch, seq=seq,
                            k_col=(2 * ml_w) // (SB_GROUP * SB_HEAD_DIM),
                            qT_row=ml_w // (SB_GROUP * SB_HEAD_DIM),
                            vT_row=(ml_w + D) // (SB_GROUP * SB_HEAD_DIM))
    x = _merge(x, h, w_in6, y_ml, y_sbT, p["conv_w"],
               p["w_ml_proj"].astype(BF16), p["w_sc_proj"].astype(BF16),
               p["w_sb_proj"].astype(BF16), p["w_out"].astype(BF16), seq=seq)
    return _mlp(x, p["norm_mlp_g"].reshape(1, D), p["w_up"].astype(BF16), p["w_down"].astype(BF16),
                nxt["g"].reshape(1, D), nxt["wg_hi"], nxt["wg_lo"], nxt["bg"], last=nxt["last"])


def kernel(x, norm_mix_g, w_in, b_if, ml_norm_g, conv_w, w_ml_proj, w_sc_proj, w_sb_proj, w_out,
           norm_mlp_g, w_up, w_down, norm_final_g):
    batch, seq, D = x.shape
    depth = w_in.shape[0]
    xt = x.reshape(batch * seq, D)
    gate_params = [_gate_params(w_in[l], b_if[l]) for l in range(depth)]
    w_in_b = w_in.astype(BF16)
    h, gates = _prenorm(xt, norm_mix_g[0].reshape(1, D), *gate_params[0])
    for l in range(depth):
        p = dict(w_in=w_in_b[l], ml_norm_g=ml_norm_g[l], conv_w=conv_w[l], w_ml_proj=w_ml_proj[l],
                 w_sc_proj=w_sc_proj[l], w_sb_proj=w_sb_proj[l], w_out=w_out[l],
                 norm_mlp_g=norm_mlp_g[l], w_up=w_up[l], w_down=w_down[l])
        last = l == depth - 1
        nl = l if last else l + 1
        nxt = dict(g=norm_final_g if last else norm_mix_g[nl], wg_hi=gate_params[nl][0],
                   wg_lo=gate_params[nl][1], bg=gate_params[nl][2], last=last)
        outs = _layer(xt, h, gates, p, nxt, batch=batch, seq=seq)
        if last:
            xt = outs[0]
        else:
            xt, h, gates = outs
    return xt.reshape(batch, seq, D)
```

```python
import functools

import jax
import jax.numpy as jnp
from jax import lax
from jax.experimental import pallas as pl
from jax.experimental.pallas import tpu as pltpu

D_MODEL = 1024
ML_HEADS = 4
ML_HEAD_DIM = 256
SB_HEADS = 16
SB_HEAD_DIM = 64
D_FF = 4 * D_MODEL
EPS = 1e-6

VMEM_LIMIT_BYTES = 56 * 1024 * 1024

ML_CHUNK = 256
SB_BLOCK = 256
SB_GROUP = 4
LOG2E = 1.4426950408889634
SB_DEAD_LOG2 = -152.0
SB_ABSENT_LOG2 = -1e30
TM_PROJ = 1024
TM_MERGE = 512
TM_MLP = 512
PREV_ROWS = 16

BF16 = jnp.bfloat16
F32 = jnp.float32


def _params(*sem):
    return pltpu.CompilerParams(dimension_semantics=sem, vmem_limit_bytes=VMEM_LIMIT_BYTES)


def _dot(a, b):
    return jnp.dot(a, b, preferred_element_type=F32)


def _dot_nt(a, b):
    return lax.dot_general(a, b, (((1,), (1,)), ((), ())), preferred_element_type=F32)


def _dot_tn(a, b):
    return lax.dot_general(a, b, (((0,), (0,)), ((), ())), preferred_element_type=F32)


def _rmsnorm_rows(x, g):
    ms = jnp.mean(x * x, axis=-1, keepdims=True)
    return x * lax.rsqrt(ms + EPS) * g


def _log_sigmoid(x):
    return jnp.minimum(x, 0.0) - jnp.log(1.0 + jnp.exp(-jnp.abs(x)))


def _split_bf16(x):
    hi = x.astype(BF16)
    lo = (x - hi.astype(F32)).astype(BF16)
    return hi, lo


def _prenorm_outputs(x, g, wg_hi, wg_lo, bg):
    h_hi, h_lo = _split_bf16(_rmsnorm_rows(x, g))
    gates = _dot_nt(wg_hi, h_hi) + (_dot_nt(wg_hi, h_lo) + _dot_nt(wg_lo, h_hi)) + bg
    return h_hi, gates


def _prenorm_kernel(x_ref, g_ref, wg_hi_ref, wg_lo_ref, bg_ref, h_ref, gates_ref):
    h_ref[...], gates_ref[...] = _prenorm_outputs(x_ref[...], g_ref[...], wg_hi_ref[...],
                                                  wg_lo_ref[...], bg_ref[...])


def _prenorm(x, g, wg_hi, wg_lo, bg):
    T, D = x.shape
    G = wg_hi.shape[0]
    tm = min(TM_PROJ, T)
    const = lambda i: (0, 0)
    return pl.pallas_call(
        _prenorm_kernel,
        grid=(T // tm,),
        in_specs=[
            pl.BlockSpec((tm, D), lambda i: (i, 0)),
            pl.BlockSpec((1, D), const),
            pl.BlockSpec((G, D), const),
            pl.BlockSpec((G, D), const),
            pl.BlockSpec((G, 1), const),
        ],
        out_specs=[pl.BlockSpec((tm, D), lambda i: (i, 0)), pl.BlockSpec((G, tm), lambda i: (0, i))],
        out_shape=[jax.ShapeDtypeStruct((T, D), BF16), jax.ShapeDtypeStruct((G, T), F32)],
        compiler_params=_params("parallel"),
        name="prenorm",
    )(x, g, wg_hi, wg_lo, bg)


def _proj_kernel(h_ref, w_ref, o_ref):
    o_ref[...] = _dot(h_ref[...], w_ref[...]).astype(o_ref.dtype)


def _proj(h, w, out_dtype, tn=1024):
    T, D = h.shape
    N = w.shape[1]
    tm = min(TM_PROJ, T)
    return pl.pallas_call(
        _proj_kernel,
        grid=(T // tm, N // tn),
        in_specs=[
            pl.BlockSpec((tm, D), lambda i, j: (i, 0)),
            pl.BlockSpec((D, tn), lambda i, j: (0, j)),
        ],
        out_specs=pl.BlockSpec((tm, tn), lambda i, j: (i, j)),
        out_shape=jax.ShapeDtypeStruct((T, N), out_dtype),
        compiler_params=_params("parallel", "parallel"),
        name="proj",
    )(h, w)


def _proj_t_kernel(scale_ref, h_ref, w_ref, o_ref):
    scale = scale_ref[pl.program_id(1)]
    out = lax.dot_general(w_ref[...], h_ref[...], (((0,), (1,)), ((), ())), preferred_element_type=F32)
    o_ref[...] = (out * scale).astype(o_ref.dtype)


def _proj_t(h, w, scales, tn=1024):
    T, D = h.shape
    N = w.shape[1]
    tm = min(TM_PROJ, T)
    return pl.pallas_call(
        _proj_t_kernel,
        grid=(T // tm, N // tn),
        in_specs=[
            pl.BlockSpec(memory_space=pltpu.SMEM),
            pl.BlockSpec((tm, D), lambda i, j: (i, 0)),
            pl.BlockSpec((D, tn), lambda i, j: (0, j)),
        ],
        out_specs=pl.BlockSpec((tn, tm), lambda i, j: (j, i)),
        out_shape=jax.ShapeDtypeStruct((N, T), BF16),
        compiler_params=_params("parallel", "parallel"),
        name="proj_t",
    )(scales, h, w)


def _cumsum_lanes(x):
    n = x.shape[-1]
    lane = lax.broadcasted_iota(jnp.int32, x.shape, x.ndim - 1)
    k = 1
    while k < n:
        x = x + jnp.where(lane >= k, pltpu.roll(x, k, axis=x.ndim - 1), 0.0)
        k *= 2
    return x


def _mlstm_kernel(q_ref, kT_ref, v_ref, o_ref, gates_ref, gain_ref, y_ref, c_ref, n_ref, m_ref):
    @pl.when(pl.program_id(1) == 0)
    def _():
        c_ref[...] = jnp.zeros_like(c_ref)
        n_ref[...] = jnp.zeros_like(n_ref)
        m_ref[...] = jnp.zeros_like(m_ref)

    L = q_ref.shape[0]
    dh = ML_HEAD_DIM
    heads = range(ML_HEADS)
    cols = lambda g: slice(g * dh, (g + 1) * dh)
    t_idx = lax.broadcasted_iota(jnp.int32, (L, L), 0)
    s_idx = lax.broadcasted_iota(jnp.int32, (L, L), 1)
    causal = s_idx <= t_idx

    q = [q_ref[:, cols(g)] for g in heads]
    c_prev = [c_ref[g] for g in heads]
    n_prev = [n_ref[g] for g in heads]
    m_prev = [m_ref[g] for g in heads]
    qk = [_dot(q[g], kT_ref[cols(g), :]) for g in heads]
    qc = [_dot(q[g], c_prev[g].astype(BF16)) for g in heads]
    nq = [_dot(q[g], jnp.broadcast_to(n_prev[g], (dh, 128)).astype(BF16))[:, 0:1] for g in heads]

    lf_all = _log_sigmoid(gates_ref[...])
    b_all = _cumsum_lanes(lf_all)
    m_t, a_inter, s, num = [], [], [], []
    b_row, i_row = [], []
    for g in heads:
        i_row.append(gates_ref[g:g + 1, :])
        lf_row = lf_all[ML_HEADS + g:ML_HEADS + g + 1, :]
        b_row.append(b_all[ML_HEADS + g:ML_HEADS + g + 1, :])
        b_col = jnp.sum(jnp.where(causal, lf_row, 0.0), axis=1, keepdims=True)
        dmat = jnp.where(causal, b_col - b_row[g] + i_row[g], -jnp.inf)
        inter = b_col + m_prev[g]
        m_t.append(jnp.maximum(inter, jnp.max(dmat, axis=1, keepdims=True)))
        a_inter.append(jnp.exp(inter - m_t[g]))
        s.append(qk[g] * jnp.exp(dmat - m_t[g]))
        num.append(_dot(s[g].astype(BF16), v_ref[:, cols(g)]))

    for g in heads:
        b_last = b_row[g][:, L - 1:L]
        g_row = b_last - b_row[g] + i_row[g]
        m_new = jnp.maximum(b_last + m_prev[g], jnp.max(g_row, axis=1, keepdims=True))
        decay = jnp.exp(b_last + m_prev[g] - m_new)
        kw = kT_ref[cols(g), :].astype(F32) * jnp.exp(g_row - m_new)
        c_ref[g] = decay * c_prev[g] + _dot(kw.astype(BF16), v_ref[:, cols(g)])
        n_ref[g] = decay * n_prev[g] + jnp.sum(kw, axis=1, keepdims=True)
        m_ref[g] = m_new

    for g in heads:
        den = jnp.sum(s[g], axis=1, keepdims=True) + a_inter[g] * nq[g]
        h = (num[g] + a_inter[g] * qc[g]) / jnp.maximum(jnp.abs(den), jnp.exp(-m_t[g]))
        hg = jax.nn.sigmoid(o_ref[:, cols(g)]) * h
        ms = jnp.mean(hg * hg, axis=-1, keepdims=True)
        y_ref[:, cols(g)] = (hg * lax.rsqrt(ms + EPS) * gain_ref[:, cols(g)]).astype(y_ref.dtype)


def _mlstm(pn_b, pn_f, pt, gates, gain, *, batch, seq, q_col, v_col, o_col, kT_row):
    T = batch * seq
    L = min(ML_CHUNK, seq)
    nc = seq // L
    dh = ML_HEAD_DIM
    W = ML_HEADS * dh
    tok = lambda b, c: b * nc + c
    return pl.pallas_call(
        _mlstm_kernel,
        grid=(batch, nc),
        in_specs=[
            pl.BlockSpec((L, W), lambda b, c: (tok(b, c), q_col)),
            pl.BlockSpec((W, L), lambda b, c: (kT_row, tok(b, c))),
            pl.BlockSpec((L, W), lambda b, c: (tok(b, c), v_col)),
            pl.BlockSpec((L, W), lambda b, c: (tok(b, c), o_col)),
            pl.BlockSpec((2 * ML_HEADS, L), lambda b, c: (0, tok(b, c))),
            pl.BlockSpec((1, W), lambda b, c: (0, 0)),
        ],
        out_specs=pl.BlockSpec((L, W), lambda b, c: (tok(b, c), 0)),
        out_shape=jax.ShapeDtypeStruct((T, W), BF16),
        scratch_shapes=[
            pltpu.VMEM((ML_HEADS, dh, dh), F32),
            pltpu.VMEM((ML_HEADS, dh, 1), F32),
            pltpu.VMEM((ML_HEADS, 1, 1), F32),
        ],
        compiler_params=_params("parallel", "arbitrary"),
        name="mlstm",
    )(pn_b, pt, pn_b, pn_f, gates, gain)


def _sb_kernel(qT_ref, k_ref, vT_ref, o_ref, acc_ref, *, blk, heads):
    i = pl.program_id(2)
    d = SB_HEAD_DIM

    s_idx = lax.broadcasted_iota(jnp.int32, (blk, blk), 0)
    t_idx = lax.broadcasted_iota(jnp.int32, (blk, blk), 1)
    later = (t_idx > s_idx)
    upper = jnp.where(later, 1.0, 0.0).astype(BF16)

    row_head = lax.broadcasted_iota(jnp.int32, (2 * d, blk), 0) // d
    q_pads = []
    for g in range(heads):
        q2 = qT_ref[(g // 2) * 2 * d:(g // 2 + 1) * 2 * d, :]
        q_pads.append(jnp.where(row_head == g % 2, q2, jnp.zeros_like(q2)))

    def visit(blocks, rs, init_acc):
        starts = [pl.multiple_of(j * blk, blk) for j, _, _ in blocks]
        z = [[_dot(k_ref[pl.ds(st, blk), (g // 2) * 2 * d:(g // 2 + 1) * 2 * d], q_pads[g])
              for g in range(heads)] for st in starts]
        log_beta, block_sum, suffix = [], [], []
        for b, (_, diagonal, _) in enumerate(blocks):
            log_beta.append([])
            block_sum.append([])
            suffix.append([])
            for g in range(heads):
                zs = z[b][g]
                neg_abs = pltpu.bitcast(pltpu.bitcast(zs, jnp.uint32) | jnp.uint32(0x80000000), F32)
                lb = jnp.minimum(zs, 0.0) - jnp.log(1.0 + jnp.exp2(neg_abs)) * LOG2E
                l1 = lb - zs
                if diagonal:
                    l1 = jnp.where(later, l1, 0.0)
                hi, lo = _split_bf16(l1)
                sfx = _dot(upper, hi) + _dot(upper, lo)
                log_beta[b].append(lb)
                suffix[b].append(sfx)
                block_sum[b].append(sfx[0:1, :] + l1[0:1, :])
        rs = list(rs)
        for b, (_, diagonal, r_gate) in enumerate(blocks):
            for g in range(heads):
                r_in = rs[g] if r_gate is None else jnp.where(r_gate, rs[g], SB_ABSENT_LOG2)
                a = jnp.exp2(log_beta[b][g] + suffix[b][g] + r_in)
                if diagonal:
                    a = jnp.where(later, a, 0.0)
                contrib = _dot(vT_ref[g * d:(g + 1) * d, pl.ds(starts[b], blk)], a.astype(BF16))
                if init_acc and b == 0:
                    acc_ref[g * d:(g + 1) * d, :] = contrib
                else:
                    acc_ref[g * d:(g + 1) * d, :] += contrib
                rs[g] = r_in + block_sum[b][g]
        return tuple(rs)

    def live(rs):
        r_max = functools.reduce(jnp.maximum, rs)
        return jnp.max(r_max) > SB_DEAD_LOG2

    def cond(c):
        n, rs = c
        return jnp.logical_and(n < i - 1, live(rs))

    def body(c):
        n, rs = c
        return n + 1, visit([(i - 2 - n, False, None)], rs, False)

    rs = visit([(i, True, None), (jnp.maximum(i - 1, 0), False, i > 0)],
               tuple(jnp.zeros((1, blk), F32) for _ in range(heads)), True)
    lax.while_loop(cond, body, (jnp.int32(0), rs))
    o_ref[...] = acc_ref[...].astype(o_ref.dtype)


def _stick_breaking(pn_b, pt, *, batch, seq, k_col, qT_row, vT_row):
    T = batch * seq
    blk = min(SB_BLOCK, seq)
    nq = seq // blk
    gd = SB_GROUP * SB_HEAD_DIM
    return pl.pallas_call(
        functools.partial(_sb_kernel, blk=blk, heads=SB_GROUP),
        grid=(batch, SB_HEADS // SB_GROUP, nq),
        in_specs=[
            pl.BlockSpec((gd, blk), lambda b, h, i: (qT_row + h, b * nq + i)),
            pl.BlockSpec((seq, gd), lambda b, h, i: (b, k_col + h)),
            pl.BlockSpec((gd, seq), lambda b, h, i: (vT_row + h, b)),
        ],
        out_specs=pl.BlockSpec((gd, blk), lambda b, h, i: (h, b * nq + i)),
        out_shape=jax.ShapeDtypeStruct((SB_HEADS * SB_HEAD_DIM, T), BF16),
        scratch_shapes=[pltpu.VMEM((gd, blk), F32)],
        compiler_params=_params("parallel", "parallel", "arbitrary"),
        name="stick_breaking",
    )(pt, pn_b, pt)


def _merge_kernel(x_ref, h_ref, hp_ref, win_ref, yml_ref, ysbT_ref, cw_ref,
                  wml_ref, wsc_ref, wsb_ref, wout_ref, o_ref, *, tiles_per_seq):
    tm, D = x_ref.shape
    P = hp_ref.shape[0]
    first = (pl.program_id(0) % tiles_per_seq) == 0
    h = h_ref[...]
    h_ext = jnp.concatenate([hp_ref[...], h], axis=0)
    col = lambda n: slice(n * D, (n + 1) * D)

    z = _dot(h_ext, win_ref[:, col(1)]) * _dot(h_ext, win_ref[:, col(2)])
    row = lax.broadcasted_iota(jnp.int32, z.shape, 0)
    z = jnp.where(jnp.logical_and(first, row < P), 0.0, z)
    cw = cw_ref[...]
    conv = (cw[0:1, :] * z[P - 2:P - 2 + tm, :] + cw[1:2, :] * z[P - 1:P - 1 + tm, :]
            + cw[2:3, :] * z[P:P + tm, :])
    y_sc = _dot(h, win_ref[:, col(0)]) * conv

    merged = jax.nn.sigmoid(_dot(h, win_ref[:, col(3)])) * _dot(yml_ref[...], wml_ref[...])
    merged += jax.nn.sigmoid(_dot(h, win_ref[:, col(4)])) * _dot(y_sc.astype(BF16), wsc_ref[...])
    merged += jax.nn.sigmoid(_dot(h, win_ref[:, col(5)])) * _dot_tn(ysbT_ref[...], wsb_ref[...])
    o_ref[...] = x_ref[...] + _dot(merged.astype(BF16), wout_ref[...])


def _merge(x, h, w_in6, y_ml, y_sbT, conv_w, w_ml, w_sc, w_sb, w_out, *, seq):
    T, D = x.shape
    tm = min(TM_MERGE, seq)
    const = lambda i: (0, 0)
    resident = lambda shape: pl.BlockSpec(shape, const, pipeline_mode=pl.Buffered(1))
    return pl.pallas_call(
        functools.partial(_merge_kernel, tiles_per_seq=seq // tm),
        grid=(T // tm,),
        in_specs=[
            pl.BlockSpec((tm, D), lambda i: (i, 0)),
            pl.BlockSpec((tm, D), lambda i: (i, 0)),
            pl.BlockSpec((PREV_ROWS, D), lambda i: (jnp.maximum(i * (tm // PREV_ROWS) - 1, 0), 0)),
            resident((D, 6 * D)),
            pl.BlockSpec((tm, D), lambda i: (i, 0)),
            pl.BlockSpec((D, tm), lambda i: (0, i)),
            resident((3, D)),
            resident((D, D)), resident((D, D)), resident((D, D)), resident((D, D)),
        ],
        out_specs=pl.BlockSpec((tm, D), lambda i: (i, 0)),
        out_shape=jax.ShapeDtypeStruct((T, D), F32),
        compiler_params=_params("parallel"),
        name="merge",
    )(x, h, h, w_in6, y_ml, y_sbT, conv_w, w_ml, w_sc, w_sb, w_out)


def _mlp_kernel(x_ref, g_ref, wup_ref, wdown_ref, gn_ref, wg_hi_ref, wg_lo_ref, bg_ref, *out_refs,
                ff_chunk, last):
    x = x_ref[...]
    h = _rmsnorm_rows(x, g_ref[...]).astype(BF16)
    acc = x
    for c in range(wup_ref.shape[1] // ff_chunk):
        cols = slice(c * ff_chunk, (c + 1) * ff_chunk)
        up = jnp.maximum(_dot(h, wup_ref[:, cols]), 0.0)
        acc = acc + _dot((up * up).astype(BF16), wdown_ref[cols, :])
    if last:
        out_refs[0][...] = _rmsnorm_rows(acc, gn_ref[...])
    else:
        out_refs[0][...] = acc
        out_refs[1][...], out_refs[2][...] = _prenorm_outputs(acc, gn_ref[...], wg_hi_ref[...],
                                                              wg_lo_ref[...], bg_ref[...])


def _mlp(x, g, w_up, w_down, g_next, wg_hi, wg_lo, bg, *, last):
    T, D = x.shape
    F = w_up.shape[1]
    G = wg_hi.shape[0]
    tm = min(TM_MLP, T)
    const = lambda i: (0, 0)
    resident = lambda shape: pl.BlockSpec(shape, const, pipeline_mode=pl.Buffered(1))
    out_specs = [pl.BlockSpec((tm, D), lambda i: (i, 0))]
    out_shape = [jax.ShapeDtypeStruct((T, D), F32)]
    if not last:
        out_specs += [pl.BlockSpec((tm, D), lambda i: (i, 0)), pl.BlockSpec((G, tm), lambda i: (0, i))]
        out_shape += [jax.ShapeDtypeStruct((T, D), BF16), jax.ShapeDtypeStruct((G, T), F32)]
    return pl.pallas_call(
        functools.partial(_mlp_kernel, ff_chunk=1024, last=last),
        grid=(T // tm,),
        in_specs=[
            pl.BlockSpec((tm, D), lambda i: (i, 0)),
            resident((1, D)),
            resident((D, F)),
            resident((F, D)),
            resident((1, D)),
            resident((G, D)), resident((G, D)), resident((G, 1)),
        ],
        out_specs=out_specs,
        out_shape=out_shape,
        compiler_params=_params("parallel"),
        name="mlp",
    )(x, g, w_up, w_down, g_next, wg_hi, wg_lo, bg)


def _gate_params(w_in, l, b_if):
    off = 4 * ML_HEADS * ML_HEAD_DIM
    wg_hi, wg_lo = _split_bf16(w_in[l, :, off:off + 2 * ML_HEADS].T)
    return wg_hi, wg_lo, b_if.reshape(2 * ML_HEADS, 1)


def _layer(x, h, gates, p, nxt, *, batch, seq):
    D = D_MODEL
    ml_w = ML_HEADS * ML_HEAD_DIM
    w_in, l = p["w_in"]
    off = 0

    def take(n):
        nonlocal off
        w = w_in[l, :, off:off + n]
        off += n
        return w

    w_mq, w_mk, w_mv, w_mo = take(ml_w), take(ml_w), take(ml_w), take(ml_w)
    take(2 * ML_HEADS)
    w_b, w_c, w_u = take(D), take(D), take(D)
    w_sq, w_sk, w_sv = take(D), take(D), take(D)
    w_gml, w_gsc, w_gsb = take(D), take(D), take(D)

    wn_b = jnp.concatenate([w_mq, w_mv, w_sk], axis=1).astype(BF16)
    wn_f = w_mo.astype(BF16)
    w_in6 = jnp.concatenate([w_b, w_c, w_u, w_gml, w_gsc, w_gsb], axis=1).astype(BF16)
    wt = jnp.concatenate([w_mk, w_sq, w_sv], axis=1).astype(BF16)
    scales = jnp.array([ML_HEAD_DIM ** -0.5, SB_HEAD_DIM ** -0.5 * LOG2E, 1.0], F32)

    pn_b = _proj(h, wn_b, BF16)
    pn_f = _proj(h, wn_f, F32)
    pt = _proj_t(h, wt, scales)

    y_ml = _mlstm(pn_b, pn_f, pt, gates, p["ml_norm_g"].reshape(1, ml_w), batch=batch, seq=seq,
                  q_col=0, v_col=1, o_col=0, kT_row=0)
    y_sbT = _stick_breaking(pn_b, pt, batch=batch, seq=seq,
                            k_col=(2 * ml_w) // (SB_GROUP * SB_HEAD_DIM),
                            qT_row=ml_w // (SB_GROUP * SB_HEAD_DIM),
                            vT_row=(ml_w + D) // (SB_GROUP * SB_HEAD_DIM))
    x = _merge(x, h, w_in6, y_ml, y_sbT, p["conv_w"],
               p["w_ml_proj"].astype(BF16), p["w_sc_proj"].astype(BF16),
               p["w_sb_proj"].astype(BF16), p["w_out"].astype(BF16), seq=seq)
    return _mlp(x, p["norm_mlp_g"].reshape(1, D), p["w_up"].astype(BF16), p["w_down"].astype(BF16),
                nxt["g"].reshape(1, D), nxt["wg_hi"], nxt["wg_lo"], nxt["bg"], last=nxt["last"])


def kernel(x, norm_mix_g, w_in, b_if, ml_norm_g, conv_w, w_ml_proj, w_sc_proj, w_sb_proj, w_out,
           norm_mlp_g, w_up, w_down, norm_final_g):
    batch, seq, D = x.shape
    depth = w_in.shape[0]
    xt = x.reshape(batch * seq, D)
    gate_params = [_gate_params(w_in, l, b_if[l]) for l in range(depth)]
    h, gates = _prenorm(xt, norm_mix_g[0].reshape(1, D), *gate_params[0])
    for l in range(depth):
        p = dict(w_in=(w_in, l), ml_norm_g=ml_norm_g[l], conv_w=conv_w[l], w_ml_proj=w_ml_proj[l],
                 w_sc_proj=w_sc_proj[l], w_sb_proj=w_sb_proj[l], w_out=w_out[l],
                 norm_mlp_g=norm_mlp_g[l], w_up=w_up[l], w_down=w_down[l])
        last = l == depth - 1
        nl = l if last else l + 1
        nxt = dict(g=norm_final_g if last else norm_mix_g[nl], wg_hi=gate_params[nl][0],
                   wg_lo=gate_params[nl][1], bg=gate_params[nl][2], last=last)
        outs = _layer(xt, h, gates, p, nxt, batch=batch, seq=seq)
        if last:
            xt = outs[0]
        else:
            xt, h, gates = outs
    return xt.reshape(batch, seq, D)
```

```python
import functools

import jax
import jax.numpy as jnp
from jax import lax
from jax.experimental import pallas as pl
from jax.experimental.pallas import tpu as pltpu

D_MODEL = 1024
ML_HEADS = 4
ML_HEAD_DIM = 256
SB_HEADS = 16
SB_HEAD_DIM = 64
D_FF = 4 * D_MODEL
EPS = 1e-6

VMEM_LIMIT_BYTES = 56 * 1024 * 1024

ML_CHUNK = 256
SB_BLOCK = 256
SB_GROUP = 4
LOG2E = 1.4426950408889634
SB_DEAD_LOG2 = -152.0
SB_ABSENT_LOG2 = -1e30
TM_PROJ = 2048
TM_MERGE = 512
TM_MLP = 512
PREV_ROWS = 16

BF16 = jnp.bfloat16
F32 = jnp.float32


def _params(*sem):
    return pltpu.CompilerParams(dimension_semantics=sem, vmem_limit_bytes=VMEM_LIMIT_BYTES)


def _dot(a, b):
    return jnp.dot(a, b, preferred_element_type=F32)


def _dot_nt(a, b):
    return lax.dot_general(a, b, (((1,), (1,)), ((), ())), preferred_element_type=F32)


def _dot_tn(a, b):
    return lax.dot_general(a, b, (((0,), (0,)), ((), ())), preferred_element_type=F32)


def _rmsnorm_rows(x, g):
    ms = jnp.mean(x * x, axis=-1, keepdims=True)
    return x * lax.rsqrt(ms + EPS) * g


def _log_sigmoid(x):
    return jnp.minimum(x, 0.0) - jnp.log(1.0 + jnp.exp(-jnp.abs(x)))


def _split_bf16(x):
    hi = x.astype(BF16)
    lo = (x - hi.astype(F32)).astype(BF16)
    return hi, lo


def _prenorm_outputs(x, g, wg_hi, wg_lo, bg):
    h_hi, h_lo = _split_bf16(_rmsnorm_rows(x, g))
    gates = _dot_nt(wg_hi, h_hi) + (_dot_nt(wg_hi, h_lo) + _dot_nt(wg_lo, h_hi)) + bg
    return h_hi, gates


def _prenorm_kernel(x_ref, g_ref, wg_hi_ref, wg_lo_ref, bg_ref, h_ref, gates_ref):
    h_ref[...], gates_ref[...] = _prenorm_outputs(x_ref[...], g_ref[...], wg_hi_ref[...],
                                                  wg_lo_ref[...], bg_ref[...])


def _prenorm(x, g, wg_hi, wg_lo, bg):
    T, D = x.shape
    G = wg_hi.shape[0]
    tm = min(TM_PROJ, T)
    const = lambda i: (0, 0)
    return pl.pallas_call(
        _prenorm_kernel,
        grid=(T // tm,),
        in_specs=[
            pl.BlockSpec((tm, D), lambda i: (i, 0)),
            pl.BlockSpec((1, D), const),
            pl.BlockSpec((G, D), const),
            pl.BlockSpec((G, D), const),
            pl.BlockSpec((G, 1), const),
        ],
        out_specs=[pl.BlockSpec((tm, D), lambda i: (i, 0)), pl.BlockSpec((G, tm), lambda i: (0, i))],
        out_shape=[jax.ShapeDtypeStruct((T, D), BF16), jax.ShapeDtypeStruct((G, T), F32)],
        compiler_params=_params("parallel"),
        name="prenorm",
    )(x, g, wg_hi, wg_lo, bg)


def _proj_kernel(h_ref, w_ref, o_ref):
    o_ref[...] = _dot(h_ref[...], w_ref[...]).astype(o_ref.dtype)


def _proj(h, w, out_dtype, tn=1024):
    T, D = h.shape
    N = w.shape[1]
    tm = min(TM_PROJ, T)
    return pl.pallas_call(
        _proj_kernel,
        grid=(T // tm, N // tn),
        in_specs=[
            pl.BlockSpec((tm, D), lambda i, j: (i, 0)),
            pl.BlockSpec((D, tn), lambda i, j: (0, j)),
        ],
        out_specs=pl.BlockSpec((tm, tn), lambda i, j: (i, j)),
        out_shape=jax.ShapeDtypeStruct((T, N), out_dtype),
        compiler_params=_params("parallel", "parallel"),
        name="proj",
    )(h, w)


def _proj_t_kernel(scale_ref, h_ref, w_ref, o_ref):
    scale = scale_ref[pl.program_id(1)]
    out = lax.dot_general(w_ref[...], h_ref[...], (((0,), (1,)), ((), ())), preferred_element_type=F32)
    o_ref[...] = (out * scale).astype(o_ref.dtype)


def _proj_t(h, w, scales, tn=1024):
    T, D = h.shape
    N = w.shape[1]
    tm = min(TM_PROJ, T)
    return pl.pallas_call(
        _proj_t_kernel,
        grid=(T // tm, N // tn),
        in_specs=[
            pl.BlockSpec(memory_space=pltpu.SMEM),
            pl.BlockSpec((tm, D), lambda i, j: (i, 0)),
            pl.BlockSpec((D, tn), lambda i, j: (0, j)),
        ],
        out_specs=pl.BlockSpec((tn, tm), lambda i, j: (j, i)),
        out_shape=jax.ShapeDtypeStruct((N, T), BF16),
        compiler_params=_params("parallel", "parallel"),
        name="proj_t",
    )(scales, h, w)


def _cumsum_lanes(x):
    n = x.shape[-1]
    lane = lax.broadcasted_iota(jnp.int32, x.shape, x.ndim - 1)
    k = 1
    while k < n:
        x = x + jnp.where(lane >= k, pltpu.roll(x, k, axis=x.ndim - 1), 0.0)
        k *= 2
    return x


def _mlstm_kernel(q_ref, kT_ref, v_ref, o_ref, gates_ref, gain_ref, y_ref, c_ref, n_ref, m_ref):
    @pl.when(pl.program_id(1) == 0)
    def _():
        c_ref[...] = jnp.zeros_like(c_ref)
        n_ref[...] = jnp.zeros_like(n_ref)
        m_ref[...] = jnp.zeros_like(m_ref)

    L = q_ref.shape[0]
    dh = ML_HEAD_DIM
    heads = range(ML_HEADS)
    cols = lambda g: slice(g * dh, (g + 1) * dh)
    t_idx = lax.broadcasted_iota(jnp.int32, (L, L), 0)
    s_idx = lax.broadcasted_iota(jnp.int32, (L, L), 1)
    causal = s_idx <= t_idx

    q = [q_ref[:, cols(g)] for g in heads]
    c_prev = [c_ref[g] for g in heads]
    n_prev = [n_ref[g] for g in heads]
    m_prev = [m_ref[g] for g in heads]
    qk = [_dot(q[g], kT_ref[cols(g), :]) for g in heads]
    qc = [_dot(q[g], c_prev[g].astype(BF16)) for g in heads]
    nq = [_dot(q[g], jnp.broadcast_to(n_prev[g], (dh, 128)).astype(BF16))[:, 0:1] for g in heads]

    lf_all = _log_sigmoid(gates_ref[...])
    b_all = _cumsum_lanes(lf_all)
    m_t, a_inter, s, num = [], [], [], []
    b_row, i_row = [], []
    for g in heads:
        i_row.append(gates_ref[g:g + 1, :])
        lf_row = lf_all[ML_HEADS + g:ML_HEADS + g + 1, :]
        b_row.append(b_all[ML_HEADS + g:ML_HEADS + g + 1, :])
        b_col = jnp.sum(jnp.where(causal, lf_row, 0.0), axis=1, keepdims=True)
        dmat = jnp.where(causal, b_col - b_row[g] + i_row[g], -jnp.inf)
        inter = b_col + m_prev[g]
        m_t.append(jnp.maximum(inter, jnp.max(dmat, axis=1, keepdims=True)))
        a_inter.append(jnp.exp(inter - m_t[g]))
        s.append(qk[g] * jnp.exp(dmat - m_t[g]))
        num.append(_dot(s[g].astype(BF16), v_ref[:, cols(g)]))

    for g in heads:
        b_last = b_row[g][:, L - 1:L]
        g_row = b_last - b_row[g] + i_row[g]
        m_new = jnp.maximum(b_last + m_prev[g], jnp.max(g_row, axis=1, keepdims=True))
        decay = jnp.exp(b_last + m_prev[g] - m_new)
        kw = kT_ref[cols(g), :].astype(F32) * jnp.exp(g_row - m_new)
        c_ref[g] = decay * c_prev[g] + _dot(kw.astype(BF16), v_ref[:, cols(g)])
        n_ref[g] = decay * n_prev[g] + jnp.sum(kw, axis=1, keepdims=True)
        m_ref[g] = m_new

    for g in heads:
        den = jnp.sum(s[g], axis=1, keepdims=True) + a_inter[g] * nq[g]
        h = (num[g] + a_inter[g] * qc[g]) / jnp.maximum(jnp.abs(den), jnp.exp(-m_t[g]))
        hg = jax.nn.sigmoid(o_ref[:, cols(g)]) * h
        ms = jnp.mean(hg * hg, axis=-1, keepdims=True)
        y_ref[:, cols(g)] = (hg * lax.rsqrt(ms + EPS) * gain_ref[:, cols(g)]).astype(y_ref.dtype)


def _mlstm(pn_b, pn_f, pt, gates, gain, *, batch, seq, q_col, v_col, o_col, kT_row):
    T = batch * seq
    L = min(ML_CHUNK, seq)
    nc = seq // L
    dh = ML_HEAD_DIM
    W = ML_HEADS * dh
    tok = lambda b, c: b * nc + c
    return pl.pallas_call(
        _mlstm_kernel,
        grid=(batch, nc),
        in_specs=[
            pl.BlockSpec((L, W), lambda b, c: (tok(b, c), q_col)),
            pl.BlockSpec((W, L), lambda b, c: (kT_row, tok(b, c))),
            pl.BlockSpec((L, W), lambda b, c: (tok(b, c), v_col)),
            pl.BlockSpec((L, W), lambda b, c: (tok(b, c), o_col)),
            pl.BlockSpec((2 * ML_HEADS, L), lambda b, c: (0, tok(b, c))),
            pl.BlockSpec((1, W), lambda b, c: (0, 0)),
        ],
        out_specs=pl.BlockSpec((L, W), lambda b, c: (tok(b, c), 0)),
        out_shape=jax.ShapeDtypeStruct((T, W), BF16),
        scratch_shapes=[
            pltpu.VMEM((ML_HEADS, dh, dh), F32),
            pltpu.VMEM((ML_HEADS, dh, 1), F32),
            pltpu.VMEM((ML_HEADS, 1, 1), F32),
        ],
        compiler_params=_params("parallel", "arbitrary"),
        name="mlstm",
    )(pn_b, pt, pn_b, pn_f, gates, gain)


def _sb_kernel(qT_ref, k_ref, vT_ref, o_ref, acc_ref, *, blk, heads):
    i = pl.program_id(2)
    d = SB_HEAD_DIM

    s_idx = lax.broadcasted_iota(jnp.int32, (blk, blk), 0)
    t_idx = lax.broadcasted_iota(jnp.int32, (blk, blk), 1)
    later = (t_idx > s_idx)
    upper = jnp.where(later, 1.0, 0.0).astype(BF16)

    row_head = lax.broadcasted_iota(jnp.int32, (2 * d, blk), 0) // d
    q_pads = []
    for g in range(heads):
        q2 = qT_ref[(g // 2) * 2 * d:(g // 2 + 1) * 2 * d, :]
        q_pads.append(jnp.where(row_head == g % 2, q2, jnp.zeros_like(q2)))

    def visit(blocks, rs, init_acc):
        starts = [pl.multiple_of(j * blk, blk) for j, _, _ in blocks]
        z = [[_dot(k_ref[pl.ds(st, blk), (g // 2) * 2 * d:(g // 2 + 1) * 2 * d], q_pads[g])
              for g in range(heads)] for st in starts]
        log_beta, block_sum, suffix = [], [], []
        for b, (_, diagonal, _) in enumerate(blocks):
            log_beta.append([])
            block_sum.append([])
            suffix.append([])
            for g in range(heads):
                zs = z[b][g]
                neg_abs = pltpu.bitcast(pltpu.bitcast(zs, jnp.uint32) | jnp.uint32(0x80000000), F32)
                lb = jnp.minimum(zs, 0.0) - jnp.log(1.0 + jnp.exp2(neg_abs)) * LOG2E
                l1 = lb - zs
                if diagonal:
                    l1 = jnp.where(later, l1, 0.0)
                sfx = _dot(upper, l1.astype(BF16))
                log_beta[b].append(lb)
                suffix[b].append(sfx)
                block_sum[b].append(sfx[0:1, :] + l1[0:1, :])
        rs = list(rs)
        r_in = []
        for b, (_, _, r_gate) in enumerate(blocks):
            r_in.append([rs[g] if r_gate is None else jnp.where(r_gate, rs[g], SB_ABSENT_LOG2)
                         for g in range(heads)])
            rs = [r_in[b][g] + block_sum[b][g] for g in range(heads)]
        live = jnp.max(functools.reduce(jnp.maximum, rs)) > SB_DEAD_LOG2
        for b, (_, diagonal, _) in enumerate(blocks):
            for g in range(heads):
                a = jnp.exp2(log_beta[b][g] + suffix[b][g] + r_in[b][g])
                if diagonal:
                    a = jnp.where(later, a, 0.0)
                contrib = _dot(vT_ref[g * d:(g + 1) * d, pl.ds(starts[b], blk)], a.astype(BF16))
                if init_acc and b == 0:
                    acc_ref[g * d:(g + 1) * d, :] = contrib
                else:
                    acc_ref[g * d:(g + 1) * d, :] += contrib
        return live, tuple(rs)

    def cond(c):
        n, live, _ = c
        return jnp.logical_and(n < i - 1, live)

    def body(c):
        n, _, rs = c
        return (n + 1,) + visit([(i - 2 - n, False, None)], rs, False)

    first = visit([(i, True, None), (jnp.maximum(i - 1, 0), False, i > 0)],
                  tuple(jnp.zeros((1, blk), F32) for _ in range(heads)), True)
    lax.while_loop(cond, body, (jnp.int32(0),) + first)
    o_ref[...] = acc_ref[...].astype(o_ref.dtype)


def _stick_breaking(pn_b, pt, *, batch, seq, k_col, qT_row, vT_row):
    T = batch * seq
    blk = min(SB_BLOCK, seq)
    nq = seq // blk
    gd = SB_GROUP * SB_HEAD_DIM
    return pl.pallas_call(
        functools.partial(_sb_kernel, blk=blk, heads=SB_GROUP),
        grid=(batch, SB_HEADS // SB_GROUP, nq),
        in_specs=[
            pl.BlockSpec((gd, blk), lambda b, h, i: (qT_row + h, b * nq + i)),
            pl.BlockSpec((seq, gd), lambda b, h, i: (b, k_col + h)),
            pl.BlockSpec((gd, seq), lambda b, h, i: (vT_row + h, b)),
        ],
        out_specs=pl.BlockSpec((gd, blk), lambda b, h, i: (h, b * nq + i)),
        out_shape=jax.ShapeDtypeStruct((SB_HEADS * SB_HEAD_DIM, T), BF16),
        scratch_shapes=[pltpu.VMEM((gd, blk), F32)],
        compiler_params=_params("parallel", "parallel", "arbitrary"),
        name="stick_breaking",
    )(pt, pn_b, pt)


def _merge_kernel(x_ref, h_ref, hp_ref, win_ref, yml_ref, ysbT_ref, cw_ref,
                  wml_ref, wsc_ref, wsb_ref, wout_ref, o_ref, *, tiles_per_seq):
    tm, D = x_ref.shape
    P = hp_ref.shape[0]
    first = (pl.program_id(0) % tiles_per_seq) == 0
    h = h_ref[...]
    h_ext = jnp.concatenate([hp_ref[...], h], axis=0)
    col = lambda n: slice(n * D, (n + 1) * D)

    z = _dot(h_ext, win_ref[:, col(1)]) * _dot(h_ext, win_ref[:, col(2)])
    row = lax.broadcasted_iota(jnp.int32, z.shape, 0)
    z = jnp.where(jnp.logical_and(first, row < P), 0.0, z)
    cw = cw_ref[...]
    conv = (cw[0:1, :] * z[P - 2:P - 2 + tm, :] + cw[1:2, :] * z[P - 1:P - 1 + tm, :]
            + cw[2:3, :] * z[P:P + tm, :])
    y_sc = _dot(h, win_ref[:, col(0)]) * conv

    merged = jax.nn.sigmoid(_dot(h, win_ref[:, col(3)])) * _dot(yml_ref[...], wml_ref[...])
    merged += jax.nn.sigmoid(_dot(h, win_ref[:, col(4)])) * _dot(y_sc.astype(BF16), wsc_ref[...])
    merged += jax.nn.sigmoid(_dot(h, win_ref[:, col(5)])) * _dot_tn(ysbT_ref[...], wsb_ref[...])
    o_ref[...] = x_ref[...] + _dot(merged.astype(BF16), wout_ref[...])


def _merge(x, h, w_in6, y_ml, y_sbT, conv_w, w_ml, w_sc, w_sb, w_out, *, seq):
    T, D = x.shape
    tm = min(TM_MERGE, seq)
    const = lambda i: (0, 0)
    resident = lambda shape: pl.BlockSpec(shape, const, pipeline_mode=pl.Buffered(1))
    return pl.pallas_call(
        functools.partial(_merge_kernel, tiles_per_seq=seq // tm),
        grid=(T // tm,),
        in_specs=[
            pl.BlockSpec((tm, D), lambda i: (i, 0)),
            pl.BlockSpec((tm, D), lambda i: (i, 0)),
            pl.BlockSpec((PREV_ROWS, D), lambda i: (jnp.maximum(i * (tm // PREV_ROWS) - 1, 0), 0)),
            resident((D, 6 * D)),
            pl.BlockSpec((tm, D), lambda i: (i, 0)),
            pl.BlockSpec((D, tm), lambda i: (0, i)),
            resident((3, D)),
            resident((D, D)), resident((D, D)), resident((D, D)), resident((D, D)),
        ],
        out_specs=pl.BlockSpec((tm, D), lambda i: (i, 0)),
        out_shape=jax.ShapeDtypeStruct((T, D), F32),
        compiler_params=_params("parallel"),
        name="merge",
    )(x, h, h, w_in6, y_ml, y_sbT, conv_w, w_ml, w_sc, w_sb, w_out)


def _mlp_kernel(x_ref, g_ref, wup_ref, wdown_ref, gn_ref, wg_hi_ref, wg_lo_ref, bg_ref, *out_refs,
                ff_chunk, last):
    x = x_ref[...]
    h = _rmsnorm_rows(x, g_ref[...]).astype(BF16)
    acc = x
    for c in range(wup_ref.shape[1] // ff_chunk):
        cols = slice(c * ff_chunk, (c + 1) * ff_chunk)
        up = jnp.maximum(_dot(h, wup_ref[:, cols]), 0.0)
        acc = acc + _dot((up * up).astype(BF16), wdown_ref[cols, :])
    if last:
        out_refs[0][...] = _rmsnorm_rows(acc, gn_ref[...])
    else:
        out_refs[0][...] = acc
        out_refs[1][...], out_refs[2][...] = _prenorm_outputs(acc, gn_ref[...], wg_hi_ref[...],
                                                              wg_lo_ref[...], bg_ref[...])


def _mlp(x, g, w_up, w_down, g_next, wg_hi, wg_lo, bg, *, last):
    T, D = x.shape
    F = w_up.shape[1]
    G = wg_hi.shape[0]
    tm = min(TM_MLP, T)
    const = lambda i: (0, 0)
    resident = lambda shape: pl.BlockSpec(shape, const, pipeline_mode=pl.Buffered(1))
    out_specs = [pl.BlockSpec((tm, D), lambda i: (i, 0))]
    out_shape = [jax.ShapeDtypeStruct((T, D), F32)]
    if not last:
        out_specs += [pl.BlockSpec((tm, D), lambda i: (i, 0)), pl.BlockSpec((G, tm), lambda i: (0, i))]
        out_shape += [jax.ShapeDtypeStruct((T, D), BF16), jax.ShapeDtypeStruct((G, T), F32)]
    return pl.pallas_call(
        functools.partial(_mlp_kernel, ff_chunk=1024, last=last),
        grid=(T // tm,),
        in_specs=[
            pl.BlockSpec((tm, D), lambda i: (i, 0)),
            resident((1, D)),
            resident((D, F)),
            resident((F, D)),
            resident((1, D)),
            resident((G, D)), resident((G, D)), resident((G, 1)),
        ],
        out_specs=out_specs,
        out_shape=out_shape,
        compiler_params=_params("parallel"),
        name="mlp",
    )(x, g, w_up, w_down, g_next, wg_hi, wg_lo, bg)


def _gate_params(w_in, l, b_if):
    off = 4 * ML_HEADS * ML_HEAD_DIM
    wg_hi, wg_lo = _split_bf16(w_in[l, :, off:off + 2 * ML_HEADS].T)
    return wg_hi, wg_lo, b_if.reshape(2 * ML_HEADS, 1)


def _layer(x, h, gates, p, nxt, *, batch, seq):
    D = D_MODEL
    ml_w = ML_HEADS * ML_HEAD_DIM
    w_in, l = p["w_in"]
    off = 0

    def take(n):
        nonlocal off
        w = w_in[l, :, off:off + n]
        off += n
        return w

    w_mq, w_mk, w_mv, w_mo = take(ml_w), take(ml_w), take(ml_w), take(ml_w)
    take(2 * ML_HEADS)
    w_b, w_c, w_u = take(D), take(D), take(D)
    w_sq, w_sk, w_sv = take(D), take(D), take(D)
    w_gml, w_gsc, w_gsb = take(D), take(D), take(D)

    wn_b = jnp.concatenate([w_mq, w_mv, w_sk], axis=1).astype(BF16)
    wn_f = w_mo.astype(BF16)
    w_in6 = jnp.concatenate([w_b, w_c, w_u, w_gml, w_gsc, w_gsb], axis=1).astype(BF16)
    wt = jnp.concatenate([w_mk, w_sq, w_sv], axis=1).astype(BF16)
    scales = jnp.array([ML_HEAD_DIM ** -0.5, SB_HEAD_DIM ** -0.5 * LOG2E, 1.0], F32)

    pn_b = _proj(h, wn_b, BF16)
    pn_f = _proj(h, wn_f, F32)
    pt = _proj_t(h, wt, scales)

    y_ml = _mlstm(pn_b, pn_f, pt, gates, p["ml_norm_g"].reshape(1, ml_w), batch=batch, seq=seq,
                  q_col=0, v_col=1, o_col=0, kT_row=0)
    y_sbT = _stick_breaking(pn_b, pt, batch=batch, seq=seq,
                            k_col=(2 * ml_w) // (SB_GROUP * SB_HEAD_DIM),
                            qT_row=ml_w // (SB_GROUP * SB_HEAD_DIM),
                            vT_row=(ml_w + D) // (SB_GROUP * SB_HEAD_DIM))
    x = _merge(x, h, w_in6, y_ml, y_sbT, p["conv_w"],
               p["w_ml_proj"].astype(BF16), p["w_sc_proj"].astype(BF16),
               p["w_sb_proj"].astype(BF16), p["w_out"].astype(BF16), seq=seq)
    return _mlp(x, p["norm_mlp_g"].reshape(1, D), p["w_up"].astype(BF16), p["w_down"].astype(BF16),
                nxt["g"].reshape(1, D), nxt["wg_hi"], nxt["wg_lo"], nxt["bg"], last=nxt["last"])


def kernel(x, norm_mix_g, w_in, b_if, ml_norm_g, conv_w, w_ml_proj, w_sc_proj, w_sb_proj, w_out,
           norm_mlp_g, w_up, w_down, norm_final_g):
    batch, seq, D = x.shape
    depth = w_in.shape[0]
    xt = x.reshape(batch * seq, D)
    gate_params = [_gate_params(w_in, l, b_if[l]) for l in range(depth)]
    h, gates = _prenorm(xt, norm_mix_g[0].reshape(1, D), *gate_params[0])
    for l in range(depth):
        p = dict(w_in=(w_in, l), ml_norm_g=ml_norm_g[l], conv_w=conv_w[l], w_ml_proj=w_ml_proj[l],
                 w_sc_proj=w_sc_proj[l], w_sb_proj=w_sb_proj[l], w_out=w_out[l],
                 norm_mlp_g=norm_mlp_g[l], w_up=w_up[l], w_down=w_down[l])
        last = l == depth - 1
        nl = l if last else l + 1
        nxt = dict(g=norm_final_g if last else norm_mix_g[nl], wg_hi=gate_params[nl][0],
                   wg_lo=gate_params[nl][1], bg=gate_params[nl][2], last=last)
        outs = _layer(xt, h, gates, p, nxt, batch=batch, seq=seq)
        if last:
            xt = outs[0]
        else:
            xt, h, gates = outs
    return xt.reshape(batch, seq, D)
```

```python
import functools

import jax
import jax.numpy as jnp
from jax import lax
from jax.experimental import pallas as pl
from jax.experimental.pallas import tpu as pltpu

D_MODEL = 1024
ML_HEADS = 4
ML_HEAD_DIM = 256
SB_HEADS = 16
SB_HEAD_DIM = 64
D_FF = 4 * D_MODEL
EPS = 1e-6

VMEM_LIMIT_BYTES = 56 * 1024 * 1024

ML_CHUNK = 256
SB_BLOCK = 256
SB_GROUP = 4
LOG2E = 1.4426950408889634
SB_DEAD_LOG2 = -152.0
SB_ABSENT_LOG2 = -1e30
TM_PROJ = 2048
TM_MERGE = 512
TM_MLP = 512
PREV_ROWS = 16

BF16 = jnp.bfloat16
F32 = jnp.float32


def _params(*sem):
    return pltpu.CompilerParams(dimension_semantics=sem, vmem_limit_bytes=VMEM_LIMIT_BYTES)


def _dot(a, b):
    return jnp.dot(a, b, preferred_element_type=F32)


def _dot_nt(a, b):
    return lax.dot_general(a, b, (((1,), (1,)), ((), ())), preferred_element_type=F32)


def _dot_tn(a, b):
    return lax.dot_general(a, b, (((0,), (0,)), ((), ())), preferred_element_type=F32)


def _rmsnorm_rows(x, g):
    ms = jnp.mean(x * x, axis=-1, keepdims=True)
    return x * lax.rsqrt(ms + EPS) * g


def _log_sigmoid(x):
    return jnp.minimum(x, 0.0) - jnp.log(1.0 + jnp.exp(-jnp.abs(x)))


def _split_bf16(x):
    hi = x.astype(BF16)
    lo = (x - hi.astype(F32)).astype(BF16)
    return hi, lo


def _prenorm_outputs(x, g, wg_hi, wg_lo, bg):
    h_hi, h_lo = _split_bf16(_rmsnorm_rows(x, g))
    gates = _dot_nt(wg_hi, h_hi) + (_dot_nt(wg_hi, h_lo) + _dot_nt(wg_lo, h_hi)) + bg
    return h_hi, gates


def _prenorm_kernel(x_ref, g_ref, wg_hi_ref, wg_lo_ref, bg_ref, h_ref, gates_ref):
    h_ref[...], gates_ref[...] = _prenorm_outputs(x_ref[...], g_ref[...], wg_hi_ref[...],
                                                  wg_lo_ref[...], bg_ref[...])


def _prenorm(x, g, wg_hi, wg_lo, bg):
    T, D = x.shape
    G = wg_hi.shape[0]
    tm = min(TM_PROJ, T)
    const = lambda i: (0, 0)
    return pl.pallas_call(
        _prenorm_kernel,
        grid=(T // tm,),
        in_specs=[
            pl.BlockSpec((tm, D), lambda i: (i, 0)),
            pl.BlockSpec((1, D), const),
            pl.BlockSpec((G, D), const),
            pl.BlockSpec((G, D), const),
            pl.BlockSpec((G, 1), const),
        ],
        out_specs=[pl.BlockSpec((tm, D), lambda i: (i, 0)), pl.BlockSpec((G, tm), lambda i: (0, i))],
        out_shape=[jax.ShapeDtypeStruct((T, D), BF16), jax.ShapeDtypeStruct((G, T), F32)],
        compiler_params=_params("parallel"),
        name="prenorm",
    )(x, g, wg_hi, wg_lo, bg)


def _proj_kernel(scale_ref, h_ref, w_ref, o_ref):
    scale = scale_ref[pl.program_id(1)]
    o_ref[...] = (_dot(h_ref[...], w_ref[...]) * scale).astype(o_ref.dtype)


def _proj(h, w, scales, out_dtype, tn=1024):
    T, D = h.shape
    N = w.shape[1]
    tm = min(TM_PROJ, T)
    return pl.pallas_call(
        _proj_kernel,
        grid=(T // tm, N // tn),
        in_specs=[
            pl.BlockSpec(memory_space=pltpu.SMEM),
            pl.BlockSpec((tm, D), lambda i, j: (i, 0)),
            pl.BlockSpec((D, tn), lambda i, j: (0, j)),
        ],
        out_specs=pl.BlockSpec((tm, tn), lambda i, j: (i, j)),
        out_shape=jax.ShapeDtypeStruct((T, N), out_dtype),
        compiler_params=_params("parallel", "parallel"),
        name="proj",
    )(scales, h, w)


def _proj_t_kernel(scale_ref, h_ref, w_ref, o_ref):
    scale = scale_ref[pl.program_id(1)]
    out = lax.dot_general(w_ref[...], h_ref[...], (((0,), (1,)), ((), ())), preferred_element_type=F32)
    o_ref[...] = (out * scale).astype(o_ref.dtype)


def _proj_t(h, w, scales, out_dtype, tn=1024):
    T, D = h.shape
    N = w.shape[1]
    tm = min(TM_PROJ, T)
    return pl.pallas_call(
        _proj_t_kernel,
        grid=(T // tm, N // tn),
        in_specs=[
            pl.BlockSpec(memory_space=pltpu.SMEM),
            pl.BlockSpec((tm, D), lambda i, j: (i, 0)),
            pl.BlockSpec((D, tn), lambda i, j: (0, j)),
        ],
        out_specs=pl.BlockSpec((tn, tm), lambda i, j: (j, i)),
        out_shape=jax.ShapeDtypeStruct((N, T), out_dtype),
        compiler_params=_params("parallel", "parallel"),
        name="proj_t",
    )(scales, h, w)


def _row_to_columns(r):
    L = r.shape[1]
    col = jnp.transpose(jnp.broadcast_to(r, (128, L)))
    return jnp.concatenate([col] * (L // 128), axis=1)


def _mlstm_kernel(k_ref, qT_ref, vT_ref, oT_ref, gates_ref, gain_ref, y_ref, c_ref, n_ref, m_ref):
    @pl.when(pl.program_id(1) == 0)
    def _():
        c_ref[...] = jnp.zeros_like(c_ref)
        n_ref[...] = jnp.zeros_like(n_ref)
        m_ref[...] = jnp.zeros_like(m_ref)

    L = k_ref.shape[0]
    dh = ML_HEAD_DIM
    heads = range(ML_HEADS)
    feat = lambda g: slice(g * dh, (g + 1) * dh)
    s_idx = lax.broadcasted_iota(jnp.int32, (L, L), 0)
    t_idx = lax.broadcasted_iota(jnp.int32, (L, L), 1)
    causal = s_idx <= t_idx

    k = [k_ref[:, feat(g)] for g in heads]
    qT = [qT_ref[feat(g), :] for g in heads]
    c_prev = [c_ref[g] for g in heads]
    n_prev = [n_ref[g] for g in heads]
    m_prev = [m_ref[g] for g in heads]
    kq = [_dot(k[g], qT[g]) for g in heads]
    cq = [_dot(c_prev[g].astype(BF16), qT[g]) for g in heads]
    nq = [_dot(n_prev[g].astype(BF16), qT[g])[0:1, :] for g in heads]


    gates = gates_ref[...]
    log_f = _log_sigmoid(gates)
    b_row, u_row, a_inter, s_w, den, num = [], [], [], [], [], []
    for g in heads:
        lf_col = _row_to_columns(log_f[ML_HEADS + g:ML_HEADS + g + 1, :])
        b_row.append(jnp.sum(jnp.where(causal, lf_col, 0.0), axis=0, keepdims=True))
        c_col = _row_to_columns(gates[g:g + 1, :] - b_row[g])
        c_max = jnp.max(jnp.where(causal, c_col, -jnp.inf), axis=0, keepdims=True)
        u_row.append(jnp.maximum(m_prev[g], c_max))
        w = jnp.where(causal, jnp.exp(c_col - u_row[g]), 0.0)
        s_w.append(kq[g] * w)
        a_inter.append(jnp.exp(m_prev[g] - u_row[g]))
        den.append(jnp.sum(s_w[g], axis=0, keepdims=True) + a_inter[g] * nq[g])
        num.append(_dot(vT_ref[feat(g), :], s_w[g].astype(BF16)))

    for g in heads:
        b_last = b_row[g][:, L - 1:L]
        g_row = b_last - b_row[g] + gates[g:g + 1, :]
        m_new = jnp.maximum(b_last + m_prev[g], jnp.max(g_row, axis=1, keepdims=True))
        decay = jnp.exp(b_last + m_prev[g] - m_new)
        w_state = jnp.exp(g_row - m_new)
        vw = (vT_ref[feat(g), :].astype(F32) * w_state).astype(BF16)
        c_ref[g] = decay * c_prev[g] + _dot(vw, k[g])
        w8 = jnp.broadcast_to(w_state, (8, L)).astype(BF16)
        n_ref[g] = decay * n_prev[g] + _dot(w8, k[g])
        m_ref[g] = m_new

    for g in heads:
        floor = jnp.exp(-(b_row[g] + u_row[g]))
        h = (num[g] + a_inter[g] * cq[g]) / jnp.maximum(jnp.abs(den[g]), floor)
        hg = jax.nn.sigmoid(oT_ref[feat(g), :]) * h
        ms = jnp.mean(hg * hg, axis=0, keepdims=True)
        gain = jnp.concatenate([gain_ref[feat(g), :]] * (L // 128), axis=1)
        y_ref[feat(g), :] = (hg * lax.rsqrt(ms + EPS) * gain).astype(y_ref.dtype)


def _mlstm(pn, pt, pt_f, gates, gain, *, batch, seq, k_col, qT_row, vT_row, oT_row):
    T = batch * seq
    L = min(ML_CHUNK, seq)
    nc = seq // L
    dh = ML_HEAD_DIM
    W = ML_HEADS * dh
    tok = lambda b, c: b * nc + c
    return pl.pallas_call(
        _mlstm_kernel,
        grid=(batch, nc),
        in_specs=[
            pl.BlockSpec((L, W), lambda b, c: (tok(b, c), k_col)),
            pl.BlockSpec((W, L), lambda b, c: (qT_row, tok(b, c))),
            pl.BlockSpec((W, L), lambda b, c: (vT_row, tok(b, c))),
            pl.BlockSpec((W, L), lambda b, c: (oT_row, tok(b, c))),
            pl.BlockSpec((2 * ML_HEADS, L), lambda b, c: (0, tok(b, c))),
            pl.BlockSpec((W, 128), lambda b, c: (0, 0)),
        ],
        out_specs=pl.BlockSpec((W, L), lambda b, c: (0, tok(b, c))),
        out_shape=jax.ShapeDtypeStruct((W, T), BF16),
        scratch_shapes=[
            pltpu.VMEM((ML_HEADS, dh, dh), F32),
            pltpu.VMEM((ML_HEADS, 8, dh), F32),
            pltpu.VMEM((ML_HEADS, 1, 1), F32),
        ],
        compiler_params=_params("parallel", "arbitrary"),
        name="mlstm",
    )(pn, pt, pt, pt_f, gates, gain)


def _sb_kernel(qT_ref, k_ref, vT_ref, o_ref, acc_ref, *, blk, heads):
    i = pl.program_id(2)
    d = SB_HEAD_DIM

    s_idx = lax.broadcasted_iota(jnp.int32, (blk, blk), 0)
    t_idx = lax.broadcasted_iota(jnp.int32, (blk, blk), 1)
    later = (t_idx > s_idx)
    from_here = jnp.where(t_idx >= s_idx, 1.0, 0.0).astype(BF16)

    row_head = lax.broadcasted_iota(jnp.int32, (2 * d, blk), 0) // d
    q_pads = []
    for g in range(heads):
        q2 = qT_ref[(g // 2) * 2 * d:(g // 2 + 1) * 2 * d, :]
        q_pads.append(jnp.where(row_head == g % 2, q2, jnp.zeros_like(q2)))

    def neg_log2_1m(zs):
        neg_abs = pltpu.bitcast(pltpu.bitcast(zs, jnp.uint32) | jnp.uint32(0x80000000), F32)
        return jnp.maximum(zs, 0.0) + jnp.log(1.0 + jnp.exp2(neg_abs)) * LOG2E

    def visit(blocks, rs, init_acc):
        starts = [pl.multiple_of(j * blk, blk) for j, _, _ in blocks]
        z = [[_dot(k_ref[pl.ds(st, blk), (g // 2) * 2 * d:(g // 2 + 1) * 2 * d], q_pads[g])
              for g in range(heads)] for st in starts]
        tail = []
        for b, (_, diagonal, _) in enumerate(blocks):
            tail.append([])
            for g in range(heads):
                nl1 = neg_log2_1m(z[b][g])
                if diagonal:
                    nl1 = jnp.where(later, nl1, 0.0)
                tail[b].append(_dot(from_here, nl1.astype(BF16)))
        rs = list(rs)
        r_in = []
        for b, (_, _, r_gate) in enumerate(blocks):
            r_in.append([rs[g] if r_gate is None else jnp.where(r_gate, rs[g], SB_ABSENT_LOG2)
                         for g in range(heads)])
            rs = [r_in[b][g] - tail[b][g][0:1, :] for g in range(heads)]
        live = jnp.max(functools.reduce(jnp.maximum, rs)) > SB_DEAD_LOG2
        for b, (_, diagonal, _) in enumerate(blocks):
            for g in range(heads):
                a = jnp.exp2(z[b][g] - tail[b][g] + r_in[b][g])
                if diagonal:
                    a = jnp.where(later, a, 0.0)
                contrib = _dot(vT_ref[g * d:(g + 1) * d, pl.ds(starts[b], blk)], a.astype(BF16))
                if init_acc and b == 0:
                    acc_ref[g * d:(g + 1) * d, :] = contrib
                else:
                    acc_ref[g * d:(g + 1) * d, :] += contrib
        return live, tuple(rs)

    def cond(c):
        n, live, _ = c
        return jnp.logical_and(n < i - 1, live)

    def body(c):
        n, _, rs = c
        return (n + 1,) + visit([(i - 2 - n, False, None)], rs, False)

    first = visit([(i, True, None), (jnp.maximum(i - 1, 0), False, i > 0)],
                  tuple(jnp.zeros((1, blk), F32) for _ in range(heads)), True)
    lax.while_loop(cond, body, (jnp.int32(0),) + first)
    o_ref[...] = acc_ref[...].astype(o_ref.dtype)


def _stick_breaking(pn_b, pt, *, batch, seq, k_col, qT_row, vT_row):
    T = batch * seq
    blk = min(SB_BLOCK, seq)
    nq = seq // blk
    gd = SB_GROUP * SB_HEAD_DIM
    return pl.pallas_call(
        functools.partial(_sb_kernel, blk=blk, heads=SB_GROUP),
        grid=(batch, SB_HEADS // SB_GROUP, nq),
        in_specs=[
            pl.BlockSpec((gd, blk), lambda b, h, i: (qT_row + h, b * nq + i)),
            pl.BlockSpec((seq, gd), lambda b, h, i: (b, k_col + h)),
            pl.BlockSpec((gd, seq), lambda b, h, i: (vT_row + h, b)),
        ],
        out_specs=pl.BlockSpec((gd, blk), lambda b, h, i: (h, b * nq + i)),
        out_shape=jax.ShapeDtypeStruct((SB_HEADS * SB_HEAD_DIM, T), BF16),
        scratch_shapes=[pltpu.VMEM((gd, blk), F32)],
        compiler_params=_params("parallel", "parallel", "arbitrary"),
        name="stick_breaking",
    )(pt, pn_b, pt)


def _merge_kernel(x_ref, h_ref, hp_ref, win_ref, ymlT_ref, ysbT_ref, cw_ref,
                  wml_ref, wsc_ref, wsb_ref, wout_ref, o_ref, *, tiles_per_seq):
    tm, D = x_ref.shape
    P = hp_ref.shape[0]
    first = (pl.program_id(0) % tiles_per_seq) == 0
    h = h_ref[...]
    h_ext = jnp.concatenate([hp_ref[...], h], axis=0)
    col = lambda n: slice(n * D, (n + 1) * D)

    z = _dot(h_ext, win_ref[:, col(1)]) * _dot(h_ext, win_ref[:, col(2)])
    row = lax.broadcasted_iota(jnp.int32, z.shape, 0)
    z = jnp.where(jnp.logical_and(first, row < P), 0.0, z)
    cw = cw_ref[...]
    conv = (cw[0:1, :] * z[P - 2:P - 2 + tm, :] + cw[1:2, :] * z[P - 1:P - 1 + tm, :]
            + cw[2:3, :] * z[P:P + tm, :])
    y_sc = _dot(h, win_ref[:, col(0)]) * conv

    merged = jax.nn.sigmoid(_dot(h, win_ref[:, col(3)])) * _dot_tn(ymlT_ref[...], wml_ref[...])
    merged += jax.nn.sigmoid(_dot(h, win_ref[:, col(4)])) * _dot(y_sc.astype(BF16), wsc_ref[...])
    merged += jax.nn.sigmoid(_dot(h, win_ref[:, col(5)])) * _dot_tn(ysbT_ref[...], wsb_ref[...])
    o_ref[...] = x_ref[...] + _dot(merged.astype(BF16), wout_ref[...])


def _merge(x, h, w_in6, y_mlT, y_sbT, conv_w, w_ml, w_sc, w_sb, w_out, *, seq):
    T, D = x.shape
    tm = min(TM_MERGE, seq)
    const = lambda i: (0, 0)
    resident = lambda shape: pl.BlockSpec(shape, const, pipeline_mode=pl.Buffered(1))
    return pl.pallas_call(
        functools.partial(_merge_kernel, tiles_per_seq=seq // tm),
        grid=(T // tm,),
        in_specs=[
            pl.BlockSpec((tm, D), lambda i: (i, 0)),
            pl.BlockSpec((tm, D), lambda i: (i, 0)),
            pl.BlockSpec((PREV_ROWS, D), lambda i: (jnp.maximum(i * (tm // PREV_ROWS) - 1, 0), 0)),
            resident((D, 6 * D)),
            pl.BlockSpec((D, tm), lambda i: (0, i)),
            pl.BlockSpec((D, tm), lambda i: (0, i)),
            resident((3, D)),
            resident((D, D)), resident((D, D)), resident((D, D)), resident((D, D)),
        ],
        out_specs=pl.BlockSpec((tm, D), lambda i: (i, 0)),
        out_shape=jax.ShapeDtypeStruct((T, D), F32),
        compiler_params=_params("parallel"),
        name="merge",
    )(x, h, h, w_in6, y_mlT, y_sbT, conv_w, w_ml, w_sc, w_sb, w_out)


def _mlp_kernel(x_ref, g_ref, wup_ref, wdown_ref, gn_ref, wg_hi_ref, wg_lo_ref, bg_ref, *out_refs,
                ff_chunk, last):
    x = x_ref[...]
    h = _rmsnorm_rows(x, g_ref[...]).astype(BF16)
    acc = x
    for c in range(wup_ref.shape[1] // ff_chunk):
        cols = slice(c * ff_chunk, (c + 1) * ff_chunk)
        up = jnp.maximum(_dot(h, wup_ref[:, cols]), 0.0)
        acc = acc + _dot((up * up).astype(BF16), wdown_ref[cols, :])
    if last:
        out_refs[0][...] = _rmsnorm_rows(acc, gn_ref[...])
    else:
        out_refs[0][...] = acc
        out_refs[1][...], out_refs[2][...] = _prenorm_outputs(acc, gn_ref[...], wg_hi_ref[...],
                                                              wg_lo_ref[...], bg_ref[...])


def _mlp(x, g, w_up, w_down, g_next, wg_hi, wg_lo, bg, *, last):
    T, D = x.shape
    F = w_up.shape[1]
    G = wg_hi.shape[0]
    tm = min(TM_MLP, T)
    const = lambda i: (0, 0)
    resident = lambda shape: pl.BlockSpec(shape, const, pipeline_mode=pl.Buffered(1))
    out_specs = [pl.BlockSpec((tm, D), lambda i: (i, 0))]
    out_shape = [jax.ShapeDtypeStruct((T, D), F32)]
    if not last:
        out_specs += [pl.BlockSpec((tm, D), lambda i: (i, 0)), pl.BlockSpec((G, tm), lambda i: (0, i))]
        out_shape += [jax.ShapeDtypeStruct((T, D), BF16), jax.ShapeDtypeStruct((G, T), F32)]
    return pl.pallas_call(
        functools.partial(_mlp_kernel, ff_chunk=1024, last=last),
        grid=(T // tm,),
        in_specs=[
            pl.BlockSpec((tm, D), lambda i: (i, 0)),
            resident((1, D)),
            resident((D, F)),
            resident((F, D)),
            resident((1, D)),
            resident((G, D)), resident((G, D)), resident((G, 1)),
        ],
        out_specs=out_specs,
        out_shape=out_shape,
        compiler_params=_params("parallel"),
        name="mlp",
    )(x, g, w_up, w_down, g_next, wg_hi, wg_lo, bg)


def _gate_params(w_in, l, b_if):
    off = 4 * ML_HEADS * ML_HEAD_DIM
    wg_hi, wg_lo = _split_bf16(w_in[l, :, off:off + 2 * ML_HEADS].T)
    return wg_hi, wg_lo, b_if.reshape(2 * ML_HEADS, 1)


def _layer(x, h, gates, p, nxt, *, batch, seq):
    D = D_MODEL
    ml_w = ML_HEADS * ML_HEAD_DIM
    w_in, l = p["w_in"]
    off = 0

    def take(n):
        nonlocal off
        w = w_in[l, :, off:off + n]
        off += n
        return w

    w_mq, w_mk, w_mv, w_mo = take(ml_w), take(ml_w), take(ml_w), take(ml_w)
    take(2 * ML_HEADS)
    w_b, w_c, w_u = take(D), take(D), take(D)
    w_sq, w_sk, w_sv = take(D), take(D), take(D)
    w_gml, w_gsc, w_gsb = take(D), take(D), take(D)

    wn = jnp.concatenate([w_mk, w_sk], axis=1).astype(BF16)
    wt = jnp.concatenate([w_mq, w_mv, w_sq, w_sv], axis=1).astype(BF16)
    w_in6 = jnp.concatenate([w_b, w_c, w_u, w_gml, w_gsc, w_gsb], axis=1).astype(BF16)
    one = jnp.ones((1,), F32)
    pn = _proj(h, wn, jnp.array([ML_HEAD_DIM ** -0.5, 1.0], F32), BF16)
    pt = _proj_t(h, wt, jnp.array([1.0, 1.0, SB_HEAD_DIM ** -0.5 * LOG2E, 1.0], F32), BF16)
    pt_f = _proj_t(h, w_mo.astype(BF16), one, F32)

    gain = jnp.broadcast_to(p["ml_norm_g"].reshape(ml_w, 1), (ml_w, 128))
    y_mlT = _mlstm(pn, pt, pt_f, gates, gain, batch=batch, seq=seq,
                   k_col=0, qT_row=0, vT_row=1, oT_row=0)
    sb_unit = SB_GROUP * SB_HEAD_DIM
    y_sbT = _stick_breaking(pn, pt, batch=batch, seq=seq, k_col=ml_w // sb_unit,
                            qT_row=2 * ml_w // sb_unit, vT_row=(2 * ml_w + D) // sb_unit)
    x = _merge(x, h, w_in6, y_mlT, y_sbT, p["conv_w"],
               p["w_ml_proj"].astype(BF16), p["w_sc_proj"].astype(BF16),
               p["w_sb_proj"].astype(BF16), p["w_out"].astype(BF16), seq=seq)
    return _mlp(x, p["norm_mlp_g"].reshape(1, D), p["w_up"].astype(BF16), p["w_down"].astype(BF16),
                nxt["g"].reshape(1, D), nxt["wg_hi"], nxt["wg_lo"], nxt["bg"], last=nxt["last"])


def kernel(x, norm_mix_g, w_in, b_if, ml_norm_g, conv_w, w_ml_proj, w_sc_proj, w_sb_proj, w_out,
           norm_mlp_g, w_up, w_down, norm_final_g):
    batch, seq, D = x.shape
    depth = w_in.shape[0]
    xt = x.reshape(batch * seq, D)
    gate_params = [_gate_params(w_in, l, b_if[l]) for l in range(depth)]
    h, gates = _prenorm(xt, norm_mix_g[0].reshape(1, D), *gate_params[0])
    for l in range(depth):
        p = dict(w_in=(w_in, l), ml_norm_g=ml_norm_g[l], conv_w=conv_w[l], w_ml_proj=w_ml_proj[l],
                 w_sc_proj=w_sc_proj[l], w_sb_proj=w_sb_proj[l], w_out=w_out[l],
                 norm_mlp_g=norm_mlp_g[l], w_up=w_up[l], w_down=w_down[l])
        last = l == depth - 1
        nl = l if last else l + 1
        nxt = dict(g=norm_final_g if last else norm_mix_g[nl], wg_hi=gate_params[nl][0],
                   wg_lo=gate_params[nl][1], bg=gate_params[nl][2], last=last)
        outs = _layer(xt, h, gates, p, nxt, batch=batch, seq=seq)
        if last:
            xt = outs[0]
        else:
            xt, h, gates = outs
    return xt.reshape(batch, seq, D)
```

```python
import functools

import jax
import jax.numpy as jnp
from jax import lax
from jax.experimental import pallas as pl
from jax.experimental.pallas import tpu as pltpu

D_MODEL = 1024
ML_HEADS = 4
ML_HEAD_DIM = 256
SB_HEADS = 16
SB_HEAD_DIM = 64
D_FF = 4 * D_MODEL
EPS = 1e-6

VMEM_LIMIT_BYTES = 56 * 1024 * 1024

ML_CHUNK = 256
SB_BLOCK = 256
SB_GROUP = 4
SB_QBLOCKS = 2
LOG2E = 1.4426950408889634
SB_DEAD_LOG2 = -152.0
SB_ABSENT_LOG2 = -1e30
TM_PROJ = 2048
TM_MERGE = 512
TM_MLP = 512
PREV_ROWS = 16

BF16 = jnp.bfloat16
F32 = jnp.float32


def _params(*sem):
    return pltpu.CompilerParams(dimension_semantics=sem, vmem_limit_bytes=VMEM_LIMIT_BYTES)


def _dot(a, b):
    return jnp.dot(a, b, preferred_element_type=F32)


def _dot_nt(a, b):
    return lax.dot_general(a, b, (((1,), (1,)), ((), ())), preferred_element_type=F32)


def _dot_tn(a, b):
    return lax.dot_general(a, b, (((0,), (0,)), ((), ())), preferred_element_type=F32)


def _rmsnorm_rows(x, g):
    ms = jnp.mean(x * x, axis=-1, keepdims=True)
    return x * lax.rsqrt(ms + EPS) * g


def _log_sigmoid(x):
    return jnp.minimum(x, 0.0) - jnp.log(1.0 + jnp.exp(-jnp.abs(x)))


def _split_bf16(x):
    hi = x.astype(BF16)
    lo = (x - hi.astype(F32)).astype(BF16)
    return hi, lo


def _prenorm_outputs(x, g, wg_hi, wg_lo, bg):
    h_hi, h_lo = _split_bf16(_rmsnorm_rows(x, g))
    gates = _dot_nt(wg_hi, h_hi) + (_dot_nt(wg_hi, h_lo) + _dot_nt(wg_lo, h_hi)) + bg
    return h_hi, gates


def _prenorm_kernel(x_ref, g_ref, wg_hi_ref, wg_lo_ref, bg_ref, h_ref, gates_ref):
    h_ref[...], gates_ref[...] = _prenorm_outputs(x_ref[...], g_ref[...], wg_hi_ref[...],
                                                  wg_lo_ref[...], bg_ref[...])


def _prenorm(x, g, wg_hi, wg_lo, bg):
    T, D = x.shape
    G = wg_hi.shape[0]
    tm = min(TM_PROJ, T)
    const = lambda i: (0, 0)
    return pl.pallas_call(
        _prenorm_kernel,
        grid=(T // tm,),
        in_specs=[
            pl.BlockSpec((tm, D), lambda i: (i, 0)),
            pl.BlockSpec((1, D), const),
            pl.BlockSpec((G, D), const),
            pl.BlockSpec((G, D), const),
            pl.BlockSpec((G, 1), const),
        ],
        out_specs=[pl.BlockSpec((tm, D), lambda i: (i, 0)), pl.BlockSpec((G, tm), lambda i: (0, i))],
        out_shape=[jax.ShapeDtypeStruct((T, D), BF16), jax.ShapeDtypeStruct((G, T), F32)],
        compiler_params=_params("parallel"),
        name="prenorm",
    )(x, g, wg_hi, wg_lo, bg)


def _proj_kernel(scale_ref, h_ref, w_ref, o_ref):
    scale = scale_ref[pl.program_id(1)]
    o_ref[...] = (_dot(h_ref[...], w_ref[...]) * scale).astype(o_ref.dtype)


def _proj(h, w, scales, out_dtype, tn=1024):
    T, D = h.shape
    N = w.shape[1]
    tm = min(TM_PROJ, T)
    return pl.pallas_call(
        _proj_kernel,
        grid=(T // tm, N // tn),
        in_specs=[
            pl.BlockSpec(memory_space=pltpu.SMEM),
            pl.BlockSpec((tm, D), lambda i, j: (i, 0)),
            pl.BlockSpec((D, tn), lambda i, j: (0, j)),
        ],
        out_specs=pl.BlockSpec((tm, tn), lambda i, j: (i, j)),
        out_shape=jax.ShapeDtypeStruct((T, N), out_dtype),
        compiler_params=_params("parallel", "parallel"),
        name="proj",
    )(scales, h, w)


def _proj_t_kernel(scale_ref, h_ref, w_ref, o_ref):
    scale = scale_ref[pl.program_id(1)]
    out = lax.dot_general(w_ref[...], h_ref[...], (((0,), (1,)), ((), ())), preferred_element_type=F32)
    o_ref[...] = (out * scale).astype(o_ref.dtype)


def _proj_t(h, w, scales, out_dtype, tn=1024):
    T, D = h.shape
    N = w.shape[1]
    tm = min(TM_PROJ, T)
    return pl.pallas_call(
        _proj_t_kernel,
        grid=(T // tm, N // tn),
        in_specs=[
            pl.BlockSpec(memory_space=pltpu.SMEM),
            pl.BlockSpec((tm, D), lambda i, j: (i, 0)),
            pl.BlockSpec((D, tn), lambda i, j: (0, j)),
        ],
        out_specs=pl.BlockSpec((tn, tm), lambda i, j: (j, i)),
        out_shape=jax.ShapeDtypeStruct((N, T), out_dtype),
        compiler_params=_params("parallel", "parallel"),
        name="proj_t",
    )(scales, h, w)


def _row_to_columns(r):
    L = r.shape[1]
    col = jnp.transpose(jnp.broadcast_to(r, (128, L)))
    return jnp.concatenate([col] * (L // 128), axis=1)


def _mlstm_kernel(k_ref, qT_ref, vT_ref, oT_ref, gates_ref, gain_ref, y_ref, c_ref, n_ref, m_ref):
    @pl.when(pl.program_id(1) == 0)
    def _():
        c_ref[...] = jnp.zeros_like(c_ref)
        n_ref[...] = jnp.zeros_like(n_ref)
        m_ref[...] = jnp.zeros_like(m_ref)

    L = k_ref.shape[0]
    dh = ML_HEAD_DIM
    heads = range(ML_HEADS)
    feat = lambda g: slice(g * dh, (g + 1) * dh)
    s_idx = lax.broadcasted_iota(jnp.int32, (L, L), 0)
    t_idx = lax.broadcasted_iota(jnp.int32, (L, L), 1)
    causal = s_idx <= t_idx

    k = [k_ref[:, feat(g)] for g in heads]
    qT = [qT_ref[feat(g), :] for g in heads]
    c_prev = [c_ref[g] for g in heads]
    n_prev = [n_ref[g] for g in heads]
    m_prev = [m_ref[g] for g in heads]
    kq = [_dot(k[g], qT[g]) for g in heads]
    cq = [_dot(c_prev[g].astype(BF16), qT[g]) for g in heads]
    nq = [_dot(n_prev[g].astype(BF16), qT[g])[0:1, :] for g in heads]


    gates = gates_ref[...]
    log_f = _log_sigmoid(gates)
    b_row, u_row, a_inter, s_w, den, num = [], [], [], [], [], []
    for g in heads:
        lf_col = _row_to_columns(log_f[ML_HEADS + g:ML_HEADS + g + 1, :])
        b_row.append(jnp.sum(jnp.where(causal, lf_col, 0.0), axis=0, keepdims=True))
        c_col = _row_to_columns(gates[g:g + 1, :] - b_row[g])
        c_max = jnp.max(jnp.where(causal, c_col, -jnp.inf), axis=0, keepdims=True)
        u_row.append(jnp.maximum(m_prev[g], c_max))
        w = jnp.where(causal, jnp.exp(c_col - u_row[g]), 0.0)
        s_w.append(kq[g] * w)
        a_inter.append(jnp.exp(m_prev[g] - u_row[g]))
        den.append(jnp.sum(s_w[g], axis=0, keepdims=True) + a_inter[g] * nq[g])
        num.append(_dot(vT_ref[feat(g), :], s_w[g].astype(BF16)))

    for g in heads:
        b_last = b_row[g][:, L - 1:L]
        g_row = b_last - b_row[g] + gates[g:g + 1, :]
        m_new = jnp.maximum(b_last + m_prev[g], jnp.max(g_row, axis=1, keepdims=True))
        decay = jnp.exp(b_last + m_prev[g] - m_new)
        w_state = jnp.exp(g_row - m_new)
        vw = (vT_ref[feat(g), :].astype(F32) * w_state).astype(BF16)
        c_ref[g] = decay * c_prev[g] + _dot(vw, k[g])
        w8 = jnp.broadcast_to(w_state, (8, L)).astype(BF16)
        n_ref[g] = decay * n_prev[g] + _dot(w8, k[g])
        m_ref[g] = m_new

    for g in heads:
        floor = jnp.exp(-(b_row[g] + u_row[g]))
        h = (num[g] + a_inter[g] * cq[g]) / jnp.maximum(jnp.abs(den[g]), floor)
        hg = jax.nn.sigmoid(oT_ref[feat(g), :]) * h
        ms = jnp.mean(hg * hg, axis=0, keepdims=True)
        gain = jnp.concatenate([gain_ref[feat(g), :]] * (L // 128), axis=1)
        y_ref[feat(g), :] = (hg * lax.rsqrt(ms + EPS) * gain).astype(y_ref.dtype)


def _mlstm(pn, pt, pt_f, gates, gain, *, batch, seq, k_col, qT_row, vT_row, oT_row):
    T = batch * seq
    L = min(ML_CHUNK, seq)
    nc = seq // L
    dh = ML_HEAD_DIM
    W = ML_HEADS * dh
    tok = lambda b, c: b * nc + c
    return pl.pallas_call(
        _mlstm_kernel,
        grid=(batch, nc),
        in_specs=[
            pl.BlockSpec((L, W), lambda b, c: (tok(b, c), k_col)),
            pl.BlockSpec((W, L), lambda b, c: (qT_row, tok(b, c))),
            pl.BlockSpec((W, L), lambda b, c: (vT_row, tok(b, c))),
            pl.BlockSpec((W, L), lambda b, c: (oT_row, tok(b, c))),
            pl.BlockSpec((2 * ML_HEADS, L), lambda b, c: (0, tok(b, c))),
            pl.BlockSpec((W, 128), lambda b, c: (0, 0)),
        ],
        out_specs=pl.BlockSpec((W, L), lambda b, c: (0, tok(b, c))),
        out_shape=jax.ShapeDtypeStruct((W, T), BF16),
        scratch_shapes=[
            pltpu.VMEM((ML_HEADS, dh, dh), F32),
            pltpu.VMEM((ML_HEADS, 8, dh), F32),
            pltpu.VMEM((ML_HEADS, 1, 1), F32),
        ],
        compiler_params=_params("parallel", "arbitrary"),
        name="mlstm",
    )(pn, pt, pt, pt_f, gates, gain)


def _sb_kernel(qT_ref, k_ref, vT_ref, o_ref, acc_ref, *, blk, heads, qblocks):
    step = pl.program_id(2)
    d = SB_HEAD_DIM

    s_idx = lax.broadcasted_iota(jnp.int32, (blk, blk), 0)
    t_idx = lax.broadcasted_iota(jnp.int32, (blk, blk), 1)
    later = (t_idx > s_idx)
    from_here = jnp.where(t_idx >= s_idx, 1.0, 0.0).astype(BF16)

    row_head = lax.broadcasted_iota(jnp.int32, (2 * d, blk), 0) // d
    q_pads = []
    for qb in range(qblocks):
        q_pads.append([])
        for g in range(heads):
            q2 = qT_ref[(g // 2) * 2 * d:(g // 2 + 1) * 2 * d, qb * blk:(qb + 1) * blk]
            q_pads[qb].append(jnp.where(row_head == g % 2, q2, jnp.zeros_like(q2)))

    def neg_log2_1m(zs):
        neg_abs = pltpu.bitcast(pltpu.bitcast(zs, jnp.uint32) | jnp.uint32(0x80000000), F32)
        return jnp.maximum(zs, 0.0) + jnp.log(1.0 + jnp.exp2(neg_abs)) * LOG2E

    def visit(chains, rs, init_acc):
        starts = [pl.multiple_of(j * blk, blk) for _, j, _, _ in chains]
        z = [[_dot(k_ref[pl.ds(st, blk), (g // 2) * 2 * d:(g // 2 + 1) * 2 * d], q_pads[qb][g])
              for g in range(heads)] for st, (qb, _, _, _) in zip(starts, chains)]
        tail = []
        for c, (_, _, diagonal, _) in enumerate(chains):
            tail.append([])
            for g in range(heads):
                nl1 = neg_log2_1m(z[c][g])
                if diagonal:
                    nl1 = jnp.where(later, nl1, 0.0)
                tail[c].append(_dot(from_here, nl1.astype(BF16)))
        rs = {qb: list(r) for qb, r in rs.items()}
        r_in = []
        for c, (qb, _, _, r_gate) in enumerate(chains):
            r_in.append([rs[qb][g] if r_gate is None else jnp.where(r_gate, rs[qb][g], SB_ABSENT_LOG2)
                         for g in range(heads)])
            rs[qb] = [r_in[c][g] - tail[c][g][0:1, :] for g in range(heads)]
        out = {qb: (jnp.max(functools.reduce(jnp.maximum, r)) > SB_DEAD_LOG2, tuple(r))
               for qb, r in rs.items()}
        started = set()
        for c, (qb, _, diagonal, _) in enumerate(chains):
            for g in range(heads):
                a = jnp.exp2(z[c][g] - tail[c][g] + r_in[c][g])
                if diagonal:
                    a = jnp.where(later, a, 0.0)
                contrib = _dot(vT_ref[g * d:(g + 1) * d, pl.ds(starts[c], blk)], a.astype(BF16))
                if init_acc and qb not in started:
                    acc_ref[g * d:(g + 1) * d, qb * blk:(qb + 1) * blk] = contrib
                else:
                    acc_ref[g * d:(g + 1) * d, qb * blk:(qb + 1) * blk] += contrib
            started.add(qb)
        return out

    blocks = [step * qblocks + qb for qb in range(qblocks)]
    chains = []
    for qb, i in enumerate(blocks):
        chains += [(qb, i, True, None), (qb, jnp.maximum(i - 1, 0), False, i > 0)]
    zeros = tuple(jnp.zeros((1, blk), F32) for _ in range(heads))
    first = visit(chains, {qb: zeros for qb in range(qblocks)}, True)

    for qb, i in enumerate(blocks):
        def cond(c, i=i):
            n, live, _ = c
            return jnp.logical_and(n < i - 1, live)

        def body(c, qb=qb, i=i):
            n, _, rs = c
            return (n + 1,) + visit([(qb, i - 2 - n, False, None)], {qb: rs}, False)[qb]

        lax.while_loop(cond, body, (jnp.int32(0),) + first[qb])
    o_ref[...] = acc_ref[...].astype(o_ref.dtype)


def _stick_breaking(pn_b, pt, *, batch, seq, k_col, qT_row, vT_row):
    T = batch * seq
    blk = min(SB_BLOCK, seq)
    qblocks = min(SB_QBLOCKS, seq // blk)
    tq = qblocks * blk
    nq = seq // tq
    gd = SB_GROUP * SB_HEAD_DIM
    return pl.pallas_call(
        functools.partial(_sb_kernel, blk=blk, heads=SB_GROUP, qblocks=qblocks),
        grid=(batch, SB_HEADS // SB_GROUP, nq),
        in_specs=[
            pl.BlockSpec((gd, tq), lambda b, h, i: (qT_row + h, b * nq + i)),
            pl.BlockSpec((seq, gd), lambda b, h, i: (b, k_col + h)),
            pl.BlockSpec((gd, seq), lambda b, h, i: (vT_row + h, b)),
        ],
        out_specs=pl.BlockSpec((gd, tq), lambda b, h, i: (h, b * nq + i)),
        out_shape=jax.ShapeDtypeStruct((SB_HEADS * SB_HEAD_DIM, T), BF16),
        scratch_shapes=[pltpu.VMEM((gd, tq), F32)],
        compiler_params=_params("parallel", "parallel", "arbitrary"),
        name="stick_breaking",
    )(pt, pn_b, pt)


def _merge_kernel(x_ref, h_ref, hp_ref, win_ref, ymlT_ref, ysbT_ref, cw_ref,
                  wml_ref, wsc_ref, wsb_ref, wout_ref, o_ref, *, tiles_per_seq):
    tm, D = x_ref.shape
    P = hp_ref.shape[0]
    first = (pl.program_id(0) % tiles_per_seq) == 0
    h = h_ref[...]
    h_ext = jnp.concatenate([hp_ref[...], h], axis=0)
    col = lambda n: slice(n * D, (n + 1) * D)

    z = _dot(h_ext, win_ref[:, col(1)]) * _dot(h_ext, win_ref[:, col(2)])
    row = lax.broadcasted_iota(jnp.int32, z.shape, 0)
    z = jnp.where(jnp.logical_and(first, row < P), 0.0, z)
    cw = cw_ref[...]
    conv = (cw[0:1, :] * z[P - 2:P - 2 + tm, :] + cw[1:2, :] * z[P - 1:P - 1 + tm, :]
            + cw[2:3, :] * z[P:P + tm, :])
    y_sc = _dot(h, win_ref[:, col(0)]) * conv

    merged = jax.nn.sigmoid(_dot(h, win_ref[:, col(3)])) * _dot_tn(ymlT_ref[...], wml_ref[...])
    merged += jax.nn.sigmoid(_dot(h, win_ref[:, col(4)])) * _dot(y_sc.astype(BF16), wsc_ref[...])
    merged += jax.nn.sigmoid(_dot(h, win_ref[:, col(5)])) * _dot_tn(ysbT_ref[...], wsb_ref[...])
    o_ref[...] = x_ref[...] + _dot(merged.astype(BF16), wout_ref[...])


def _merge(x, h, w_in6, y_mlT, y_sbT, conv_w, w_ml, w_sc, w_sb, w_out, *, seq):
    T, D = x.shape
    tm = min(TM_MERGE, seq)
    const = lambda i: (0, 0)
    resident = lambda shape: pl.BlockSpec(shape, const, pipeline_mode=pl.Buffered(1))
    return pl.pallas_call(
        functools.partial(_merge_kernel, tiles_per_seq=seq // tm),
        grid=(T // tm,),
        in_specs=[
            pl.BlockSpec((tm, D), lambda i: (i, 0)),
            pl.BlockSpec((tm, D), lambda i: (i, 0)),
            pl.BlockSpec((PREV_ROWS, D), lambda i: (jnp.maximum(i * (tm // PREV_ROWS) - 1, 0), 0)),
            resident((D, 6 * D)),
            pl.BlockSpec((D, tm), lambda i: (0, i)),
            pl.BlockSpec((D, tm), lambda i: (0, i)),
            resident((3, D)),
            resident((D, D)), resident((D, D)), resident((D, D)), resident((D, D)),
        ],
        out_specs=pl.BlockSpec((tm, D), lambda i: (i, 0)),
        out_shape=jax.ShapeDtypeStruct((T, D), F32),
        compiler_params=_params("parallel"),
        name="merge",
    )(x, h, h, w_in6, y_mlT, y_sbT, conv_w, w_ml, w_sc, w_sb, w_out)


def _mlp_kernel(x_ref, g_ref, wup_ref, wdown_ref, gn_ref, wg_hi_ref, wg_lo_ref, bg_ref, *out_refs,
                ff_chunk, last):
    x = x_ref[...]
    h = _rmsnorm_rows(x, g_ref[...]).astype(BF16)
    acc = x
    for c in range(wup_ref.shape[1] // ff_chunk):
        cols = slice(c * ff_chunk, (c + 1) * ff_chunk)
        up = jnp.maximum(_dot(h, wup_ref[:, cols]), 0.0)
        acc = acc + _dot((up * up).astype(BF16), wdown_ref[cols, :])
    if last:
        out_refs[0][...] = _rmsnorm_rows(acc, gn_ref[...])
    else:
        out_refs[0][...] = acc
        out_refs[1][...], out_refs[2][...] = _prenorm_outputs(acc, gn_ref[...], wg_hi_ref[...],
                                                              wg_lo_ref[...], bg_ref[...])


def _mlp(x, g, w_up, w_down, g_next, wg_hi, wg_lo, bg, *, last):
    T, D = x.shape
    F = w_up.shape[1]
    G = wg_hi.shape[0]
    tm = min(TM_MLP, T)
    const = lambda i: (0, 0)
    resident = lambda shape: pl.BlockSpec(shape, const, pipeline_mode=pl.Buffered(1))
    out_specs = [pl.BlockSpec((tm, D), lambda i: (i, 0))]
    out_shape = [jax.ShapeDtypeStruct((T, D), F32)]
    if not last:
        out_specs += [pl.BlockSpec((tm, D), lambda i: (i, 0)), pl.BlockSpec((G, tm), lambda i: (0, i))]
        out_shape += [jax.ShapeDtypeStruct((T, D), BF16), jax.ShapeDtypeStruct((G, T), F32)]
    return pl.pallas_call(
        functools.partial(_mlp_kernel, ff_chunk=1024, last=last),
        grid=(T // tm,),
        in_specs=[
            pl.BlockSpec((tm, D), lambda i: (i, 0)),
            resident((1, D)),
            resident((D, F)),
            resident((F, D)),
            resident((1, D)),
            resident((G, D)), resident((G, D)), resident((G, 1)),
        ],
        out_specs=out_specs,
        out_shape=out_shape,
        compiler_params=_params("parallel"),
        name="mlp",
    )(x, g, w_up, w_down, g_next, wg_hi, wg_lo, bg)


def _gate_params(w_in, l, b_if):
    off = 4 * ML_HEADS * ML_HEAD_DIM
    wg_hi, wg_lo = _split_bf16(w_in[l, :, off:off + 2 * ML_HEADS].T)
    return wg_hi, wg_lo, b_if.reshape(2 * ML_HEADS, 1)


def _layer(x, h, gates, p, nxt, *, batch, seq):
    D = D_MODEL
    ml_w = ML_HEADS * ML_HEAD_DIM
    w_in, l = p["w_in"]
    off = 0

    def take(n):
        nonlocal off
        w = w_in[l, :, off:off + n]
        off += n
        return w

    w_mq, w_mk, w_mv, w_mo = take(ml_w), take(ml_w), take(ml_w), take(ml_w)
    take(2 * ML_HEADS)
    w_b, w_c, w_u = take(D), take(D), take(D)
    w_sq, w_sk, w_sv = take(D), take(D), take(D)
    w_gml, w_gsc, w_gsb = take(D), take(D), take(D)

    wn = jnp.concatenate([w_mk, w_sk], axis=1).astype(BF16)
    wt = jnp.concatenate([w_mq, w_mv, w_sq, w_sv], axis=1).astype(BF16)
    w_in6 = jnp.concatenate([w_b, w_c, w_u, w_gml, w_gsc, w_gsb], axis=1).astype(BF16)
    one = jnp.ones((1,), F32)
    pn = _proj(h, wn, jnp.array([ML_HEAD_DIM ** -0.5, 1.0], F32), BF16)
    pt = _proj_t(h, wt, jnp.array([1.0, 1.0, SB_HEAD_DIM ** -0.5 * LOG2E, 1.0], F32), BF16)
    pt_f = _proj_t(h, w_mo.astype(BF16), one, F32)

    gain = jnp.broadcast_to(p["ml_norm_g"].reshape(ml_w, 1), (ml_w, 128))
    y_mlT = _mlstm(pn, pt, pt_f, gates, gain, batch=batch, seq=seq,
                   k_col=0, qT_row=0, vT_row=1, oT_row=0)
    sb_unit = SB_GROUP * SB_HEAD_DIM
    y_sbT = _stick_breaking(pn, pt, batch=batch, seq=seq, k_col=ml_w // sb_unit,
                            qT_row=2 * ml_w // sb_unit, vT_row=(2 * ml_w + D) // sb_unit)
    x = _merge(x, h, w_in6, y_mlT, y_sbT, p["conv_w"],
               p["w_ml_proj"].astype(BF16), p["w_sc_proj"].astype(BF16),
               p["w_sb_proj"].astype(BF16), p["w_out"].astype(BF16), seq=seq)
    return _mlp(x, p["norm_mlp_g"].reshape(1, D), p["w_up"].astype(BF16), p["w_down"].astype(BF16),
                nxt["g"].reshape(1, D), nxt["wg_hi"], nxt["wg_lo"], nxt["bg"], last=nxt["last"])


def kernel(x, norm_mix_g, w_in, b_if, ml_norm_g, conv_w, w_ml_proj, w_sc_proj, w_sb_proj, w_out,
           norm_mlp_g, w_up, w_down, norm_final_g):
    batch, seq, D = x.shape
    depth = w_in.shape[0]
    xt = x.reshape(batch * seq, D)
    gate_params = [_gate_params(w_in, l, b_if[l]) for l in range(depth)]
    h, gates = _prenorm(xt, norm_mix_g[0].reshape(1, D), *gate_params[0])
    for l in range(depth):
        p = dict(w_in=(w_in, l), ml_norm_g=ml_norm_g[l], conv_w=conv_w[l], w_ml_proj=w_ml_proj[l],
                 w_sc_proj=w_sc_proj[l], w_sb_proj=w_sb_proj[l], w_out=w_out[l],
                 norm_mlp_g=norm_mlp_g[l], w_up=w_up[l], w_down=w_down[l])
        last = l == depth - 1
        nl = l if last else l + 1
        nxt = dict(g=norm_final_g if last else norm_mix_g[nl], wg_hi=gate_params[nl][0],
                   wg_lo=gate_params[nl][1], bg=gate_params[nl][2], last=last)
        outs = _layer(xt, h, gates, p, nxt, batch=batch, seq=seq)
        if last:
            xt = outs[0]
        else:
            xt, h, gates = outs
    return xt.reshape(batch, seq, D)
```

```python
import functools

import jax
import jax.numpy as jnp
from jax import lax
from jax.experimental import pallas as pl
from jax.experimental.pallas import tpu as pltpu

D_MODEL = 1024
ML_HEADS = 4
ML_HEAD_DIM = 256
SB_HEADS = 16
SB_HEAD_DIM = 64
D_FF = 4 * D_MODEL
EPS = 1e-6

VMEM_LIMIT_BYTES = 56 * 1024 * 1024

ML_CHUNK = 256
SB_BLOCK = 256
SB_GROUP = 4
SB_QBLOCKS = 2
LOG2E = 1.4426950408889634
SB_DEAD_LOG2 = -152.0
SB_ABSENT_LOG2 = -1e30
TM_PROJ = 2048
TM_MERGE = 512
TM_MLP = 512
PREV_ROWS = 16

BF16 = jnp.bfloat16
F32 = jnp.float32


def _params(*sem):
    return pltpu.CompilerParams(dimension_semantics=sem, vmem_limit_bytes=VMEM_LIMIT_BYTES)


def _dot(a, b):
    return jnp.dot(a, b, preferred_element_type=F32)


def _dot_nt(a, b):
    return lax.dot_general(a, b, (((1,), (1,)), ((), ())), preferred_element_type=F32)


def _dot_tn(a, b):
    return lax.dot_general(a, b, (((0,), (0,)), ((), ())), preferred_element_type=F32)


def _rmsnorm_rows(x, g):
    ms = jnp.mean(x * x, axis=-1, keepdims=True)
    return x * lax.rsqrt(ms + EPS) * g


def _log_sigmoid(x):
    return jnp.minimum(x, 0.0) - jnp.log(1.0 + jnp.exp(-jnp.abs(x)))


def _split_bf16(x):
    hi = x.astype(BF16)
    lo = (x - hi.astype(F32)).astype(BF16)
    return hi, lo


def _prenorm_outputs(x, g, wg_hi, wg_lo, bg):
    h_hi, h_lo = _split_bf16(_rmsnorm_rows(x, g))
    gates = _dot_nt(wg_hi, h_hi) + (_dot_nt(wg_hi, h_lo) + _dot_nt(wg_lo, h_hi)) + bg
    return h_hi, gates


def _prenorm_kernel(x_ref, g_ref, wg_hi_ref, wg_lo_ref, bg_ref, h_ref, gates_ref):
    h_ref[...], gates_ref[...] = _prenorm_outputs(x_ref[...], g_ref[...], wg_hi_ref[...],
                                                  wg_lo_ref[...], bg_ref[...])


def _prenorm(x, g, wg_hi, wg_lo, bg):
    T, D = x.shape
    G = wg_hi.shape[0]
    tm = min(TM_PROJ, T)
    const = lambda i: (0, 0)
    return pl.pallas_call(
        _prenorm_kernel,
        grid=(T // tm,),
        in_specs=[
            pl.BlockSpec((tm, D), lambda i: (i, 0)),
            pl.BlockSpec((1, D), const),
            pl.BlockSpec((G, D), const),
            pl.BlockSpec((G, D), const),
            pl.BlockSpec((G, 1), const),
        ],
        out_specs=[pl.BlockSpec((tm, D), lambda i: (i, 0)), pl.BlockSpec((G, tm), lambda i: (0, i))],
        out_shape=[jax.ShapeDtypeStruct((T, D), BF16), jax.ShapeDtypeStruct((G, T), F32)],
        compiler_params=_params("parallel"),
        name="prenorm",
    )(x, g, wg_hi, wg_lo, bg)


N_GROUPS = 13
ALIGNED_GROUPS = 4
GATE_COLS = 2 * ML_HEADS
LANES = 128
(G_MQ, G_MK, G_MV, G_MO, G_B, G_C, G_U, G_SQ, G_SK, G_SV, G_GML, G_GSC, G_GSB) = range(N_GROUPS)


def _regroup_kernel(a_ref, b_ref, o_ref):
    k = pl.program_id(1)

    @pl.when(k < ALIGNED_GROUPS)
    def _():
        o_ref[...] = a_ref[...].astype(o_ref.dtype)

    @pl.when(k >= ALIGNED_GROUPS)
    def _():
        D = a_ref.shape[1]
        window = jnp.concatenate([a_ref[...], b_ref[...]], axis=1)
        shifted = pltpu.roll(window, D + LANES - GATE_COLS, axis=1)
        o_ref[...] = shifted[:, :D].astype(o_ref.dtype)


def _regroup_w_in(w_in):
    depth, D, _ = w_in.shape
    return pl.pallas_call(
        _regroup_kernel,
        grid=(depth, N_GROUPS),
        in_specs=[
            pl.BlockSpec((None, D, D), lambda l, k: (l, 0, k)),
            pl.BlockSpec((None, D, LANES), lambda l, k: (l, 0, (k + 1) * (D // LANES))),
        ],
        out_specs=pl.BlockSpec((None, D, D), lambda l, k: (l, 0, k)),
        out_shape=jax.ShapeDtypeStruct((depth, D, N_GROUPS * D), BF16),
        compiler_params=_params("parallel", "parallel"),
        name="regroup_w_in",
    )(w_in, w_in)


def _pick(j, values):
    return sum(jnp.where(j == n, v, 0) for n, v in enumerate(values))


def _proj_kernel(scale_ref, h_ref, w_ref, o_ref):
    scale = scale_ref[pl.program_id(1)]
    o_ref[...] = (_dot(h_ref[...], w_ref[...]) * scale).astype(o_ref.dtype)


def _proj(h, w_b, l, groups, scales, out_dtype):
    T, D = h.shape
    tm = min(TM_PROJ, T)
    return pl.pallas_call(
        _proj_kernel,
        grid=(T // tm, len(groups)),
        in_specs=[
            pl.BlockSpec(memory_space=pltpu.SMEM),
            pl.BlockSpec((tm, D), lambda i, j: (i, 0)),
            pl.BlockSpec((None, D, D), lambda i, j: (l, 0, _pick(j, groups))),
        ],
        out_specs=pl.BlockSpec((tm, D), lambda i, j: (i, j)),
        out_shape=jax.ShapeDtypeStruct((T, len(groups) * D), out_dtype),
        compiler_params=_params("parallel", "parallel"),
        name="proj",
    )(jnp.array(scales, F32), h, w_b)


def _proj_t_kernel(scale_ref, h_ref, w_ref, o_ref):
    scale = scale_ref[pl.program_id(1)]
    out = lax.dot_general(w_ref[...], h_ref[...], (((0,), (1,)), ((), ())), preferred_element_type=F32)
    o_ref[...] = (out * scale).astype(o_ref.dtype)


def _proj_t(h, w_b, l, groups, scales, out_dtype):
    T, D = h.shape
    tm = min(TM_PROJ, T)
    return pl.pallas_call(
        _proj_t_kernel,
        grid=(T // tm, len(groups)),
        in_specs=[
            pl.BlockSpec(memory_space=pltpu.SMEM),
            pl.BlockSpec((tm, D), lambda i, j: (i, 0)),
            pl.BlockSpec((None, D, D), lambda i, j: (l, 0, _pick(j, groups))),
        ],
        out_specs=pl.BlockSpec((D, tm), lambda i, j: (j, i)),
        out_shape=jax.ShapeDtypeStruct((len(groups) * D, T), out_dtype),
        compiler_params=_params("parallel", "parallel"),
        name="proj_t",
    )(jnp.array(scales, F32), h, w_b)


def _row_to_columns(r):
    L = r.shape[1]
    col = jnp.transpose(jnp.broadcast_to(r, (128, L)))
    return jnp.concatenate([col] * (L // 128), axis=1)


def _mlstm_kernel(k_ref, qT_ref, vT_ref, oT_ref, gates_ref, gain_ref, y_ref, c_ref, n_ref, m_ref):
    @pl.when(pl.program_id(1) == 0)
    def _():
        c_ref[...] = jnp.zeros_like(c_ref)
        n_ref[...] = jnp.zeros_like(n_ref)
        m_ref[...] = jnp.zeros_like(m_ref)

    L = k_ref.shape[0]
    dh = ML_HEAD_DIM
    heads = range(ML_HEADS)
    feat = lambda g: slice(g * dh, (g + 1) * dh)
    s_idx = lax.broadcasted_iota(jnp.int32, (L, L), 0)
    t_idx = lax.broadcasted_iota(jnp.int32, (L, L), 1)
    causal = s_idx <= t_idx

    k = [k_ref[:, feat(g)] for g in heads]
    qT = [qT_ref[feat(g), :] for g in heads]
    c_prev = [c_ref[g] for g in heads]
    n_prev = [n_ref[g] for g in heads]
    m_prev = [m_ref[g] for g in heads]
    kq = [_dot(k[g], qT[g]) for g in heads]
    cq = [_dot(c_prev[g].astype(BF16), qT[g]) for g in heads]
    nq = [_dot(n_prev[g].astype(BF16), qT[g])[0:1, :] for g in heads]


    gates = gates_ref[...]
    log_f = _log_sigmoid(gates)
    b_row, u_row, a_inter, s_w, den, num = [], [], [], [], [], []
    for g in heads:
        lf_col = _row_to_columns(log_f[ML_HEADS + g:ML_HEADS + g + 1, :])
        b_row.append(jnp.sum(jnp.where(causal, lf_col, 0.0), axis=0, keepdims=True))
        c_col = _row_to_columns(gates[g:g + 1, :] - b_row[g])
        c_max = jnp.max(jnp.where(causal, c_col, -jnp.inf), axis=0, keepdims=True)
        u_row.append(jnp.maximum(m_prev[g], c_max))
        w = jnp.where(causal, jnp.exp(c_col - u_row[g]), 0.0)
        s_w.append(kq[g] * w)
        a_inter.append(jnp.exp(m_prev[g] - u_row[g]))
        den.append(jnp.sum(s_w[g], axis=0, keepdims=True) + a_inter[g] * nq[g])
        num.append(_dot(vT_ref[feat(g), :], s_w[g].astype(BF16)))

    for g in heads:
        b_last = b_row[g][:, L - 1:L]
        g_row = b_last - b_row[g] + gates[g:g + 1, :]
        m_new = jnp.maximum(b_last + m_prev[g], jnp.max(g_row, axis=1, keepdims=True))
        decay = jnp.exp(b_last + m_prev[g] - m_new)
        w_state = jnp.exp(g_row - m_new)
        vw = (vT_ref[feat(g), :].astype(F32) * w_state).astype(BF16)
        c_ref[g] = decay * c_prev[g] + _dot(vw, k[g])
        w8 = jnp.broadcast_to(w_state, (8, L)).astype(BF16)
        n_ref[g] = decay * n_prev[g] + _dot(w8, k[g])
        m_ref[g] = m_new

    for g in heads:
        floor = jnp.exp(-(b_row[g] + u_row[g]))
        h = (num[g] + a_inter[g] * cq[g]) / jnp.maximum(jnp.abs(den[g]), floor)
        hg = jax.nn.sigmoid(oT_ref[feat(g), :]) * h
        ms = jnp.mean(hg * hg, axis=0, keepdims=True)
        gain = jnp.concatenate([gain_ref[feat(g), :]] * (L // 128), axis=1)
        y_ref[feat(g), :] = (hg * lax.rsqrt(ms + EPS) * gain).astype(y_ref.dtype)


def _mlstm(pn, pt, pt_f, gates, gain, *, batch, seq, k_col, qT_row, vT_row, oT_row):
    T = batch * seq
    L = min(ML_CHUNK, seq)
    nc = seq // L
    dh = ML_HEAD_DIM
    W = ML_HEADS * dh
    tok = lambda b, c: b * nc + c
    return pl.pallas_call(
        _mlstm_kernel,
        grid=(batch, nc),
        in_specs=[
            pl.BlockSpec((L, W), lambda b, c: (tok(b, c), k_col)),
            pl.BlockSpec((W, L), lambda b, c: (qT_row, tok(b, c))),
            pl.BlockSpec((W, L), lambda b, c: (vT_row, tok(b, c))),
            pl.BlockSpec((W, L), lambda b, c: (oT_row, tok(b, c))),
            pl.BlockSpec((2 * ML_HEADS, L), lambda b, c: (0, tok(b, c))),
            pl.BlockSpec((W, 128), lambda b, c: (0, 0)),
        ],
        out_specs=pl.BlockSpec((W, L), lambda b, c: (0, tok(b, c))),
        out_shape=jax.ShapeDtypeStruct((W, T), BF16),
        scratch_shapes=[
            pltpu.VMEM((ML_HEADS, dh, dh), F32),
            pltpu.VMEM((ML_HEADS, 8, dh), F32),
            pltpu.VMEM((ML_HEADS, 1, 1), F32),
        ],
        compiler_params=_params("parallel", "arbitrary"),
        name="mlstm",
    )(pn, pt, pt, pt_f, gates, gain)


def _sb_kernel(qT_ref, k_ref, vT_ref, o_ref, acc_ref, *, blk, heads, qblocks):
    step = pl.program_id(2)
    d = SB_HEAD_DIM

    s_idx = lax.broadcasted_iota(jnp.int32, (blk, blk), 0)
    t_idx = lax.broadcasted_iota(jnp.int32, (blk, blk), 1)
    later = (t_idx > s_idx)
    from_here = jnp.where(t_idx >= s_idx, 1.0, 0.0).astype(BF16)

    row_head = lax.broadcasted_iota(jnp.int32, (2 * d, blk), 0) // d
    q_pads = []
    for qb in range(qblocks):
        q_pads.append([])
        for g in range(heads):
            q2 = qT_ref[(g // 2) * 2 * d:(g // 2 + 1) * 2 * d, qb * blk:(qb + 1) * blk]
            q_pads[qb].append(jnp.where(row_head == g % 2, q2, jnp.zeros_like(q2)))

    def neg_log2_1m(zs):
        neg_abs = pltpu.bitcast(pltpu.bitcast(zs, jnp.uint32) | jnp.uint32(0x80000000), F32)
        return jnp.maximum(zs, 0.0) + jnp.log(1.0 + jnp.exp2(neg_abs)) * LOG2E

    def visit(chains, rs, init_acc):
        starts = [pl.multiple_of(j * blk, blk) for _, j, _, _ in chains]
        z = [[_dot(k_ref[pl.ds(st, blk), (g // 2) * 2 * d:(g // 2 + 1) * 2 * d], q_pads[qb][g])
              for g in range(heads)] for st, (qb, _, _, _) in zip(starts, chains)]
        tail = []
        for c, (_, _, diagonal, _) in enumerate(chains):
            tail.append([])
            for g in range(heads):
                nl1 = neg_log2_1m(z[c][g])
                if diagonal:
                    nl1 = jnp.where(later, nl1, 0.0)
                tail[c].append(_dot(from_here, nl1.astype(BF16)))
        rs = {qb: list(r) for qb, r in rs.items()}
        r_in = []
        for c, (qb, _, _, r_gate) in enumerate(chains):
            r_in.append([rs[qb][g] if r_gate is None else jnp.where(r_gate, rs[qb][g], SB_ABSENT_LOG2)
                         for g in range(heads)])
            rs[qb] = [r_in[c][g] - tail[c][g][0:1, :] for g in range(heads)]
        out = {qb: (jnp.max(functools.reduce(jnp.maximum, r)) > SB_DEAD_LOG2, tuple(r))
               for qb, r in rs.items()}
        started = set()
        for c, (qb, _, diagonal, _) in enumerate(chains):
            for g in range(heads):
                a = jnp.exp2(z[c][g] - tail[c][g] + r_in[c][g])
                if diagonal:
                    a = jnp.where(later, a, 0.0)
                contrib = _dot(vT_ref[g * d:(g + 1) * d, pl.ds(starts[c], blk)], a.astype(BF16))
                if init_acc and qb not in started:
                    acc_ref[g * d:(g + 1) * d, qb * blk:(qb + 1) * blk] = contrib
                else:
                    acc_ref[g * d:(g + 1) * d, qb * blk:(qb + 1) * blk] += contrib
            started.add(qb)
        return out

    blocks = [step * qblocks + qb for qb in range(qblocks)]
    chains = []
    for qb, i in enumerate(blocks):
        chains += [(qb, i, True, None), (qb, jnp.maximum(i - 1, 0), False, i > 0)]
    zeros = tuple(jnp.zeros((1, blk), F32) for _ in range(heads))
    first = visit(chains, {qb: zeros for qb in range(qblocks)}, True)

    for qb, i in enumerate(blocks):
        def cond(c, i=i):
            n, live, _ = c
            return jnp.logical_and(n < i - 1, live)

        def body(c, qb=qb, i=i):
            n, _, rs = c
            return (n + 1,) + visit([(qb, i - 2 - n, False, None)], {qb: rs}, False)[qb]

        lax.while_loop(cond, body, (jnp.int32(0),) + first[qb])
    o_ref[...] = acc_ref[...].astype(o_ref.dtype)


def _stick_breaking(pn_b, pt, *, batch, seq, k_col, qT_row, vT_row):
    T = batch * seq
    blk = min(SB_BLOCK, seq)
    qblocks = min(SB_QBLOCKS, seq // blk)
    tq = qblocks * blk
    nq = seq // tq
    gd = SB_GROUP * SB_HEAD_DIM
    return pl.pallas_call(
        functools.partial(_sb_kernel, blk=blk, heads=SB_GROUP, qblocks=qblocks),
        grid=(batch, SB_HEADS // SB_GROUP, nq),
        in_specs=[
            pl.BlockSpec((gd, tq), lambda b, h, i: (qT_row + h, b * nq + i)),
            pl.BlockSpec((seq, gd), lambda b, h, i: (b, k_col + h)),
            pl.BlockSpec((gd, seq), lambda b, h, i: (vT_row + h, b)),
        ],
        out_specs=pl.BlockSpec((gd, tq), lambda b, h, i: (h, b * nq + i)),
        out_shape=jax.ShapeDtypeStruct((SB_HEADS * SB_HEAD_DIM, T), BF16),
        scratch_shapes=[pltpu.VMEM((gd, tq), F32)],
        compiler_params=_params("parallel", "parallel", "arbitrary"),
        name="stick_breaking",
    )(pt, pn_b, pt)


def _merge_kernel(x_ref, h_ref, hp_ref, wb_ref, wc_ref, wu_ref, wgml_ref, wgsc_ref, wgsb_ref,
                  ymlT_ref, ysbT_ref, cw_ref, wml_ref, wsc_ref, wsb_ref, wout_ref, o_ref, *,
                  tiles_per_seq):
    tm, D = x_ref.shape
    P = hp_ref.shape[0]
    first = (pl.program_id(0) % tiles_per_seq) == 0
    h = h_ref[...]
    h_ext = jnp.concatenate([hp_ref[...], h], axis=0)

    z = _dot(h_ext, wc_ref[...]) * _dot(h_ext, wu_ref[...])
    row = lax.broadcasted_iota(jnp.int32, z.shape, 0)
    z = jnp.where(jnp.logical_and(first, row < P), 0.0, z)
    cw = cw_ref[...]
    conv = (cw[0:1, :] * z[P - 2:P - 2 + tm, :] + cw[1:2, :] * z[P - 1:P - 1 + tm, :]
            + cw[2:3, :] * z[P:P + tm, :])
    y_sc = _dot(h, wb_ref[...]) * conv

    merged = jax.nn.sigmoid(_dot(h, wgml_ref[...])) * _dot_tn(ymlT_ref[...], wml_ref[...])
    merged += jax.nn.sigmoid(_dot(h, wgsc_ref[...])) * _dot(y_sc.astype(BF16), wsc_ref[...])
    merged += jax.nn.sigmoid(_dot(h, wgsb_ref[...])) * _dot_tn(ysbT_ref[...], wsb_ref[...])
    o_ref[...] = x_ref[...] + _dot(merged.astype(BF16), wout_ref[...])


def _merge(x, h, w_b, l, y_mlT, y_sbT, conv_w, w_ml, w_sc, w_sb, w_out, *, seq):
    T, D = x.shape
    tm = min(TM_MERGE, seq)
    const = lambda i: (0, 0)
    resident = lambda shape: pl.BlockSpec(shape, const, pipeline_mode=pl.Buffered(1))
    group = lambda g: pl.BlockSpec((None, D, D), lambda i: (l, 0, g), pipeline_mode=pl.Buffered(1))
    return pl.pallas_call(
        functools.partial(_merge_kernel, tiles_per_seq=seq // tm),
        grid=(T // tm,),
        in_specs=[
            pl.BlockSpec((tm, D), lambda i: (i, 0)),
            pl.BlockSpec((tm, D), lambda i: (i, 0)),
            pl.BlockSpec((PREV_ROWS, D), lambda i: (jnp.maximum(i * (tm // PREV_ROWS) - 1, 0), 0)),
            group(G_B), group(G_C), group(G_U), group(G_GML), group(G_GSC), group(G_GSB),
            pl.BlockSpec((D, tm), lambda i: (0, i)),
            pl.BlockSpec((D, tm), lambda i: (0, i)),
            resident((3, D)),
            resident((D, D)), resident((D, D)), resident((D, D)), resident((D, D)),
        ],
        out_specs=pl.BlockSpec((tm, D), lambda i: (i, 0)),
        out_shape=jax.ShapeDtypeStruct((T, D), F32),
        compiler_params=_params("parallel"),
        name="merge",
    )(x, h, h, w_b, w_b, w_b, w_b, w_b, w_b, y_mlT, y_sbT, conv_w, w_ml, w_sc, w_sb, w_out)


def _mlp_kernel(x_ref, g_ref, wup_ref, wdown_ref, gn_ref, wg_hi_ref, wg_lo_ref, bg_ref, *out_refs,
                ff_chunk, last):
    x = x_ref[...]
    h = _rmsnorm_rows(x, g_ref[...]).astype(BF16)
    acc = x
    for c in range(wup_ref.shape[1] // ff_chunk):
        cols = slice(c * ff_chunk, (c + 1) * ff_chunk)
        up = jnp.maximum(_dot(h, wup_ref[:, cols]), 0.0)
        acc = acc + _dot((up * up).astype(BF16), wdown_ref[cols, :])
    if last:
        out_refs[0][...] = _rmsnorm_rows(acc, gn_ref[...])
    else:
        out_refs[0][...] = acc
        out_refs[1][...], out_refs[2][...] = _prenorm_outputs(acc, gn_ref[...], wg_hi_ref[...],
                                                              wg_lo_ref[...], bg_ref[...])


def _mlp(x, g, w_up, w_down, g_next, wg_hi, wg_lo, bg, *, last):
    T, D = x.shape
    F = w_up.shape[1]
    G = wg_hi.shape[0]
    tm = min(TM_MLP, T)
    const = lambda i: (0, 0)
    resident = lambda shape: pl.BlockSpec(shape, const, pipeline_mode=pl.Buffered(1))
    out_specs = [pl.BlockSpec((tm, D), lambda i: (i, 0))]
    out_shape = [jax.ShapeDtypeStruct((T, D), F32)]
    if not last:
        out_specs += [pl.BlockSpec((tm, D), lambda i: (i, 0)), pl.BlockSpec((G, tm), lambda i: (0, i))]
        out_shape += [jax.ShapeDtypeStruct((T, D), BF16), jax.ShapeDtypeStruct((G, T), F32)]
    return pl.pallas_call(
        functools.partial(_mlp_kernel, ff_chunk=1024, last=last),
        grid=(T // tm,),
        in_specs=[
            pl.BlockSpec((tm, D), lambda i: (i, 0)),
            resident((1, D)),
            resident((D, F)),
            resident((F, D)),
            resident((1, D)),
            resident((G, D)), resident((G, D)), resident((G, 1)),
        ],
        out_specs=out_specs,
        out_shape=out_shape,
        compiler_params=_params("parallel"),
        name="mlp",
    )(x, g, w_up, w_down, g_next, wg_hi, wg_lo, bg)


def _gate_params(w_in, l, b_if):
    off = 4 * ML_HEADS * ML_HEAD_DIM
    wg_hi, wg_lo = _split_bf16(w_in[l, :, off:off + 2 * ML_HEADS].T)
    return wg_hi, wg_lo, b_if.reshape(2 * ML_HEADS, 1)


def _layer(x, h, gates, p, nxt, *, batch, seq):
    D = D_MODEL
    ml_w = ML_HEADS * ML_HEAD_DIM
    w_b, l = p["w_in"]
    pn = _proj(h, w_b, l, (G_MK, G_SK), (ML_HEAD_DIM ** -0.5, 1.0), BF16)
    pt = _proj_t(h, w_b, l, (G_MQ, G_MV, G_SQ, G_SV), (1.0, 1.0, SB_HEAD_DIM ** -0.5 * LOG2E, 1.0), BF16)
    pt_f = _proj_t(h, w_b, l, (G_MO,), (1.0,), F32)

    gain = jnp.broadcast_to(p["ml_norm_g"].reshape(ml_w, 1), (ml_w, 128))
    y_mlT = _mlstm(pn, pt, pt_f, gates, gain, batch=batch, seq=seq,
                   k_col=0, qT_row=0, vT_row=1, oT_row=0)
    sb_unit = SB_GROUP * SB_HEAD_DIM
    y_sbT = _stick_breaking(pn, pt, batch=batch, seq=seq, k_col=ml_w // sb_unit,
                            qT_row=2 * ml_w // sb_unit, vT_row=(2 * ml_w + D) // sb_unit)
    x = _merge(x, h, w_b, l, y_mlT, y_sbT, p["conv_w"],
               p["w_ml_proj"].astype(BF16), p["w_sc_proj"].astype(BF16),
               p["w_sb_proj"].astype(BF16), p["w_out"].astype(BF16), seq=seq)
    return _mlp(x, p["norm_mlp_g"].reshape(1, D), p["w_up"].astype(BF16), p["w_down"].astype(BF16),
                nxt["g"].reshape(1, D), nxt["wg_hi"], nxt["wg_lo"], nxt["bg"], last=nxt["last"])


def kernel(x, norm_mix_g, w_in, b_if, ml_norm_g, conv_w, w_ml_proj, w_sc_proj, w_sb_proj, w_out,
           norm_mlp_g, w_up, w_down, norm_final_g):
    batch, seq, D = x.shape
    depth = w_in.shape[0]
    xt = x.reshape(batch * seq, D)
    gate_params = [_gate_params(w_in, l, b_if[l]) for l in range(depth)]
    h, gates = _prenorm(xt, norm_mix_g[0].reshape(1, D), *gate_params[0])
    w_b = _regroup_w_in(w_in)
    for l in range(depth):
        p = dict(w_in=(w_b, l), ml_norm_g=ml_norm_g[l], conv_w=conv_w[l], w_ml_proj=w_ml_proj[l],
                 w_sc_proj=w_sc_proj[l], w_sb_proj=w_sb_proj[l], w_out=w_out[l],
                 norm_mlp_g=norm_mlp_g[l], w_up=w_up[l], w_down=w_down[l])
        last = l == depth - 1
        nl = l if last else l + 1
        nxt = dict(g=norm_final_g if last else norm_mix_g[nl], wg_hi=gate_params[nl][0],
                   wg_lo=gate_params[nl][1], bg=gate_params[nl][2], last=last)
        outs = _layer(xt, h, gates, p, nxt, batch=batch, seq=seq)
        if last:
            xt = outs[0]
        else:
            xt, h, gates = outs
    return xt.reshape(batch, seq, D)
```

```python
import functools

import jax
import jax.numpy as jnp
from jax import lax
from jax.experimental import pallas as pl
from jax.experimental.pallas import tpu as pltpu

D_MODEL = 1024
ML_HEADS = 4
ML_HEAD_DIM = 256
SB_HEADS = 16
SB_HEAD_DIM = 64
D_FF = 4 * D_MODEL
EPS = 1e-6

VMEM_LIMIT_BYTES = 56 * 1024 * 1024

ML_CHUNK = 256
SB_BLOCK = 256
SB_GROUP = 4
SB_QBLOCKS = 2
LOG2E = 1.4426950408889634
SB_DEAD_LOG2 = -152.0
SB_ABSENT_LOG2 = -1e30
TM_PROJ = 2048
TM_MERGE = 512
TM_MLP = 512
PREV_ROWS = 16

BF16 = jnp.bfloat16
F32 = jnp.float32


def _params(*sem):
    return pltpu.CompilerParams(dimension_semantics=sem, vmem_limit_bytes=VMEM_LIMIT_BYTES)


def _dot(a, b):
    return jnp.dot(a, b, preferred_element_type=F32)


def _dot_nt(a, b):
    return lax.dot_general(a, b, (((1,), (1,)), ((), ())), preferred_element_type=F32)


def _dot_tn(a, b):
    return lax.dot_general(a, b, (((0,), (0,)), ((), ())), preferred_element_type=F32)


def _rmsnorm_rows(x, g):
    ms = jnp.mean(x * x, axis=-1, keepdims=True)
    return x * lax.rsqrt(ms + EPS) * g


def _log_sigmoid(x):
    return jnp.minimum(x, 0.0) - jnp.log(1.0 + jnp.exp(-jnp.abs(x)))


def _split_bf16(x):
    hi = x.astype(BF16)
    lo = (x - hi.astype(F32)).astype(BF16)
    return hi, lo


def _prenorm_outputs(x, g, wg_hi, wg_lo, bg):
    h_hi, h_lo = _split_bf16(_rmsnorm_rows(x, g))
    gates = _dot_nt(wg_hi, h_hi) + (_dot_nt(wg_hi, h_lo) + _dot_nt(wg_lo, h_hi)) + bg
    return h_hi, gates


def _prenorm_kernel(x_ref, g_ref, wg_hi_ref, wg_lo_ref, bg_ref, h_ref, gates_ref):
    h_ref[...], gates_ref[...] = _prenorm_outputs(x_ref[...], g_ref[...], wg_hi_ref[...],
                                                  wg_lo_ref[...], bg_ref[...])


def _prenorm(x, g, wg_hi, wg_lo, bg):
    T, D = x.shape
    G = wg_hi.shape[0]
    tm = min(TM_PROJ, T)
    const = lambda i: (0, 0)
    return pl.pallas_call(
        _prenorm_kernel,
        grid=(T // tm,),
        in_specs=[
            pl.BlockSpec((tm, D), lambda i: (i, 0)),
            pl.BlockSpec((1, D), const),
            pl.BlockSpec((G, D), const),
            pl.BlockSpec((G, D), const),
            pl.BlockSpec((G, 1), const),
        ],
        out_specs=[pl.BlockSpec((tm, D), lambda i: (i, 0)), pl.BlockSpec((G, tm), lambda i: (0, i))],
        out_shape=[jax.ShapeDtypeStruct((T, D), BF16), jax.ShapeDtypeStruct((G, T), F32)],
        compiler_params=_params("parallel"),
        name="prenorm",
    )(x, g, wg_hi, wg_lo, bg)


N_GROUPS = 13
ALIGNED_GROUPS = 4
GATE_ROWS = 2 * ML_HEADS
(G_MQ, G_MK, G_MV, G_MO, G_B, G_C, G_U, G_SQ, G_SK, G_SV, G_GML, G_GSC, G_GSB) = range(N_GROUPS)


def _regroup_kernel(a_ref, b_ref, o_ref):
    k = pl.program_id(1)

    @pl.when(k < ALIGNED_GROUPS)
    def _():
        o_ref[...] = a_ref[...].astype(o_ref.dtype)

    @pl.when(k >= ALIGNED_GROUPS)
    def _():
        window = jnp.concatenate([a_ref[GATE_ROWS:, :], b_ref[...]], axis=0)
        o_ref[...] = window.astype(o_ref.dtype)


def _regroup_w_in_t(w_in_t):
    depth, _, D = w_in_t.shape
    return pl.pallas_call(
        _regroup_kernel,
        grid=(depth, N_GROUPS),
        in_specs=[
            pl.BlockSpec((None, D, D), lambda l, k: (l, k, 0)),
            pl.BlockSpec((None, GATE_ROWS, D), lambda l, k: (l, (k + 1) * (D // GATE_ROWS), 0)),
        ],
        out_specs=pl.BlockSpec((None, D, D), lambda l, k: (l, k, 0)),
        out_shape=jax.ShapeDtypeStruct((depth, N_GROUPS * D, D), BF16),
        compiler_params=_params("parallel", "parallel"),
        name="regroup_w_in",
    )(w_in_t, w_in_t)


def _pick(j, values):
    return sum(jnp.where(j == n, v, 0) for n, v in enumerate(values))


def _proj_kernel(scale_ref, h_ref, w_ref, o_ref):
    scale = scale_ref[pl.program_id(1)]
    o_ref[...] = (_dot_nt(h_ref[...], w_ref[...]) * scale).astype(o_ref.dtype)


def _proj(h, wt, l, groups, scales, out_dtype):
    T, D = h.shape
    tm = min(TM_PROJ, T)
    return pl.pallas_call(
        _proj_kernel,
        grid=(T // tm, len(groups)),
        in_specs=[
            pl.BlockSpec(memory_space=pltpu.SMEM),
            pl.BlockSpec((tm, D), lambda i, j: (i, 0)),
            pl.BlockSpec((None, D, D), lambda i, j: (l, _pick(j, groups), 0)),
        ],
        out_specs=pl.BlockSpec((tm, D), lambda i, j: (i, j)),
        out_shape=jax.ShapeDtypeStruct((T, len(groups) * D), out_dtype),
        compiler_params=_params("parallel", "parallel"),
        name="proj",
    )(jnp.array(scales, F32), h, wt)


def _proj_t_kernel(scale_ref, h_ref, w_ref, o_ref):
    scale = scale_ref[pl.program_id(1)]
    o_ref[...] = (_dot_nt(w_ref[...], h_ref[...]) * scale).astype(o_ref.dtype)


def _proj_t(h, wt, l, groups, scales, out_dtype):
    T, D = h.shape
    tm = min(TM_PROJ, T)
    return pl.pallas_call(
        _proj_t_kernel,
        grid=(T // tm, len(groups)),
        in_specs=[
            pl.BlockSpec(memory_space=pltpu.SMEM),
            pl.BlockSpec((tm, D), lambda i, j: (i, 0)),
            pl.BlockSpec((None, D, D), lambda i, j: (l, _pick(j, groups), 0)),
        ],
        out_specs=pl.BlockSpec((D, tm), lambda i, j: (j, i)),
        out_shape=jax.ShapeDtypeStruct((len(groups) * D, T), out_dtype),
        compiler_params=_params("parallel", "parallel"),
        name="proj_t",
    )(jnp.array(scales, F32), h, wt)


def _row_to_columns(r):
    L = r.shape[1]
    col = jnp.transpose(jnp.broadcast_to(r, (128, L)))
    return jnp.concatenate([col] * (L // 128), axis=1)


def _mlstm_kernel(k_ref, qT_ref, vT_ref, oT_ref, gates_ref, gain_ref, y_ref, c_ref, n_ref, m_ref):
    @pl.when(pl.program_id(1) == 0)
    def _():
        c_ref[...] = jnp.zeros_like(c_ref)
        n_ref[...] = jnp.zeros_like(n_ref)
        m_ref[...] = jnp.zeros_like(m_ref)

    L = k_ref.shape[0]
    dh = ML_HEAD_DIM
    heads = range(ML_HEADS)
    feat = lambda g: slice(g * dh, (g + 1) * dh)
    s_idx = lax.broadcasted_iota(jnp.int32, (L, L), 0)
    t_idx = lax.broadcasted_iota(jnp.int32, (L, L), 1)
    causal = s_idx <= t_idx

    k = [k_ref[:, feat(g)] for g in heads]
    qT = [qT_ref[feat(g), :] for g in heads]
    c_prev = [c_ref[g] for g in heads]
    n_prev = [n_ref[g] for g in heads]
    m_prev = [m_ref[g] for g in heads]
    kq = [_dot(k[g], qT[g]) for g in heads]
    cq = [_dot(c_prev[g].astype(BF16), qT[g]) for g in heads]
    nq = [_dot(n_prev[g].astype(BF16), qT[g])[0:1, :] for g in heads]


    gates = gates_ref[...]
    log_f = _log_sigmoid(gates)
    b_row, u_row, a_inter, s_w, den, num = [], [], [], [], [], []
    for g in heads:
        lf_col = _row_to_columns(log_f[ML_HEADS + g:ML_HEADS + g + 1, :])
        b_row.append(jnp.sum(jnp.where(causal, lf_col, 0.0), axis=0, keepdims=True))
        c_col = _row_to_columns(gates[g:g + 1, :] - b_row[g])
        c_max = jnp.max(jnp.where(causal, c_col, -jnp.inf), axis=0, keepdims=True)
        u_row.append(jnp.maximum(m_prev[g], c_max))
        w = jnp.where(causal, jnp.exp(c_col - u_row[g]), 0.0)
        s_w.append(kq[g] * w)
        a_inter.append(jnp.exp(m_prev[g] - u_row[g]))
        den.append(jnp.sum(s_w[g], axis=0, keepdims=True) + a_inter[g] * nq[g])
        num.append(_dot(vT_ref[feat(g), :], s_w[g].astype(BF16)))

    for g in heads:
        b_last = b_row[g][:, L - 1:L]
        g_row = b_last - b_row[g] + gates[g:g + 1, :]
        m_new = jnp.maximum(b_last + m_prev[g], jnp.max(g_row, axis=1, keepdims=True))
        decay = jnp.exp(b_last + m_prev[g] - m_new)
        w_state = jnp.exp(g_row - m_new)
        vw = (vT_ref[feat(g), :].astype(F32) * w_state).astype(BF16)
        c_ref[g] = decay * c_prev[g] + _dot(vw, k[g])
        w8 = jnp.broadcast_to(w_state, (8, L)).astype(BF16)
        n_ref[g] = decay * n_prev[g] + _dot(w8, k[g])
        m_ref[g] = m_new

    for g in heads:
        floor = jnp.exp(-(b_row[g] + u_row[g]))
        h = (num[g] + a_inter[g] * cq[g]) / jnp.maximum(jnp.abs(den[g]), floor)
        hg = jax.nn.sigmoid(oT_ref[feat(g), :]) * h
        ms = jnp.mean(hg * hg, axis=0, keepdims=True)
        gain = jnp.concatenate([gain_ref[feat(g), :]] * (L // 128), axis=1)
        y_ref[feat(g), :] = (hg * lax.rsqrt(ms + EPS) * gain).astype(y_ref.dtype)


def _mlstm(pn, pt, pt_f, gates, gain, *, batch, seq, k_col, qT_row, vT_row, oT_row):
    T = batch * seq
    L = min(ML_CHUNK, seq)
    nc = seq // L
    dh = ML_HEAD_DIM
    W = ML_HEADS * dh
    tok = lambda b, c: b * nc + c
    return pl.pallas_call(
        _mlstm_kernel,
        grid=(batch, nc),
        in_specs=[
            pl.BlockSpec((L, W), lambda b, c: (tok(b, c), k_col)),
            pl.BlockSpec((W, L), lambda b, c: (qT_row, tok(b, c))),
            pl.BlockSpec((W, L), lambda b, c: (vT_row, tok(b, c))),
            pl.BlockSpec((W, L), lambda b, c: (oT_row, tok(b, c))),
            pl.BlockSpec((2 * ML_HEADS, L), lambda b, c: (0, tok(b, c))),
            pl.BlockSpec((W, 128), lambda b, c: (0, 0)),
        ],
        out_specs=pl.BlockSpec((W, L), lambda b, c: (0, tok(b, c))),
        out_shape=jax.ShapeDtypeStruct((W, T), BF16),
        scratch_shapes=[
            pltpu.VMEM((ML_HEADS, dh, dh), F32),
            pltpu.VMEM((ML_HEADS, 8, dh), F32),
            pltpu.VMEM((ML_HEADS, 1, 1), F32),
        ],
        compiler_params=_params("parallel", "arbitrary"),
        name="mlstm",
    )(pn, pt, pt, pt_f, gates, gain)


def _sb_kernel(qT_ref, k_ref, vT_ref, o_ref, acc_ref, *, blk, heads, qblocks):
    step = pl.program_id(2)
    d = SB_HEAD_DIM

    s_idx = lax.broadcasted_iota(jnp.int32, (blk, blk), 0)
    t_idx = lax.broadcasted_iota(jnp.int32, (blk, blk), 1)
    later = (t_idx > s_idx)
    from_here = jnp.where(t_idx >= s_idx, 1.0, 0.0).astype(BF16)

    row_head = lax.broadcasted_iota(jnp.int32, (2 * d, blk), 0) // d
    q_pads = []
    for qb in range(qblocks):
        q_pads.append([])
        for g in range(heads):
            q2 = qT_ref[(g // 2) * 2 * d:(g // 2 + 1) * 2 * d, qb * blk:(qb + 1) * blk]
            q_pads[qb].append(jnp.where(row_head == g % 2, q2, jnp.zeros_like(q2)))

    def neg_log2_1m(zs):
        neg_abs = pltpu.bitcast(pltpu.bitcast(zs, jnp.uint32) | jnp.uint32(0x80000000), F32)
        return jnp.maximum(zs, 0.0) + jnp.log(1.0 + jnp.exp2(neg_abs)) * LOG2E

    def visit(chains, rs, init_acc):
        starts = [pl.multiple_of(j * blk, blk) for _, j, _, _ in chains]
        z = [[_dot(k_ref[pl.ds(st, blk), (g // 2) * 2 * d:(g // 2 + 1) * 2 * d], q_pads[qb][g])
              for g in range(heads)] for st, (qb, _, _, _) in zip(starts, chains)]
        tail = []
        for c, (_, _, diagonal, _) in enumerate(chains):
            tail.append([])
            for g in range(heads):
                nl1 = neg_log2_1m(z[c][g])
                if diagonal:
                    nl1 = jnp.where(later, nl1, 0.0)
                tail[c].append(_dot(from_here, nl1.astype(BF16)))
        rs = {qb: list(r) for qb, r in rs.items()}
        r_in = []
        for c, (qb, _, _, r_gate) in enumerate(chains):
            r_in.append([rs[qb][g] if r_gate is None else jnp.where(r_gate, rs[qb][g], SB_ABSENT_LOG2)
                         for g in range(heads)])
            rs[qb] = [r_in[c][g] - tail[c][g][0:1, :] for g in range(heads)]
        out = {qb: (jnp.max(functools.reduce(jnp.maximum, r)) > SB_DEAD_LOG2, tuple(r))
               for qb, r in rs.items()}
        started = set()
        for c, (qb, _, diagonal, _) in enumerate(chains):
            for g in range(heads):
                a = jnp.exp2(z[c][g] - tail[c][g] + r_in[c][g])
                if diagonal:
                    a = jnp.where(later, a, 0.0)
                contrib = _dot(vT_ref[g * d:(g + 1) * d, pl.ds(starts[c], blk)], a.astype(BF16))
                if init_acc and qb not in started:
                    acc_ref[g * d:(g + 1) * d, qb * blk:(qb + 1) * blk] = contrib
                else:
                    acc_ref[g * d:(g + 1) * d, qb * blk:(qb + 1) * blk] += contrib
            started.add(qb)
        return out

    blocks = [step * qblocks + qb for qb in range(qblocks)]
    chains = []
    for qb, i in enumerate(blocks):
        chains += [(qb, i, True, None), (qb, jnp.maximum(i - 1, 0), False, i > 0)]
    zeros = tuple(jnp.zeros((1, blk), F32) for _ in range(heads))
    first = visit(chains, {qb: zeros for qb in range(qblocks)}, True)

    for qb, i in enumerate(blocks):
        def cond(c, i=i):
            n, live, _ = c
            return jnp.logical_and(n < i - 1, live)

        def body(c, qb=qb, i=i):
            n, _, rs = c
            return (n + 1,) + visit([(qb, i - 2 - n, False, None)], {qb: rs}, False)[qb]

        lax.while_loop(cond, body, (jnp.int32(0),) + first[qb])
    o_ref[...] = acc_ref[...].astype(o_ref.dtype)


def _stick_breaking(pn_b, pt, *, batch, seq, k_col, qT_row, vT_row):
    T = batch * seq
    blk = min(SB_BLOCK, seq)
    qblocks = min(SB_QBLOCKS, seq // blk)
    tq = qblocks * blk
    nq = seq // tq
    gd = SB_GROUP * SB_HEAD_DIM
    return pl.pallas_call(
        functools.partial(_sb_kernel, blk=blk, heads=SB_GROUP, qblocks=qblocks),
        grid=(batch, SB_HEADS // SB_GROUP, nq),
        in_specs=[
            pl.BlockSpec((gd, tq), lambda b, h, i: (qT_row + h, b * nq + i)),
            pl.BlockSpec((seq, gd), lambda b, h, i: (b, k_col + h)),
            pl.BlockSpec((gd, seq), lambda b, h, i: (vT_row + h, b)),
        ],
        out_specs=pl.BlockSpec((gd, tq), lambda b, h, i: (h, b * nq + i)),
        out_shape=jax.ShapeDtypeStruct((SB_HEADS * SB_HEAD_DIM, T), BF16),
        scratch_shapes=[pltpu.VMEM((gd, tq), F32)],
        compiler_params=_params("parallel", "parallel", "arbitrary"),
        name="stick_breaking",
    )(pt, pn_b, pt)


def _merge_kernel(x_ref, h_ref, hp_ref, wb_ref, wc_ref, wu_ref, wgml_ref, wgsc_ref, wgsb_ref,
                  ymlT_ref, ysbT_ref, cw_ref, wml_ref, wsc_ref, wsb_ref, wout_ref, o_ref, *,
                  tiles_per_seq):
    tm, D = x_ref.shape
    P = hp_ref.shape[0]
    first = (pl.program_id(0) % tiles_per_seq) == 0
    h = h_ref[...]
    h_ext = jnp.concatenate([hp_ref[...], h], axis=0)

    z = _dot_nt(h_ext, wc_ref[...]) * _dot_nt(h_ext, wu_ref[...])
    row = lax.broadcasted_iota(jnp.int32, z.shape, 0)
    z = jnp.where(jnp.logical_and(first, row < P), 0.0, z)
    cw = cw_ref[...]
    conv = (cw[0:1, :] * z[P - 2:P - 2 + tm, :] + cw[1:2, :] * z[P - 1:P - 1 + tm, :]
            + cw[2:3, :] * z[P:P + tm, :])
    y_sc = _dot_nt(h, wb_ref[...]) * conv

    merged = jax.nn.sigmoid(_dot_nt(h, wgml_ref[...])) * _dot_tn(ymlT_ref[...], wml_ref[...])
    merged += jax.nn.sigmoid(_dot_nt(h, wgsc_ref[...])) * _dot(y_sc.astype(BF16), wsc_ref[...])
    merged += jax.nn.sigmoid(_dot_nt(h, wgsb_ref[...])) * _dot_tn(ysbT_ref[...], wsb_ref[...])
    o_ref[...] = x_ref[...] + _dot(merged.astype(BF16), wout_ref[...])


def _merge(x, h, wt, l, y_mlT, y_sbT, conv_w, w_ml, w_sc, w_sb, w_out, *, seq):
    T, D = x.shape
    tm = min(TM_MERGE, seq)
    const = lambda i: (0, 0)
    resident = lambda shape: pl.BlockSpec(shape, const, pipeline_mode=pl.Buffered(1))
    group = lambda g: pl.BlockSpec((None, D, D), lambda i: (l, g, 0), pipeline_mode=pl.Buffered(1))
    return pl.pallas_call(
        functools.partial(_merge_kernel, tiles_per_seq=seq // tm),
        grid=(T // tm,),
        in_specs=[
            pl.BlockSpec((tm, D), lambda i: (i, 0)),
            pl.BlockSpec((tm, D), lambda i: (i, 0)),
            pl.BlockSpec((PREV_ROWS, D), lambda i: (jnp.maximum(i * (tm // PREV_ROWS) - 1, 0), 0)),
            group(G_B), group(G_C), group(G_U), group(G_GML), group(G_GSC), group(G_GSB),
            pl.BlockSpec((D, tm), lambda i: (0, i)),
            pl.BlockSpec((D, tm), lambda i: (0, i)),
            resident((3, D)),
            resident((D, D)), resident((D, D)), resident((D, D)), resident((D, D)),
        ],
        out_specs=pl.BlockSpec((tm, D), lambda i: (i, 0)),
        out_shape=jax.ShapeDtypeStruct((T, D), F32),
        compiler_params=_params("parallel"),
        name="merge",
    )(x, h, h, wt, wt, wt, wt, wt, wt, y_mlT, y_sbT, conv_w, w_ml, w_sc, w_sb, w_out)


def _mlp_kernel(x_ref, g_ref, wup_ref, wdown_ref, gn_ref, wg_hi_ref, wg_lo_ref, bg_ref, *out_refs,
                ff_chunk, last):
    x = x_ref[...]
    h = _rmsnorm_rows(x, g_ref[...]).astype(BF16)
    acc = x
    for c in range(wup_ref.shape[1] // ff_chunk):
        cols = slice(c * ff_chunk, (c + 1) * ff_chunk)
        up = jnp.maximum(_dot(h, wup_ref[:, cols]), 0.0)
        acc = acc + _dot((up * up).astype(BF16), wdown_ref[cols, :])
    if last:
        out_refs[0][...] = _rmsnorm_rows(acc, gn_ref[...])
    else:
        out_refs[0][...] = acc
        out_refs[1][...], out_refs[2][...] = _prenorm_outputs(acc, gn_ref[...], wg_hi_ref[...],
                                                              wg_lo_ref[...], bg_ref[...])


def _mlp(x, g, w_up, w_down, g_next, wg_hi, wg_lo, bg, *, last):
    T, D = x.shape
    F = w_up.shape[1]
    G = wg_hi.shape[0]
    tm = min(TM_MLP, T)
    const = lambda i: (0, 0)
    resident = lambda shape: pl.BlockSpec(shape, const, pipeline_mode=pl.Buffered(1))
    out_specs = [pl.BlockSpec((tm, D), lambda i: (i, 0))]
    out_shape = [jax.ShapeDtypeStruct((T, D), F32)]
    if not last:
        out_specs += [pl.BlockSpec((tm, D), lambda i: (i, 0)), pl.BlockSpec((G, tm), lambda i: (0, i))]
        out_shape += [jax.ShapeDtypeStruct((T, D), BF16), jax.ShapeDtypeStruct((G, T), F32)]
    return pl.pallas_call(
        functools.partial(_mlp_kernel, ff_chunk=1024, last=last),
        grid=(T // tm,),
        in_specs=[
            pl.BlockSpec((tm, D), lambda i: (i, 0)),
            resident((1, D)),
            resident((D, F)),
            resident((F, D)),
            resident((1, D)),
            resident((G, D)), resident((G, D)), resident((G, 1)),
        ],
        out_specs=out_specs,
        out_shape=out_shape,
        compiler_params=_params("parallel"),
        name="mlp",
    )(x, g, w_up, w_down, g_next, wg_hi, wg_lo, bg)


def _gate_params(w_in_t, l, b_if):
    off = 4 * ML_HEADS * ML_HEAD_DIM
    wg_hi, wg_lo = _split_bf16(w_in_t[l, off:off + 2 * ML_HEADS, :])
    return wg_hi, wg_lo, b_if.reshape(2 * ML_HEADS, 1)


def _layer(x, h, gates, p, nxt, *, batch, seq):
    D = D_MODEL
    ml_w = ML_HEADS * ML_HEAD_DIM
    wt, l = p["w_in"]
    pn = _proj(h, wt, l, (G_MK, G_SK), (ML_HEAD_DIM ** -0.5, 1.0), BF16)
    pt = _proj_t(h, wt, l, (G_MQ, G_MV, G_SQ, G_SV), (1.0, 1.0, SB_HEAD_DIM ** -0.5 * LOG2E, 1.0), BF16)
    pt_f = _proj_t(h, wt, l, (G_MO,), (1.0,), F32)

    gain = jnp.broadcast_to(p["ml_norm_g"].reshape(ml_w, 1), (ml_w, 128))
    y_mlT = _mlstm(pn, pt, pt_f, gates, gain, batch=batch, seq=seq,
                   k_col=0, qT_row=0, vT_row=1, oT_row=0)
    sb_unit = SB_GROUP * SB_HEAD_DIM
    y_sbT = _stick_breaking(pn, pt, batch=batch, seq=seq, k_col=ml_w // sb_unit,
                            qT_row=2 * ml_w // sb_unit, vT_row=(2 * ml_w + D) // sb_unit)
    x = _merge(x, h, wt, l, y_mlT, y_sbT, p["conv_w"],
               p["w_ml_proj"].astype(BF16), p["w_sc_proj"].astype(BF16),
               p["w_sb_proj"].astype(BF16), p["w_out"].astype(BF16), seq=seq)
    return _mlp(x, p["norm_mlp_g"].reshape(1, D), p["w_up"].astype(BF16), p["w_down"].astype(BF16),
                nxt["g"].reshape(1, D), nxt["wg_hi"], nxt["wg_lo"], nxt["bg"], last=nxt["last"])


def kernel(x, norm_mix_g, w_in, b_if, ml_norm_g, conv_w, w_ml_proj, w_sc_proj, w_sb_proj, w_out,
           norm_mlp_g, w_up, w_down, norm_final_g):
    batch, seq, D = x.shape
    depth = w_in.shape[0]
    xt = x.reshape(batch * seq, D)
    w_in_t = jnp.swapaxes(w_in, 1, 2)
    gate_params = [_gate_params(w_in_t, l, b_if[l]) for l in range(depth)]
    h, gates = _prenorm(xt, norm_mix_g[0].reshape(1, D), *gate_params[0])
    wt = _regroup_w_in_t(w_in_t)
    for l in range(depth):
        p = dict(w_in=(wt, l), ml_norm_g=ml_norm_g[l], conv_w=conv_w[l], w_ml_proj=w_ml_proj[l],
                 w_sc_proj=w_sc_proj[l], w_sb_proj=w_sb_proj[l], w_out=w_out[l],
                 norm_mlp_g=norm_mlp_g[l], w_up=w_up[l], w_down=w_down[l])
        last = l == depth - 1
        nl = l if last else l + 1
        nxt = dict(g=norm_final_g if last else norm_mix_g[nl], wg_hi=gate_params[nl][0],
                   wg_lo=gate_params[nl][1], bg=gate_params[nl][2], last=last)
        outs = _layer(xt, h, gates, p, nxt, batch=batch, seq=seq)
        if last:
            xt = outs[0]
        else:
            xt, h, gates = outs
    return xt.reshape(batch, seq, D)
```

```python
import functools

import jax
import jax.numpy as jnp
from jax import lax
from jax.experimental import pallas as pl
from jax.experimental.pallas import tpu as pltpu

D_MODEL = 1024
ML_HEADS = 4
ML_HEAD_DIM = 256
SB_HEADS = 16
SB_HEAD_DIM = 64
D_FF = 4 * D_MODEL
EPS = 1e-6

VMEM_LIMIT_BYTES = 56 * 1024 * 1024

ML_CHUNK = 256
SB_BLOCK = 256
SB_GROUP = 4
SB_QBLOCKS = 2
LOG2E = 1.4426950408889634
SB_DEAD_LOG2 = -152.0
SB_ABSENT_LOG2 = -1e30
TM_PROJ = 2048
TM_MERGE = 512
TM_MLP = 512
PREV_ROWS = 16

BF16 = jnp.bfloat16
F32 = jnp.float32


def _params(*sem):
    return pltpu.CompilerParams(dimension_semantics=sem, vmem_limit_bytes=VMEM_LIMIT_BYTES)


def _dot(a, b):
    return jnp.dot(a, b, preferred_element_type=F32)


def _dot_nt(a, b):
    return lax.dot_general(a, b, (((1,), (1,)), ((), ())), preferred_element_type=F32)


def _dot_tn(a, b):
    return lax.dot_general(a, b, (((0,), (0,)), ((), ())), preferred_element_type=F32)


def _rmsnorm_rows(x, g):
    ms = jnp.mean(x * x, axis=-1, keepdims=True)
    return x * lax.rsqrt(ms + EPS) * g


def _log_sigmoid(x):
    return jnp.minimum(x, 0.0) - jnp.log(1.0 + jnp.exp(-jnp.abs(x)))


def _split_bf16(x):
    hi = x.astype(BF16)
    lo = (x - hi.astype(F32)).astype(BF16)
    return hi, lo


def _prenorm_outputs(x, g, wg_hi, wg_lo, bg):
    h_hi, h_lo = _split_bf16(_rmsnorm_rows(x, g))
    gates = _dot_nt(wg_hi, h_hi) + (_dot_nt(wg_hi, h_lo) + _dot_nt(wg_lo, h_hi)) + bg
    return h_hi, gates


def _prenorm_kernel(x_ref, g_ref, wg_hi_ref, wg_lo_ref, bg_ref, h_ref, gates_ref):
    h_ref[...], gates_ref[...] = _prenorm_outputs(x_ref[...], g_ref[...], wg_hi_ref[...],
                                                  wg_lo_ref[...], bg_ref[...])


def _prenorm(x, g, wg_hi, wg_lo, bg):
    T, D = x.shape
    G = wg_hi.shape[0]
    tm = min(TM_PROJ, T)
    const = lambda i: (0, 0)
    return pl.pallas_call(
        _prenorm_kernel,
        grid=(T // tm,),
        in_specs=[
            pl.BlockSpec((tm, D), lambda i: (i, 0)),
            pl.BlockSpec((1, D), const),
            pl.BlockSpec((G, D), const),
            pl.BlockSpec((G, D), const),
            pl.BlockSpec((G, 1), const),
        ],
        out_specs=[pl.BlockSpec((tm, D), lambda i: (i, 0)), pl.BlockSpec((G, tm), lambda i: (0, i))],
        out_shape=[jax.ShapeDtypeStruct((T, D), BF16), jax.ShapeDtypeStruct((G, T), F32)],
        compiler_params=_params("parallel"),
        name="prenorm",
    )(x, g, wg_hi, wg_lo, bg)


N_GROUPS = 13
ALIGNED_GROUPS = 4
GATE_ROWS = 2 * ML_HEADS
(G_MQ, G_MK, G_MV, G_MO, G_B, G_C, G_U, G_SQ, G_SK, G_SV, G_GML, G_GSC, G_GSB) = range(N_GROUPS)


def _regroup_kernel(a_ref, b_ref, o_ref):
    k = pl.program_id(1)

    @pl.when(k < ALIGNED_GROUPS)
    def _():
        o_ref[...] = a_ref[...].astype(o_ref.dtype)

    @pl.when(k >= ALIGNED_GROUPS)
    def _():
        window = jnp.concatenate([a_ref[GATE_ROWS:, :], b_ref[...]], axis=0)
        o_ref[...] = window.astype(o_ref.dtype)


def _regroup_w_in_t(w_in_t):
    depth, _, D = w_in_t.shape
    return pl.pallas_call(
        _regroup_kernel,
        grid=(depth, N_GROUPS),
        in_specs=[
            pl.BlockSpec((None, D, D), lambda l, k: (l, k, 0)),
            pl.BlockSpec((None, GATE_ROWS, D), lambda l, k: (l, (k + 1) * (D // GATE_ROWS), 0)),
        ],
        out_specs=pl.BlockSpec((None, D, D), lambda l, k: (l, k, 0)),
        out_shape=jax.ShapeDtypeStruct((depth, N_GROUPS * D, D), BF16),
        compiler_params=_params("parallel", "parallel"),
        name="regroup_w_in",
    )(w_in_t, w_in_t)


def _pick(j, values):
    return sum(jnp.where(j == n, v, 0) for n, v in enumerate(values))


def _proj_kernel(scale_ref, h_ref, w_ref, o_ref):
    scale = scale_ref[pl.program_id(1)]
    o_ref[...] = (_dot_nt(h_ref[...], w_ref[...]) * scale).astype(o_ref.dtype)


def _proj(h, wt, l, groups, scales, out_dtype):
    T, D = h.shape
    tm = min(TM_PROJ, T)
    return pl.pallas_call(
        _proj_kernel,
        grid=(T // tm, len(groups)),
        in_specs=[
            pl.BlockSpec(memory_space=pltpu.SMEM),
            pl.BlockSpec((tm, D), lambda i, j: (i, 0)),
            pl.BlockSpec((None, D, D), lambda i, j: (l, _pick(j, groups), 0)),
        ],
        out_specs=pl.BlockSpec((tm, D), lambda i, j: (i, j)),
        out_shape=jax.ShapeDtypeStruct((T, len(groups) * D), out_dtype),
        compiler_params=_params("parallel", "parallel"),
        name="proj",
    )(jnp.array(scales, F32), h, wt)


def _proj_t_kernel(scale_ref, h_ref, w_ref, o_ref, *, sigmoid):
    out = _dot_nt(w_ref[...], h_ref[...]) * scale_ref[pl.program_id(1)]
    o_ref[...] = (jax.nn.sigmoid(out) if sigmoid else out).astype(o_ref.dtype)


def _proj_t(h, wt, l, groups, scales, out_dtype, sigmoid=False):
    T, D = h.shape
    tm = min(TM_PROJ, T)
    return pl.pallas_call(
        functools.partial(_proj_t_kernel, sigmoid=sigmoid),
        grid=(T // tm, len(groups)),
        in_specs=[
            pl.BlockSpec(memory_space=pltpu.SMEM),
            pl.BlockSpec((tm, D), lambda i, j: (i, 0)),
            pl.BlockSpec((None, D, D), lambda i, j: (l, _pick(j, groups), 0)),
        ],
        out_specs=pl.BlockSpec((D, tm), lambda i, j: (j, i)),
        out_shape=jax.ShapeDtypeStruct((len(groups) * D, T), out_dtype),
        compiler_params=_params("parallel", "parallel"),
        name="proj_t",
    )(jnp.array(scales, F32), h, wt)


def _row_to_columns(r):
    L = r.shape[1]
    col = jnp.transpose(jnp.broadcast_to(r, (128, L)))
    return jnp.concatenate([col] * (L // 128), axis=1)


def _mlstm_kernel(k_ref, qT_ref, vT_ref, oT_ref, gates_ref, gain_ref, y_ref, c_ref, n_ref, m_ref):
    @pl.when(pl.program_id(1) == 0)
    def _():
        c_ref[...] = jnp.zeros_like(c_ref)
        n_ref[...] = jnp.zeros_like(n_ref)
        m_ref[...] = jnp.zeros_like(m_ref)

    L = k_ref.shape[0]
    dh = ML_HEAD_DIM
    heads = range(ML_HEADS)
    feat = lambda g: slice(g * dh, (g + 1) * dh)
    s_idx = lax.broadcasted_iota(jnp.int32, (L, L), 0)
    t_idx = lax.broadcasted_iota(jnp.int32, (L, L), 1)
    causal = s_idx <= t_idx

    k = [k_ref[:, feat(g)] for g in heads]
    qT = [qT_ref[feat(g), :] for g in heads]
    c_prev = [c_ref[g] for g in heads]
    n_prev = [n_ref[g] for g in heads]
    m_prev = [m_ref[g] for g in heads]
    kq = [_dot(k[g], qT[g]) for g in heads]
    cq = [_dot(c_prev[g].astype(BF16), qT[g]) for g in heads]
    nq = [_dot(n_prev[g].astype(BF16), qT[g])[0:1, :] for g in heads]


    gates = gates_ref[...]
    log_f = _log_sigmoid(gates)
    b_row, u_row, a_inter, s_w, den, num = [], [], [], [], [], []
    for g in heads:
        lf_col = _row_to_columns(log_f[ML_HEADS + g:ML_HEADS + g + 1, :])
        b_row.append(jnp.sum(jnp.where(causal, lf_col, 0.0), axis=0, keepdims=True))
        c_col = _row_to_columns(gates[g:g + 1, :] - b_row[g])
        c_max = jnp.max(jnp.where(causal, c_col, -jnp.inf), axis=0, keepdims=True)
        u_row.append(jnp.maximum(m_prev[g], c_max))
        w = jnp.where(causal, jnp.exp(c_col - u_row[g]), 0.0)
        s_w.append(kq[g] * w)
        a_inter.append(jnp.exp(m_prev[g] - u_row[g]))
        den.append(jnp.sum(s_w[g], axis=0, keepdims=True) + a_inter[g] * nq[g])
        num.append(_dot(vT_ref[feat(g), :], s_w[g].astype(BF16)))

    for g in heads:
        b_last = b_row[g][:, L - 1:L]
        g_row = b_last - b_row[g] + gates[g:g + 1, :]
        m_new = jnp.maximum(b_last + m_prev[g], jnp.max(g_row, axis=1, keepdims=True))
        decay = jnp.exp(b_last + m_prev[g] - m_new)
        w_state = jnp.exp(g_row - m_new)
        vw = (vT_ref[feat(g), :].astype(F32) * w_state).astype(BF16)
        c_ref[g] = decay * c_prev[g] + _dot(vw, k[g])
        w8 = jnp.broadcast_to(w_state, (8, L)).astype(BF16)
        n_ref[g] = decay * n_prev[g] + _dot(w8, k[g])
        m_ref[g] = m_new

    for g in heads:
        floor = jnp.exp(-(b_row[g] + u_row[g]))
        h = (num[g] + a_inter[g] * cq[g]) / jnp.maximum(jnp.abs(den[g]), floor)
        hg = oT_ref[feat(g), :] * h
        ms = jnp.mean(hg * hg, axis=0, keepdims=True)
        gain = jnp.concatenate([gain_ref[feat(g), :]] * (L // 128), axis=1)
        y_ref[feat(g), :] = (hg * lax.rsqrt(ms + EPS) * gain).astype(y_ref.dtype)


def _mlstm(pn, pt, pt_f, gates, gain, *, batch, seq, k_col, qT_row, vT_row, oT_row):
    T = batch * seq
    L = min(ML_CHUNK, seq)
    nc = seq // L
    dh = ML_HEAD_DIM
    W = ML_HEADS * dh
    tok = lambda b, c: b * nc + c
    return pl.pallas_call(
        _mlstm_kernel,
        grid=(batch, nc),
        in_specs=[
            pl.BlockSpec((L, W), lambda b, c: (tok(b, c), k_col)),
            pl.BlockSpec((W, L), lambda b, c: (qT_row, tok(b, c))),
            pl.BlockSpec((W, L), lambda b, c: (vT_row, tok(b, c))),
            pl.BlockSpec((W, L), lambda b, c: (oT_row, tok(b, c))),
            pl.BlockSpec((2 * ML_HEADS, L), lambda b, c: (0, tok(b, c))),
            pl.BlockSpec((W, 128), lambda b, c: (0, 0)),
        ],
        out_specs=pl.BlockSpec((W, L), lambda b, c: (0, tok(b, c))),
        out_shape=jax.ShapeDtypeStruct((W, T), BF16),
        scratch_shapes=[
            pltpu.VMEM((ML_HEADS, dh, dh), F32),
            pltpu.VMEM((ML_HEADS, 8, dh), F32),
            pltpu.VMEM((ML_HEADS, 1, 1), F32),
        ],
        compiler_params=_params("parallel", "arbitrary"),
        name="mlstm",
    )(pn, pt, pt, pt_f, gates, gain)


def _sb_kernel(qT_ref, k_ref, vT_ref, o_ref, acc_ref, *, blk, heads, qblocks):
    step = pl.program_id(2)
    d = SB_HEAD_DIM

    s_idx = lax.broadcasted_iota(jnp.int32, (blk, blk), 0)
    t_idx = lax.broadcasted_iota(jnp.int32, (blk, blk), 1)
    later = (t_idx > s_idx)
    from_here = jnp.where(t_idx >= s_idx, 1.0, 0.0).astype(BF16)

    row_head = lax.broadcasted_iota(jnp.int32, (2 * d, blk), 0) // d
    q_pads = []
    for qb in range(qblocks):
        q_pads.append([])
        for g in range(heads):
            q2 = qT_ref[(g // 2) * 2 * d:(g // 2 + 1) * 2 * d, qb * blk:(qb + 1) * blk]
            q_pads[qb].append(jnp.where(row_head == g % 2, q2, jnp.zeros_like(q2)))

    def neg_log2_1m(zs):
        neg_abs = pltpu.bitcast(pltpu.bitcast(zs, jnp.uint32) | jnp.uint32(0x80000000), F32)
        return jnp.maximum(zs, 0.0) + jnp.log(1.0 + jnp.exp2(neg_abs)) * LOG2E

    def visit(chains, rs, init_acc):
        starts = [pl.multiple_of(j * blk, blk) for _, j, _, _ in chains]
        z = [[_dot(k_ref[pl.ds(st, blk), (g // 2) * 2 * d:(g // 2 + 1) * 2 * d], q_pads[qb][g])
              for g in range(heads)] for st, (qb, _, _, _) in zip(starts, chains)]
        tail = []
        for c, (_, _, diagonal, _) in enumerate(chains):
            tail.append([])
            for g in range(heads):
                nl1 = neg_log2_1m(z[c][g])
                if diagonal:
                    nl1 = jnp.where(later, nl1, 0.0)
                tail[c].append(_dot(from_here, nl1.astype(BF16)))
        rs = {qb: list(r) for qb, r in rs.items()}
        r_in = []
        for c, (qb, _, _, r_gate) in enumerate(chains):
            r_in.append([rs[qb][g] if r_gate is None else jnp.where(r_gate, rs[qb][g], SB_ABSENT_LOG2)
                         for g in range(heads)])
            rs[qb] = [r_in[c][g] - tail[c][g][0:1, :] for g in range(heads)]
        out = {qb: (jnp.max(functools.reduce(jnp.maximum, r)) > SB_DEAD_LOG2, tuple(r))
               for qb, r in rs.items()}
        started = set()
        for c, (qb, _, diagonal, _) in enumerate(chains):
            for g in range(heads):
                a = jnp.exp2(z[c][g] - tail[c][g] + r_in[c][g])
                if diagonal:
                    a = jnp.where(later, a, 0.0)
                contrib = _dot(vT_ref[g * d:(g + 1) * d, pl.ds(starts[c], blk)], a.astype(BF16))
                if init_acc and qb not in started:
                    acc_ref[g * d:(g + 1) * d, qb * blk:(qb + 1) * blk] = contrib
                else:
                    acc_ref[g * d:(g + 1) * d, qb * blk:(qb + 1) * blk] += contrib
            started.add(qb)
        return out

    blocks = [step * qblocks + qb for qb in range(qblocks)]
    chains = []
    for qb, i in enumerate(blocks):
        chains += [(qb, i, True, None), (qb, jnp.maximum(i - 1, 0), False, i > 0)]
    zeros = tuple(jnp.zeros((1, blk), F32) for _ in range(heads))
    first = visit(chains, {qb: zeros for qb in range(qblocks)}, True)

    for qb, i in enumerate(blocks):
        def cond(c, i=i):
            n, live, _ = c
            return jnp.logical_and(n < i - 1, live)

        def body(c, qb=qb, i=i):
            n, _, rs = c
            return (n + 1,) + visit([(qb, i - 2 - n, False, None)], {qb: rs}, False)[qb]

        lax.while_loop(cond, body, (jnp.int32(0),) + first[qb])
    o_ref[...] = acc_ref[...].astype(o_ref.dtype)


def _stick_breaking(pn_b, pt, *, batch, seq, k_col, qT_row, vT_row):
    T = batch * seq
    blk = min(SB_BLOCK, seq)
    qblocks = min(SB_QBLOCKS, seq // blk)
    tq = qblocks * blk
    nq = seq // tq
    gd = SB_GROUP * SB_HEAD_DIM
    return pl.pallas_call(
        functools.partial(_sb_kernel, blk=blk, heads=SB_GROUP, qblocks=qblocks),
        grid=(batch, SB_HEADS // SB_GROUP, nq),
        in_specs=[
            pl.BlockSpec((gd, tq), lambda b, h, i: (qT_row + h, b * nq + i)),
            pl.BlockSpec((seq, gd), lambda b, h, i: (b, k_col + h)),
            pl.BlockSpec((gd, seq), lambda b, h, i: (vT_row + h, b)),
        ],
        out_specs=pl.BlockSpec((gd, tq), lambda b, h, i: (h, b * nq + i)),
        out_shape=jax.ShapeDtypeStruct((SB_HEADS * SB_HEAD_DIM, T), BF16),
        scratch_shapes=[pltpu.VMEM((gd, tq), F32)],
        compiler_params=_params("parallel", "parallel", "arbitrary"),
        name="stick_breaking",
    )(pt, pn_b, pt)


def _merge_kernel(x_ref, h_ref, hp_ref, wb_ref, wc_ref, wu_ref, wgml_ref, wgsc_ref, wgsb_ref,
                  ymlT_ref, ysbT_ref, cw_ref, wml_ref, wsc_ref, wsb_ref, wout_ref, o_ref, *,
                  tiles_per_seq):
    tm, D = x_ref.shape
    P = hp_ref.shape[0]
    first = (pl.program_id(0) % tiles_per_seq) == 0
    h = h_ref[...]
    h_ext = jnp.concatenate([hp_ref[...], h], axis=0)

    z = _dot_nt(h_ext, wc_ref[...]) * _dot_nt(h_ext, wu_ref[...])
    row = lax.broadcasted_iota(jnp.int32, z.shape, 0)
    z = jnp.where(jnp.logical_and(first, row < P), 0.0, z)
    cw = cw_ref[...]
    conv = (cw[0:1, :] * z[P - 2:P - 2 + tm, :] + cw[1:2, :] * z[P - 1:P - 1 + tm, :]
            + cw[2:3, :] * z[P:P + tm, :])
    y_sc = _dot_nt(h, wb_ref[...]) * conv

    merged = jax.nn.sigmoid(_dot_nt(h, wgml_ref[...])) * _dot_tn(ymlT_ref[...], wml_ref[...])
    merged += jax.nn.sigmoid(_dot_nt(h, wgsc_ref[...])) * _dot(y_sc.astype(BF16), wsc_ref[...])
    merged += jax.nn.sigmoid(_dot_nt(h, wgsb_ref[...])) * _dot_tn(ysbT_ref[...], wsb_ref[...])
    o_ref[...] = x_ref[...] + _dot(merged.astype(BF16), wout_ref[...])


def _merge(x, h, wt, l, y_mlT, y_sbT, conv_w, w_ml, w_sc, w_sb, w_out, *, seq):
    T, D = x.shape
    tm = min(TM_MERGE, seq)
    const = lambda i: (0, 0)
    resident = lambda shape: pl.BlockSpec(shape, const, pipeline_mode=pl.Buffered(1))
    group = lambda g: pl.BlockSpec((None, D, D), lambda i: (l, g, 0), pipeline_mode=pl.Buffered(1))
    layer = lambda shape: pl.BlockSpec((None,) + shape, lambda i: (l, 0, 0), pipeline_mode=pl.Buffered(1))
    return pl.pallas_call(
        functools.partial(_merge_kernel, tiles_per_seq=seq // tm),
        grid=(T // tm,),
        in_specs=[
            pl.BlockSpec((tm, D), lambda i: (i, 0)),
            pl.BlockSpec((tm, D), lambda i: (i, 0)),
            pl.BlockSpec((PREV_ROWS, D), lambda i: (jnp.maximum(i * (tm // PREV_ROWS) - 1, 0), 0)),
            group(G_B), group(G_C), group(G_U), group(G_GML), group(G_GSC), group(G_GSB),
            pl.BlockSpec((D, tm), lambda i: (0, i)),
            pl.BlockSpec((D, tm), lambda i: (0, i)),
            resident((3, D)),
            layer((D, D)), layer((D, D)), layer((D, D)), layer((D, D)),
        ],
        out_specs=pl.BlockSpec((tm, D), lambda i: (i, 0)),
        out_shape=jax.ShapeDtypeStruct((T, D), F32),
        compiler_params=_params("parallel"),
        name="merge",
    )(x, h, h, wt, wt, wt, wt, wt, wt, y_mlT, y_sbT, conv_w, w_ml, w_sc, w_sb, w_out)


def _mlp_kernel(x_ref, g_ref, wup_ref, wdown_ref, gn_ref, wg_hi_ref, wg_lo_ref, bg_ref, *out_refs,
                ff_chunk, last):
    x = x_ref[...]
    h = _rmsnorm_rows(x, g_ref[...]).astype(BF16)
    acc = x
    for c in range(wup_ref.shape[1] // ff_chunk):
        cols = slice(c * ff_chunk, (c + 1) * ff_chunk)
        up = jnp.maximum(_dot(h, wup_ref[:, cols]), 0.0)
        acc = acc + _dot((up * up).astype(BF16), wdown_ref[cols, :])
    if last:
        out_refs[0][...] = _rmsnorm_rows(acc, gn_ref[...])
    else:
        out_refs[0][...] = acc
        out_refs[1][...], out_refs[2][...] = _prenorm_outputs(acc, gn_ref[...], wg_hi_ref[...],
                                                              wg_lo_ref[...], bg_ref[...])


def _mlp(x, g, w_up, w_down, l, g_next, wg_hi, wg_lo, bg, *, last):
    T, D = x.shape
    F = w_up.shape[2]
    G = wg_hi.shape[0]
    tm = min(TM_MLP, T)
    const = lambda i: (0, 0)
    resident = lambda shape: pl.BlockSpec(shape, const, pipeline_mode=pl.Buffered(1))
    layer = lambda shape: pl.BlockSpec((None,) + shape, lambda i: (l, 0, 0), pipeline_mode=pl.Buffered(1))
    out_specs = [pl.BlockSpec((tm, D), lambda i: (i, 0))]
    out_shape = [jax.ShapeDtypeStruct((T, D), F32)]
    if not last:
        out_specs += [pl.BlockSpec((tm, D), lambda i: (i, 0)), pl.BlockSpec((G, tm), lambda i: (0, i))]
        out_shape += [jax.ShapeDtypeStruct((T, D), BF16), jax.ShapeDtypeStruct((G, T), F32)]
    return pl.pallas_call(
        functools.partial(_mlp_kernel, ff_chunk=1024, last=last),
        grid=(T // tm,),
        in_specs=[
            pl.BlockSpec((tm, D), lambda i: (i, 0)),
            resident((1, D)),
            layer((D, F)),
            layer((F, D)),
            resident((1, D)),
            resident((G, D)), resident((G, D)), resident((G, 1)),
        ],
        out_specs=out_specs,
        out_shape=out_shape,
        compiler_params=_params("parallel"),
        name="mlp",
    )(x, g, w_up, w_down, g_next, wg_hi, wg_lo, bg)


def _gate_params(w_in_t, l, b_if):
    off = 4 * ML_HEADS * ML_HEAD_DIM
    wg_hi, wg_lo = _split_bf16(w_in_t[l, off:off + 2 * ML_HEADS, :])
    return wg_hi, wg_lo, b_if.reshape(2 * ML_HEADS, 1)


def _layer(x, h, gates, p, nxt, *, batch, seq):
    D = D_MODEL
    ml_w = ML_HEADS * ML_HEAD_DIM
    wt, l = p["w_in"]
    pn = _proj(h, wt, l, (G_MK, G_SK), (ML_HEAD_DIM ** -0.5, 1.0), BF16)
    pt = _proj_t(h, wt, l, (G_MQ, G_MV, G_SQ, G_SV), (1.0, 1.0, SB_HEAD_DIM ** -0.5 * LOG2E, 1.0), BF16)
    pt_f = _proj_t(h, wt, l, (G_MO,), (1.0,), F32, sigmoid=True)

    gain = jnp.broadcast_to(p["ml_norm_g"].reshape(ml_w, 1), (ml_w, 128))
    y_mlT = _mlstm(pn, pt, pt_f, gates, gain, batch=batch, seq=seq,
                   k_col=0, qT_row=0, vT_row=1, oT_row=0)
    sb_unit = SB_GROUP * SB_HEAD_DIM
    y_sbT = _stick_breaking(pn, pt, batch=batch, seq=seq, k_col=ml_w // sb_unit,
                            qT_row=2 * ml_w // sb_unit, vT_row=(2 * ml_w + D) // sb_unit)
    x = _merge(x, h, wt, l, y_mlT, y_sbT, p["conv_w"],
               p["w_ml_proj"], p["w_sc_proj"], p["w_sb_proj"], p["w_out"], seq=seq)
    return _mlp(x, p["norm_mlp_g"].reshape(1, D), p["w_up"], p["w_down"], l,
                nxt["g"].reshape(1, D), nxt["wg_hi"], nxt["wg_lo"], nxt["bg"], last=nxt["last"])


def kernel(x, norm_mix_g, w_in, b_if, ml_norm_g, conv_w, w_ml_proj, w_sc_proj, w_sb_proj, w_out,
           norm_mlp_g, w_up, w_down, norm_final_g):
    batch, seq, D = x.shape
    depth = w_in.shape[0]
    xt = x.reshape(batch * seq, D)
    w_in_t = jnp.swapaxes(w_in, 1, 2)
    gate_params = [_gate_params(w_in_t, l, b_if[l]) for l in range(depth)]
    h, gates = _prenorm(xt, norm_mix_g[0].reshape(1, D), *gate_params[0])
    wt = _regroup_w_in_t(w_in_t)
    stacked_bf16 = dict(w_ml_proj=w_ml_proj.astype(BF16), w_sc_proj=w_sc_proj.astype(BF16),
                        w_sb_proj=w_sb_proj.astype(BF16), w_out=w_out.astype(BF16),
                        w_up=w_up.astype(BF16), w_down=w_down.astype(BF16))
    for l in range(depth):
        p = dict(w_in=(wt, l), ml_norm_g=ml_norm_g[l], conv_w=conv_w[l], norm_mlp_g=norm_mlp_g[l],
                 **stacked_bf16)
        last = l == depth - 1
        nl = l if last else l + 1
        nxt = dict(g=norm_final_g if last else norm_mix_g[nl], wg_hi=gate_params[nl][0],
                   wg_lo=gate_params[nl][1], bg=gate_params[nl][2], last=last)
        outs = _layer(xt, h, gates, p, nxt, batch=batch, seq=seq)
        if last:
            xt = outs[0]
        else:
            xt, h, gates = outs
    return xt.reshape(batch, seq, D)
```

```python
import functools

import jax
import jax.numpy as jnp
from jax import lax
from jax.experimental import pallas as pl
from jax.experimental.pallas import tpu as pltpu

D_MODEL = 1024
ML_HEADS = 4
ML_HEAD_DIM = 256
SB_HEADS = 16
SB_HEAD_DIM = 64
D_FF = 4 * D_MODEL
EPS = 1e-6

VMEM_LIMIT_BYTES = 56 * 1024 * 1024

ML_CHUNK = 256
SB_BLOCK = 256
SB_GROUP = 4
SB_QBLOCKS = 2
SB_STAGE_LAG = 2
LOG2E = 1.4426950408889634
SB_DEAD_LOG2 = -152.0
SB_ABSENT_LOG2 = -1e30
TM_PROJ = 2048
TM_MERGE = 512
TM_MLP = 512
PREV_ROWS = 16

BF16 = jnp.bfloat16
F32 = jnp.float32


def _params(*sem):
    return pltpu.CompilerParams(dimension_semantics=sem, vmem_limit_bytes=VMEM_LIMIT_BYTES)


def _dot(a, b):
    return jnp.dot(a, b, preferred_element_type=F32)


def _dot_nt(a, b):
    return lax.dot_general(a, b, (((1,), (1,)), ((), ())), preferred_element_type=F32)


def _dot_tn(a, b):
    return lax.dot_general(a, b, (((0,), (0,)), ((), ())), preferred_element_type=F32)


def _rmsnorm_rows(x, g):
    ms = jnp.mean(x * x, axis=-1, keepdims=True)
    return x * lax.rsqrt(ms + EPS) * g


def _log_sigmoid(x):
    return jnp.minimum(x, 0.0) - jnp.log(1.0 + jnp.exp(-jnp.abs(x)))


def _split_bf16(x):
    hi = x.astype(BF16)
    lo = (x - hi.astype(F32)).astype(BF16)
    return hi, lo


def _prenorm_outputs(x, g, wg_hi, wg_lo, bg):
    h_hi, h_lo = _split_bf16(_rmsnorm_rows(x, g))
    gates = _dot_nt(wg_hi, h_hi) + (_dot_nt(wg_hi, h_lo) + _dot_nt(wg_lo, h_hi)) + bg
    return h_hi, gates


def _prenorm_kernel(x_ref, g_ref, wg_hi_ref, wg_lo_ref, bg_ref, h_ref, gates_ref):
    h_ref[...], gates_ref[...] = _prenorm_outputs(x_ref[...], g_ref[...], wg_hi_ref[...],
                                                  wg_lo_ref[...], bg_ref[...])


def _prenorm(x, g, wg_hi, wg_lo, bg):
    T, D = x.shape
    G = wg_hi.shape[0]
    tm = min(TM_PROJ, T)
    const = lambda i: (0, 0)
    return pl.pallas_call(
        _prenorm_kernel,
        grid=(T // tm,),
        in_specs=[
            pl.BlockSpec((tm, D), lambda i: (i, 0)),
            pl.BlockSpec((1, D), const),
            pl.BlockSpec((G, D), const),
            pl.BlockSpec((G, D), const),
            pl.BlockSpec((G, 1), const),
        ],
        out_specs=[pl.BlockSpec((tm, D), lambda i: (i, 0)), pl.BlockSpec((G, tm), lambda i: (0, i))],
        out_shape=[jax.ShapeDtypeStruct((T, D), BF16), jax.ShapeDtypeStruct((G, T), F32)],
        compiler_params=_params("parallel"),
        name="prenorm",
    )(x, g, wg_hi, wg_lo, bg)


N_GROUPS = 13
ALIGNED_GROUPS = 4
GATE_ROWS = 2 * ML_HEADS
(G_MQ, G_MK, G_MV, G_MO, G_B, G_C, G_U, G_SQ, G_SK, G_SV, G_GML, G_GSC, G_GSB) = range(N_GROUPS)


def _regroup_kernel(a_ref, b_ref, o_ref):
    k = pl.program_id(1)

    @pl.when(k < ALIGNED_GROUPS)
    def _():
        o_ref[...] = a_ref[...].astype(o_ref.dtype)

    @pl.when(k >= ALIGNED_GROUPS)
    def _():
        window = jnp.concatenate([a_ref[GATE_ROWS:, :], b_ref[...]], axis=0)
        o_ref[...] = window.astype(o_ref.dtype)


def _regroup_w_in_t(w_in_t):
    depth, _, D = w_in_t.shape
    return pl.pallas_call(
        _regroup_kernel,
        grid=(depth, N_GROUPS),
        in_specs=[
            pl.BlockSpec((None, D, D), lambda l, k: (l, k, 0)),
            pl.BlockSpec((None, GATE_ROWS, D), lambda l, k: (l, (k + 1) * (D // GATE_ROWS), 0)),
        ],
        out_specs=pl.BlockSpec((None, D, D), lambda l, k: (l, k, 0)),
        out_shape=jax.ShapeDtypeStruct((depth, N_GROUPS * D, D), BF16),
        compiler_params=_params("parallel", "parallel"),
        name="regroup_w_in",
    )(w_in_t, w_in_t)


def _pick(j, values):
    return sum(jnp.where(j == n, v, 0) for n, v in enumerate(values))


def _proj_kernel(scale_ref, h_ref, w_ref, o_ref):
    scale = scale_ref[pl.program_id(1)]
    o_ref[...] = (_dot_nt(h_ref[...], w_ref[...]) * scale).astype(o_ref.dtype)


def _proj(h, wt, l, groups, scales, out_dtype):
    T, D = h.shape
    tm = min(TM_PROJ, T)
    return pl.pallas_call(
        _proj_kernel,
        grid=(T // tm, len(groups)),
        in_specs=[
            pl.BlockSpec(memory_space=pltpu.SMEM),
            pl.BlockSpec((tm, D), lambda i, j: (i, 0)),
            pl.BlockSpec((None, D, D), lambda i, j: (l, _pick(j, groups), 0)),
        ],
        out_specs=pl.BlockSpec((tm, D), lambda i, j: (i, j)),
        out_shape=jax.ShapeDtypeStruct((T, len(groups) * D), out_dtype),
        compiler_params=_params("parallel", "parallel"),
        name="proj",
    )(jnp.array(scales, F32), h, wt)


def _proj_t_kernel(scale_ref, h_ref, w_ref, o_ref, *, sigmoid):
    out = _dot_nt(w_ref[...], h_ref[...]) * scale_ref[pl.program_id(1)]
    o_ref[...] = (jax.nn.sigmoid(out) if sigmoid else out).astype(o_ref.dtype)


def _proj_t(h, wt, l, groups, scales, out_dtype, sigmoid=False):
    T, D = h.shape
    tm = min(TM_PROJ, T)
    return pl.pallas_call(
        functools.partial(_proj_t_kernel, sigmoid=sigmoid),
        grid=(T // tm, len(groups)),
        in_specs=[
            pl.BlockSpec(memory_space=pltpu.SMEM),
            pl.BlockSpec((tm, D), lambda i, j: (i, 0)),
            pl.BlockSpec((None, D, D), lambda i, j: (l, _pick(j, groups), 0)),
        ],
        out_specs=pl.BlockSpec((D, tm), lambda i, j: (j, i)),
        out_shape=jax.ShapeDtypeStruct((len(groups) * D, T), out_dtype),
        compiler_params=_params("parallel", "parallel"),
        name="proj_t",
    )(jnp.array(scales, F32), h, wt)


def _row_to_columns(r):
    L = r.shape[1]
    col = jnp.transpose(jnp.broadcast_to(r, (128, L)))
    return jnp.concatenate([col] * (L // 128), axis=1)


def _mlstm_kernel(k_ref, qT_ref, vT_ref, oT_ref, gates_ref, gain_ref, y_ref, c_ref, n_ref, m_ref):
    @pl.when(pl.program_id(1) == 0)
    def _():
        c_ref[...] = jnp.zeros_like(c_ref)
        n_ref[...] = jnp.zeros_like(n_ref)
        m_ref[...] = jnp.zeros_like(m_ref)

    L = k_ref.shape[0]
    dh = ML_HEAD_DIM
    heads = range(ML_HEADS)
    feat = lambda g: slice(g * dh, (g + 1) * dh)
    s_idx = lax.broadcasted_iota(jnp.int32, (L, L), 0)
    t_idx = lax.broadcasted_iota(jnp.int32, (L, L), 1)
    causal = s_idx <= t_idx

    k = [k_ref[:, feat(g)] for g in heads]
    qT = [qT_ref[feat(g), :] for g in heads]
    c_prev = [c_ref[g] for g in heads]
    n_prev = [n_ref[g] for g in heads]
    m_prev = [m_ref[g] for g in heads]
    kq = [_dot(k[g], qT[g]) for g in heads]
    cq = [_dot(c_prev[g].astype(BF16), qT[g]) for g in heads]
    nq = [_dot(n_prev[g].astype(BF16), qT[g])[0:1, :] for g in heads]


    gates = gates_ref[...]
    log_f = _log_sigmoid(gates)
    b_row, u_row, a_inter, s_w, den, num = [], [], [], [], [], []
    for g in heads:
        lf_col = _row_to_columns(log_f[ML_HEADS + g:ML_HEADS + g + 1, :])
        b_row.append(jnp.sum(jnp.where(causal, lf_col, 0.0), axis=0, keepdims=True))
        c_col = _row_to_columns(gates[g:g + 1, :] - b_row[g])
        c_max = jnp.max(jnp.where(causal, c_col, -jnp.inf), axis=0, keepdims=True)
        u_row.append(jnp.maximum(m_prev[g], c_max))
        w = jnp.where(causal, jnp.exp(c_col - u_row[g]), 0.0)
        s_w.append(kq[g] * w)
        a_inter.append(jnp.exp(m_prev[g] - u_row[g]))
        den.append(jnp.sum(s_w[g], axis=0, keepdims=True) + a_inter[g] * nq[g])
        num.append(_dot(vT_ref[feat(g), :], s_w[g].astype(BF16)))

    for g in heads:
        b_last = b_row[g][:, L - 1:L]
        g_row = b_last - b_row[g] + gates[g:g + 1, :]
        m_new = jnp.maximum(b_last + m_prev[g], jnp.max(g_row, axis=1, keepdims=True))
        decay = jnp.exp(b_last + m_prev[g] - m_new)
        w_state = jnp.exp(g_row - m_new)
        vw = (vT_ref[feat(g), :].astype(F32) * w_state).astype(BF16)
        c_ref[g] = decay * c_prev[g] + _dot(vw, k[g])
        w8 = jnp.broadcast_to(w_state, (8, L)).astype(BF16)
        n_ref[g] = decay * n_prev[g] + _dot(w8, k[g])
        m_ref[g] = m_new

    for g in heads:
        floor = jnp.exp(-(b_row[g] + u_row[g]))
        h = (num[g] + a_inter[g] * cq[g]) / jnp.maximum(jnp.abs(den[g]), floor)
        hg = oT_ref[feat(g), :] * h
        ms = jnp.mean(hg * hg, axis=0, keepdims=True)
        gain = jnp.concatenate([gain_ref[feat(g), :]] * (L // 128), axis=1)
        y_ref[feat(g), :] = (hg * lax.rsqrt(ms + EPS) * gain).astype(y_ref.dtype)


def _mlstm(pn, pt, pt_f, gates, gain, *, batch, seq, k_col, qT_row, vT_row, oT_row):
    T = batch * seq
    L = min(ML_CHUNK, seq)
    nc = seq // L
    dh = ML_HEAD_DIM
    W = ML_HEADS * dh
    tok = lambda b, c: b * nc + c
    return pl.pallas_call(
        _mlstm_kernel,
        grid=(batch, nc),
        in_specs=[
            pl.BlockSpec((L, W), lambda b, c: (tok(b, c), k_col)),
            pl.BlockSpec((W, L), lambda b, c: (qT_row, tok(b, c))),
            pl.BlockSpec((W, L), lambda b, c: (vT_row, tok(b, c))),
            pl.BlockSpec((W, L), lambda b, c: (oT_row, tok(b, c))),
            pl.BlockSpec((2 * ML_HEADS, L), lambda b, c: (0, tok(b, c))),
            pl.BlockSpec((W, 128), lambda b, c: (0, 0)),
        ],
        out_specs=pl.BlockSpec((W, L), lambda b, c: (0, tok(b, c))),
        out_shape=jax.ShapeDtypeStruct((W, T), BF16),
        scratch_shapes=[
            pltpu.VMEM((ML_HEADS, dh, dh), F32),
            pltpu.VMEM((ML_HEADS, 8, dh), F32),
            pltpu.VMEM((ML_HEADS, 1, 1), F32),
        ],
        compiler_params=_params("parallel", "arbitrary"),
        name="mlstm",
    )(pn, pt, pt, pt_f, gates, gain)


def _sb_kernel(qT_ref, k_ref, vT_ref, o_ref, acc_ref, *, blk, heads, qblocks):
    step = pl.program_id(2)
    d = SB_HEAD_DIM

    s_idx = lax.broadcasted_iota(jnp.int32, (blk, blk), 0)
    t_idx = lax.broadcasted_iota(jnp.int32, (blk, blk), 1)
    later = (t_idx > s_idx)
    from_here = jnp.where(t_idx >= s_idx, 1.0, 0.0).astype(BF16)

    row_head = lax.broadcasted_iota(jnp.int32, (2 * d, blk), 0) // d
    q_pads = []
    for qb in range(qblocks):
        q_pads.append([])
        for g in range(heads):
            q2 = qT_ref[(g // 2) * 2 * d:(g // 2 + 1) * 2 * d, qb * blk:(qb + 1) * blk]
            q_pads[qb].append(jnp.where(row_head == g % 2, q2, jnp.zeros_like(q2)))

    def neg_log2_1m(zs):
        neg_abs = pltpu.bitcast(pltpu.bitcast(zs, jnp.uint32) | jnp.uint32(0x80000000), F32)
        return jnp.maximum(zs, 0.0) + jnp.log(1.0 + jnp.exp2(neg_abs)) * LOG2E

    def visit(chains, rs, init_acc):
        starts = [pl.multiple_of(j * blk, blk) for _, j, _, _ in chains]
        units = [(c, g) for c in range(len(chains)) for g in range(heads)]
        rs = {qb: list(r) for qb, r in rs.items()}
        z, tail, started = {}, {}, set()

        def scores(c, g):
            qb = chains[c][0]
            z[c, g] = _dot(k_ref[pl.ds(starts[c], blk), (g // 2) * 2 * d:(g // 2 + 1) * 2 * d],
                           q_pads[qb][g])

        def tails(c, g):
            nl1 = neg_log2_1m(z[c, g])
            if chains[c][2]:
                nl1 = jnp.where(later, nl1, 0.0)
            tail[c, g] = _dot(from_here, nl1.astype(BF16))

        def weights(c, g):
            qb, _, diagonal, r_gate = chains[c]
            r_in = rs[qb][g] if r_gate is None else jnp.where(r_gate, rs[qb][g], SB_ABSENT_LOG2)
            rs[qb][g] = r_in - tail[c, g][0:1, :]
            a = jnp.exp2(z.pop((c, g)) - tail.pop((c, g)) + r_in)
            if diagonal:
                a = jnp.where(later, a, 0.0)
            contrib = _dot(vT_ref[g * d:(g + 1) * d, pl.ds(starts[c], blk)], a.astype(BF16))
            if init_acc and (qb, g) not in started:
                acc_ref[g * d:(g + 1) * d, qb * blk:(qb + 1) * blk] = contrib
            else:
                acc_ref[g * d:(g + 1) * d, qb * blk:(qb + 1) * blk] += contrib
            started.add((qb, g))

        stages = (scores, tails, weights)
        lag = SB_STAGE_LAG
        for n in range(len(units) + lag * (len(stages) - 1)):
            for k, stage in enumerate(stages):
                if 0 <= n - lag * k < len(units):
                    stage(*units[n - lag * k])
        return {qb: (jnp.max(functools.reduce(jnp.maximum, r)) > SB_DEAD_LOG2, tuple(r))
                for qb, r in rs.items()}

    blocks = [step * qblocks + qb for qb in range(qblocks)]
    chains = []
    for qb, i in enumerate(blocks):
        chains += [(qb, i, True, None), (qb, jnp.maximum(i - 1, 0), False, i > 0)]
    zeros = tuple(jnp.zeros((1, blk), F32) for _ in range(heads))
    first = visit(chains, {qb: zeros for qb in range(qblocks)}, True)

    for qb, i in enumerate(blocks):
        def cond(c, i=i):
            n, live, _ = c
            return jnp.logical_and(n < i - 1, live)

        def body(c, qb=qb, i=i):
            n, _, rs = c
            return (n + 1,) + visit([(qb, i - 2 - n, False, None)], {qb: rs}, False)[qb]

        lax.while_loop(cond, body, (jnp.int32(0),) + first[qb])
    o_ref[...] = acc_ref[...].astype(o_ref.dtype)


def _stick_breaking(pn_b, pt, *, batch, seq, k_col, qT_row, vT_row):
    T = batch * seq
    blk = min(SB_BLOCK, seq)
    qblocks = min(SB_QBLOCKS, seq // blk)
    tq = qblocks * blk
    nq = seq // tq
    gd = SB_GROUP * SB_HEAD_DIM
    return pl.pallas_call(
        functools.partial(_sb_kernel, blk=blk, heads=SB_GROUP, qblocks=qblocks),
        grid=(batch, SB_HEADS // SB_GROUP, nq),
        in_specs=[
            pl.BlockSpec((gd, tq), lambda b, h, i: (qT_row + h, b * nq + i)),
            pl.BlockSpec((seq, gd), lambda b, h, i: (b, k_col + h)),
            pl.BlockSpec((gd, seq), lambda b, h, i: (vT_row + h, b)),
        ],
        out_specs=pl.BlockSpec((gd, tq), lambda b, h, i: (h, b * nq + i)),
        out_shape=jax.ShapeDtypeStruct((SB_HEADS * SB_HEAD_DIM, T), BF16),
        scratch_shapes=[pltpu.VMEM((gd, tq), F32)],
        compiler_params=_params("parallel", "parallel", "arbitrary"),
        name="stick_breaking",
    )(pt, pn_b, pt)


def _merge_kernel(x_ref, h_ref, hp_ref, wb_ref, wc_ref, wu_ref, wgml_ref, wgsc_ref, wgsb_ref,
                  ymlT_ref, ysbT_ref, cw_ref, wml_ref, wsc_ref, wsb_ref, wout_ref, o_ref, *,
                  tiles_per_seq):
    tm, D = x_ref.shape
    P = hp_ref.shape[0]
    first = (pl.program_id(0) % tiles_per_seq) == 0
    h = h_ref[...]
    h_ext = jnp.concatenate([hp_ref[...], h], axis=0)

    z = _dot_nt(h_ext, wc_ref[...]) * _dot_nt(h_ext, wu_ref[...])
    row = lax.broadcasted_iota(jnp.int32, z.shape, 0)
    z = jnp.where(jnp.logical_and(first, row < P), 0.0, z)
    cw = cw_ref[...]
    conv = (cw[0:1, :] * z[P - 2:P - 2 + tm, :] + cw[1:2, :] * z[P - 1:P - 1 + tm, :]
            + cw[2:3, :] * z[P:P + tm, :])
    y_sc = _dot_nt(h, wb_ref[...]) * conv

    merged = jax.nn.sigmoid(_dot_nt(h, wgml_ref[...])) * _dot_tn(ymlT_ref[...], wml_ref[...])
    merged += jax.nn.sigmoid(_dot_nt(h, wgsc_ref[...])) * _dot(y_sc.astype(BF16), wsc_ref[...])
    merged += jax.nn.sigmoid(_dot_nt(h, wgsb_ref[...])) * _dot_tn(ysbT_ref[...], wsb_ref[...])
    o_ref[...] = x_ref[...] + _dot(merged.astype(BF16), wout_ref[...])


def _merge(x, h, wt, l, y_mlT, y_sbT, conv_w, w_ml, w_sc, w_sb, w_out, *, seq):
    T, D = x.shape
    tm = min(TM_MERGE, seq)
    const = lambda i: (0, 0)
    resident = lambda shape: pl.BlockSpec(shape, const, pipeline_mode=pl.Buffered(1))
    group = lambda g: pl.BlockSpec((None, D, D), lambda i: (l, g, 0), pipeline_mode=pl.Buffered(1))
    layer = lambda shape: pl.BlockSpec((None,) + shape, lambda i: (l, 0, 0), pipeline_mode=pl.Buffered(1))
    return pl.pallas_call(
        functools.partial(_merge_kernel, tiles_per_seq=seq // tm),
        grid=(T // tm,),
        in_specs=[
            pl.BlockSpec((tm, D), lambda i: (i, 0)),
            pl.BlockSpec((tm, D), lambda i: (i, 0)),
            pl.BlockSpec((PREV_ROWS, D), lambda i: (jnp.maximum(i * (tm // PREV_ROWS) - 1, 0), 0)),
            group(G_B), group(G_C), group(G_U), group(G_GML), group(G_GSC), group(G_GSB),
            pl.BlockSpec((D, tm), lambda i: (0, i)),
            pl.BlockSpec((D, tm), lambda i: (0, i)),
            resident((3, D)),
            layer((D, D)), layer((D, D)), layer((D, D)), layer((D, D)),
        ],
        out_specs=pl.BlockSpec((tm, D), lambda i: (i, 0)),
        out_shape=jax.ShapeDtypeStruct((T, D), F32),
        compiler_params=_params("parallel"),
        name="merge",
    )(x, h, h, wt, wt, wt, wt, wt, wt, y_mlT, y_sbT, conv_w, w_ml, w_sc, w_sb, w_out)


def _mlp_kernel(x_ref, g_ref, wup_ref, wdown_ref, gn_ref, wg_hi_ref, wg_lo_ref, bg_ref, *out_refs,
                ff_chunk, last):
    x = x_ref[...]
    h = _rmsnorm_rows(x, g_ref[...]).astype(BF16)
    acc = x
    for c in range(wup_ref.shape[1] // ff_chunk):
        cols = slice(c * ff_chunk, (c + 1) * ff_chunk)
        up = jnp.maximum(_dot(h, wup_ref[:, cols]), 0.0)
        acc = acc + _dot((up * up).astype(BF16), wdown_ref[cols, :])
    if last:
        out_refs[0][...] = _rmsnorm_rows(acc, gn_ref[...])
    else:
        out_refs[0][...] = acc
        out_refs[1][...], out_refs[2][...] = _prenorm_outputs(acc, gn_ref[...], wg_hi_ref[...],
                                                              wg_lo_ref[...], bg_ref[...])


def _mlp(x, g, w_up, w_down, l, g_next, wg_hi, wg_lo, bg, *, last):
    T, D = x.shape
    F = w_up.shape[2]
    G = wg_hi.shape[0]
    tm = min(TM_MLP, T)
    const = lambda i: (0, 0)
    resident = lambda shape: pl.BlockSpec(shape, const, pipeline_mode=pl.Buffered(1))
    layer = lambda shape: pl.BlockSpec((None,) + shape, lambda i: (l, 0, 0), pipeline_mode=pl.Buffered(1))
    out_specs = [pl.BlockSpec((tm, D), lambda i: (i, 0))]
    out_shape = [jax.ShapeDtypeStruct((T, D), F32)]
    if not last:
        out_specs += [pl.BlockSpec((tm, D), lambda i: (i, 0)), pl.BlockSpec((G, tm), lambda i: (0, i))]
        out_shape += [jax.ShapeDtypeStruct((T, D), BF16), jax.ShapeDtypeStruct((G, T), F32)]
    return pl.pallas_call(
        functools.partial(_mlp_kernel, ff_chunk=1024, last=last),
        grid=(T // tm,),
        in_specs=[
            pl.BlockSpec((tm, D), lambda i: (i, 0)),
            resident((1, D)),
            layer((D, F)),
            layer((F, D)),
            resident((1, D)),
            resident((G, D)), resident((G, D)), resident((G, 1)),
        ],
        out_specs=out_specs,
        out_shape=out_shape,
        compiler_params=_params("parallel"),
        name="mlp",
    )(x, g, w_up, w_down, g_next, wg_hi, wg_lo, bg)


def _gate_params(w_in_t, l, b_if):
    off = 4 * ML_HEADS * ML_HEAD_DIM
    wg_hi, wg_lo = _split_bf16(w_in_t[l, off:off + 2 * ML_HEADS, :])
    return wg_hi, wg_lo, b_if.reshape(2 * ML_HEADS, 1)


def _layer(x, h, gates, p, nxt, *, batch, seq):
    D = D_MODEL
    ml_w = ML_HEADS * ML_HEAD_DIM
    wt, l = p["w_in"]
    pn = _proj(h, wt, l, (G_MK, G_SK), (ML_HEAD_DIM ** -0.5, 1.0), BF16)
    pt = _proj_t(h, wt, l, (G_MQ, G_MV, G_SQ, G_SV), (1.0, 1.0, SB_HEAD_DIM ** -0.5 * LOG2E, 1.0), BF16)
    pt_f = _proj_t(h, wt, l, (G_MO,), (1.0,), F32, sigmoid=True)

    gain = jnp.broadcast_to(p["ml_norm_g"].reshape(ml_w, 1), (ml_w, 128))
    y_mlT = _mlstm(pn, pt, pt_f, gates, gain, batch=batch, seq=seq,
                   k_col=0, qT_row=0, vT_row=1, oT_row=0)
    sb_unit = SB_GROUP * SB_HEAD_DIM
    y_sbT = _stick_breaking(pn, pt, batch=batch, seq=seq, k_col=ml_w // sb_unit,
                            qT_row=2 * ml_w // sb_unit, vT_row=(2 * ml_w + D) // sb_unit)
    x = _merge(x, h, wt, l, y_mlT, y_sbT, p["conv_w"],
               p["w_ml_proj"], p["w_sc_proj"], p["w_sb_proj"], p["w_out"], seq=seq)
    return _mlp(x, p["norm_mlp_g"].reshape(1, D), p["w_up"], p["w_down"], l,
                nxt["g"].reshape(1, D), nxt["wg_hi"], nxt["wg_lo"], nxt["bg"], last=nxt["last"])


def kernel(x, norm_mix_g, w_in, b_if, ml_norm_g, conv_w, w_ml_proj, w_sc_proj, w_sb_proj, w_out,
           norm_mlp_g, w_up, w_down, norm_final_g):
    batch, seq, D = x.shape
    depth = w_in.shape[0]
    xt = x.reshape(batch * seq, D)
    w_in_t = jnp.swapaxes(w_in, 1, 2)
    gate_params = [_gate_params(w_in_t, l, b_if[l]) for l in range(depth)]
    h, gates = _prenorm(xt, norm_mix_g[0].reshape(1, D), *gate_params[0])
    wt = _regroup_w_in_t(w_in_t)
    stacked_bf16 = dict(w_ml_proj=w_ml_proj.astype(BF16), w_sc_proj=w_sc_proj.astype(BF16),
                        w_sb_proj=w_sb_proj.astype(BF16), w_out=w_out.astype(BF16),
                        w_up=w_up.astype(BF16), w_down=w_down.astype(BF16))
    for l in range(depth):
        p = dict(w_in=(wt, l), ml_norm_g=ml_norm_g[l], conv_w=conv_w[l], norm_mlp_g=norm_mlp_g[l],
                 **stacked_bf16)
        last = l == depth - 1
        nl = l if last else l + 1
        nxt = dict(g=norm_final_g if last else norm_mix_g[nl], wg_hi=gate_params[nl][0],
                   wg_lo=gate_params[nl][1], bg=gate_params[nl][2], last=last)
        outs = _layer(xt, h, gates, p, nxt, batch=batch, seq=seq)
        if last:
            xt = outs[0]
        else:
            xt, h, gates = outs
    return xt.reshape(batch, seq, D)
```

```python
import functools

import jax
import jax.numpy as jnp
from jax import lax
from jax.experimental import pallas as pl
from jax.experimental.pallas import tpu as pltpu

D_MODEL = 1024
ML_HEADS = 4
ML_HEAD_DIM = 256
SB_HEADS = 16
SB_HEAD_DIM = 64
D_FF = 4 * D_MODEL
EPS = 1e-6

VMEM_LIMIT_BYTES = 56 * 1024 * 1024

ML_CHUNK = 256
SB_BLOCK = 256
SB_GROUP = 4
SB_QBLOCKS = 2
SB_STAGE_LAG = 2
LOG2E = 1.4426950408889634
SB_DEAD_LOG2 = -152.0
SB_ABSENT_LOG2 = -1e30
TM_PROJ = 2048
TM_MERGE = 512
TM_MLP = 1024
PREV_ROWS = 16

BF16 = jnp.bfloat16
F32 = jnp.float32


def _params(*sem):
    return pltpu.CompilerParams(dimension_semantics=sem, vmem_limit_bytes=VMEM_LIMIT_BYTES)


def _dot(a, b):
    return jnp.dot(a, b, preferred_element_type=F32)


def _dot_nt(a, b):
    return lax.dot_general(a, b, (((1,), (1,)), ((), ())), preferred_element_type=F32)


def _dot_tn(a, b):
    return lax.dot_general(a, b, (((0,), (0,)), ((), ())), preferred_element_type=F32)


def _rmsnorm_rows(x, g):
    ms = jnp.mean(x * x, axis=-1, keepdims=True)
    return x * lax.rsqrt(ms + EPS) * g


def _log_sigmoid(x):
    return jnp.minimum(x, 0.0) - jnp.log(1.0 + jnp.exp(-jnp.abs(x)))


def _split_bf16(x):
    hi = x.astype(BF16)
    lo = (x - hi.astype(F32)).astype(BF16)
    return hi, lo


def _prenorm_outputs(x, g, wg_hi, wg_lo, bg):
    h_hi, h_lo = _split_bf16(_rmsnorm_rows(x, g))
    gates = _dot_nt(wg_hi, h_hi) + (_dot_nt(wg_hi, h_lo) + _dot_nt(wg_lo, h_hi)) + bg
    return h_hi, gates


def _prenorm_kernel(x_ref, g_ref, wg_hi_ref, wg_lo_ref, bg_ref, h_ref, gates_ref):
    h_ref[...], gates_ref[...] = _prenorm_outputs(x_ref[...], g_ref[...], wg_hi_ref[...],
                                                  wg_lo_ref[...], bg_ref[...])


def _prenorm(x, g, wg_hi, wg_lo, bg):
    T, D = x.shape
    G = wg_hi.shape[0]
    tm = min(TM_PROJ, T)
    const = lambda i: (0, 0)
    return pl.pallas_call(
        _prenorm_kernel,
        grid=(T // tm,),
        in_specs=[
            pl.BlockSpec((tm, D), lambda i: (i, 0)),
            pl.BlockSpec((1, D), const),
            pl.BlockSpec((G, D), const),
            pl.BlockSpec((G, D), const),
            pl.BlockSpec((G, 1), const),
        ],
        out_specs=[pl.BlockSpec((tm, D), lambda i: (i, 0)), pl.BlockSpec((G, tm), lambda i: (0, i))],
        out_shape=[jax.ShapeDtypeStruct((T, D), BF16), jax.ShapeDtypeStruct((G, T), F32)],
        compiler_params=_params("parallel"),
        name="prenorm",
    )(x, g, wg_hi, wg_lo, bg)


N_GROUPS = 13
ALIGNED_GROUPS = 4
GATE_ROWS = 2 * ML_HEADS
(G_MQ, G_MK, G_MV, G_MO, G_B, G_C, G_U, G_SQ, G_SK, G_SV, G_GML, G_GSC, G_GSB) = range(N_GROUPS)


def _regroup_kernel(a_ref, b_ref, o_ref):
    k = pl.program_id(1)

    @pl.when(k < ALIGNED_GROUPS)
    def _():
        o_ref[...] = a_ref[...].astype(o_ref.dtype)

    @pl.when(k >= ALIGNED_GROUPS)
    def _():
        window = jnp.concatenate([a_ref[GATE_ROWS:, :], b_ref[...]], axis=0)
        o_ref[...] = window.astype(o_ref.dtype)


def _regroup_w_in_t(w_in_t):
    depth, _, D = w_in_t.shape
    return pl.pallas_call(
        _regroup_kernel,
        grid=(depth, N_GROUPS),
        in_specs=[
            pl.BlockSpec((None, D, D), lambda l, k: (l, k, 0)),
            pl.BlockSpec((None, GATE_ROWS, D), lambda l, k: (l, (k + 1) * (D // GATE_ROWS), 0)),
        ],
        out_specs=pl.BlockSpec((None, D, D), lambda l, k: (l, k, 0)),
        out_shape=jax.ShapeDtypeStruct((depth, N_GROUPS * D, D), BF16),
        compiler_params=_params("parallel", "parallel"),
        name="regroup_w_in",
    )(w_in_t, w_in_t)


def _pick(j, values):
    return sum(jnp.where(j == n, v, 0) for n, v in enumerate(values))


def _proj_kernel(scale_ref, h_ref, w_ref, o_ref):
    scale = scale_ref[pl.program_id(1)]
    o_ref[...] = (_dot_nt(h_ref[...], w_ref[...]) * scale).astype(o_ref.dtype)


def _proj(h, wt, l, groups, scales, out_dtype):
    T, D = h.shape
    tm = min(TM_PROJ, T)
    return pl.pallas_call(
        _proj_kernel,
        grid=(T // tm, len(groups)),
        in_specs=[
            pl.BlockSpec(memory_space=pltpu.SMEM),
            pl.BlockSpec((tm, D), lambda i, j: (i, 0)),
            pl.BlockSpec((None, D, D), lambda i, j: (l, _pick(j, groups), 0)),
        ],
        out_specs=pl.BlockSpec((tm, D), lambda i, j: (i, j)),
        out_shape=jax.ShapeDtypeStruct((T, len(groups) * D), out_dtype),
        compiler_params=_params("parallel", "parallel"),
        name="proj",
    )(jnp.array(scales, F32), h, wt)


def _proj_t_kernel(scale_ref, h_ref, w_ref, o_ref, *, sigmoid):
    out = _dot_nt(w_ref[...], h_ref[...]) * scale_ref[pl.program_id(1)]
    o_ref[...] = (jax.nn.sigmoid(out) if sigmoid else out).astype(o_ref.dtype)


def _proj_t(h, wt, l, groups, scales, out_dtype, sigmoid=False):
    T, D = h.shape
    tm = min(TM_PROJ, T)
    return pl.pallas_call(
        functools.partial(_proj_t_kernel, sigmoid=sigmoid),
        grid=(T // tm, len(groups)),
        in_specs=[
            pl.BlockSpec(memory_space=pltpu.SMEM),
            pl.BlockSpec((tm, D), lambda i, j: (i, 0)),
            pl.BlockSpec((None, D, D), lambda i, j: (l, _pick(j, groups), 0)),
        ],
        out_specs=pl.BlockSpec((D, tm), lambda i, j: (j, i)),
        out_shape=jax.ShapeDtypeStruct((len(groups) * D, T), out_dtype),
        compiler_params=_params("parallel", "parallel"),
        name="proj_t",
    )(jnp.array(scales, F32), h, wt)


def _row_to_columns(r):
    L = r.shape[1]
    col = jnp.transpose(jnp.broadcast_to(r, (128, L)))
    return jnp.concatenate([col] * (L // 128), axis=1)


def _mlstm_kernel(k_ref, qT_ref, vT_ref, oT_ref, gates_ref, gain_ref, y_ref, c_ref, n_ref, m_ref):
    @pl.when(pl.program_id(1) == 0)
    def _():
        c_ref[...] = jnp.zeros_like(c_ref)
        n_ref[...] = jnp.zeros_like(n_ref)
        m_ref[...] = jnp.zeros_like(m_ref)

    L = k_ref.shape[0]
    dh = ML_HEAD_DIM
    feat = lambda g: slice(g * dh, (g + 1) * dh)
    s_idx = lax.broadcasted_iota(jnp.int32, (L, L), 0)
    t_idx = lax.broadcasted_iota(jnp.int32, (L, L), 1)
    causal = s_idx <= t_idx

    gates = gates_ref[...]
    log_f = _log_sigmoid(gates)
    st = {}

    def state_products(g):
        k = k_ref[:, feat(g)]
        qT = qT_ref[feat(g), :]
        c_prev, n_prev = c_ref[g], n_ref[g]
        st[g, "kq"] = _dot(k, qT)
        st[g, "cq"] = _dot(c_prev.astype(BF16), qT)
        st[g, "nq"] = _dot(n_prev.astype(BF16), qT)[0:1, :]

    def intra_chunk(g):
        m_prev = m_ref[g]
        lf_col = _row_to_columns(log_f[ML_HEADS + g:ML_HEADS + g + 1, :])
        b_row = jnp.sum(jnp.where(causal, lf_col, 0.0), axis=0, keepdims=True)
        c_col = _row_to_columns(gates[g:g + 1, :] - b_row)
        c_max = jnp.max(jnp.where(causal, c_col, -jnp.inf), axis=0, keepdims=True)
        u_row = jnp.maximum(m_prev, c_max)
        s_w = st.pop((g, "kq")) * jnp.where(causal, jnp.exp(c_col - u_row), 0.0)
        a_inter = jnp.exp(m_prev - u_row)
        st[g, "den"] = jnp.sum(s_w, axis=0, keepdims=True) + a_inter * st.pop((g, "nq"))
        st[g, "num"] = _dot(vT_ref[feat(g), :], s_w.astype(BF16))
        st[g, "b_row"], st[g, "u_row"], st[g, "a_inter"] = b_row, u_row, a_inter

    def next_state(g):
        m_prev, b_row = m_ref[g], st[g, "b_row"]
        b_last = b_row[:, L - 1:L]
        g_row = b_last - b_row + gates[g:g + 1, :]
        m_new = jnp.maximum(b_last + m_prev, jnp.max(g_row, axis=1, keepdims=True))
        decay = jnp.exp(b_last + m_prev - m_new)
        w_state = jnp.exp(g_row - m_new)
        k = k_ref[:, feat(g)]
        vw = (vT_ref[feat(g), :].astype(F32) * w_state).astype(BF16)
        c_ref[g] = decay * c_ref[g] + _dot(vw, k)
        n_ref[g] = decay * n_ref[g] + _dot(jnp.broadcast_to(w_state, (8, L)).astype(BF16), k)
        m_ref[g] = m_new

    def output(g):
        floor = jnp.exp(-(st.pop((g, "b_row")) + st.pop((g, "u_row"))))
        num = st.pop((g, "num")) + st.pop((g, "a_inter")) * st.pop((g, "cq"))
        h = num / jnp.maximum(jnp.abs(st.pop((g, "den"))), floor)
        hg = oT_ref[feat(g), :] * h
        ms = jnp.mean(hg * hg, axis=0, keepdims=True)
        gain = jnp.concatenate([gain_ref[feat(g), :]] * (L // 128), axis=1)
        y_ref[feat(g), :] = (hg * lax.rsqrt(ms + EPS) * gain).astype(y_ref.dtype)

    for stage in (state_products, intra_chunk, next_state, output):
        for g in range(ML_HEADS):
            stage(g)


def _mlstm(pn, pt, pt_f, gates, gain, *, batch, seq, k_col, qT_row, vT_row, oT_row):
    T = batch * seq
    L = min(ML_CHUNK, seq)
    nc = seq // L
    dh = ML_HEAD_DIM
    W = ML_HEADS * dh
    tok = lambda b, c: b * nc + c
    return pl.pallas_call(
        _mlstm_kernel,
        grid=(batch, nc),
        in_specs=[
            pl.BlockSpec((L, W), lambda b, c: (tok(b, c), k_col)),
            pl.BlockSpec((W, L), lambda b, c: (qT_row, tok(b, c))),
            pl.BlockSpec((W, L), lambda b, c: (vT_row, tok(b, c))),
            pl.BlockSpec((W, L), lambda b, c: (oT_row, tok(b, c))),
            pl.BlockSpec((2 * ML_HEADS, L), lambda b, c: (0, tok(b, c))),
            pl.BlockSpec((W, 128), lambda b, c: (0, 0)),
        ],
        out_specs=pl.BlockSpec((W, L), lambda b, c: (0, tok(b, c))),
        out_shape=jax.ShapeDtypeStruct((W, T), BF16),
        scratch_shapes=[
            pltpu.VMEM((ML_HEADS, dh, dh), F32),
            pltpu.VMEM((ML_HEADS, 8, dh), F32),
            pltpu.VMEM((ML_HEADS, 1, 1), F32),
        ],
        compiler_params=_params("parallel", "arbitrary"),
        name="mlstm",
    )(pn, pt, pt, pt_f, gates, gain)


def _sb_kernel(qT_ref, k_ref, vT_ref, o_ref, acc_ref, *, blk, heads, qblocks):
    step = pl.program_id(2)
    d = SB_HEAD_DIM

    s_idx = lax.broadcasted_iota(jnp.int32, (blk, blk), 0)
    t_idx = lax.broadcasted_iota(jnp.int32, (blk, blk), 1)
    later = (t_idx > s_idx)
    from_here = jnp.where(t_idx >= s_idx, 1.0, 0.0).astype(BF16)

    row_head = lax.broadcasted_iota(jnp.int32, (2 * d, blk), 0) // d
    q_pads = []
    for qb in range(qblocks):
        q_pads.append([])
        for g in range(heads):
            q2 = qT_ref[(g // 2) * 2 * d:(g // 2 + 1) * 2 * d, qb * blk:(qb + 1) * blk]
            q_pads[qb].append(jnp.where(row_head == g % 2, q2, jnp.zeros_like(q2)))

    def neg_log2_1m(zs):
        neg_abs = pltpu.bitcast(pltpu.bitcast(zs, jnp.uint32) | jnp.uint32(0x80000000), F32)
        return jnp.maximum(zs, 0.0) + jnp.log(1.0 + jnp.exp2(neg_abs)) * LOG2E

    def visit(chains, rs, init_acc):
        starts = [pl.multiple_of(j * blk, blk) for _, j, _, _ in chains]
        units = [(c, g) for c in range(len(chains)) for g in range(heads)]
        rs = {qb: list(r) for qb, r in rs.items()}
        z, tail, started = {}, {}, set()

        def scores(c, g):
            qb = chains[c][0]
            z[c, g] = _dot(k_ref[pl.ds(starts[c], blk), (g // 2) * 2 * d:(g // 2 + 1) * 2 * d],
                           q_pads[qb][g])

        def tails(c, g):
            nl1 = neg_log2_1m(z[c, g])
            if chains[c][2]:
                nl1 = jnp.where(later, nl1, 0.0)
            tail[c, g] = _dot(from_here, nl1.astype(BF16))

        def weights(c, g):
            qb, _, diagonal, r_gate = chains[c]
            r_in = rs[qb][g] if r_gate is None else jnp.where(r_gate, rs[qb][g], SB_ABSENT_LOG2)
            rs[qb][g] = r_in - tail[c, g][0:1, :]
            a = jnp.exp2(z.pop((c, g)) - tail.pop((c, g)) + r_in)
            if diagonal:
                a = jnp.where(later, a, 0.0)
            contrib = _dot(vT_ref[g * d:(g + 1) * d, pl.ds(starts[c], blk)], a.astype(BF16))
            if init_acc and (qb, g) not in started:
                acc_ref[g * d:(g + 1) * d, qb * blk:(qb + 1) * blk] = contrib
            else:
                acc_ref[g * d:(g + 1) * d, qb * blk:(qb + 1) * blk] += contrib
            started.add((qb, g))

        stages = (scores, tails, weights)
        lag = SB_STAGE_LAG
        for n in range(len(units) + lag * (len(stages) - 1)):
            for k, stage in enumerate(stages):
                if 0 <= n - lag * k < len(units):
                    stage(*units[n - lag * k])
        return {qb: (jnp.max(functools.reduce(jnp.maximum, r)) > SB_DEAD_LOG2, tuple(r))
                for qb, r in rs.items()}

    blocks = [step * qblocks + qb for qb in range(qblocks)]
    chains = []
    for qb, i in enumerate(blocks):
        chains += [(qb, i, True, None), (qb, jnp.maximum(i - 1, 0), False, i > 0)]
    zeros = tuple(jnp.zeros((1, blk), F32) for _ in range(heads))
    first = visit(chains, {qb: zeros for qb in range(qblocks)}, True)

    for qb, i in enumerate(blocks):
        def cond(c, i=i):
            n, live, _ = c
            return jnp.logical_and(n < i - 1, live)

        def body(c, qb=qb, i=i):
            n, _, rs = c
            return (n + 1,) + visit([(qb, i - 2 - n, False, None)], {qb: rs}, False)[qb]

        lax.while_loop(cond, body, (jnp.int32(0),) + first[qb])
    o_ref[...] = acc_ref[...].astype(o_ref.dtype)


def _stick_breaking(pn_b, pt, *, batch, seq, k_col, qT_row, vT_row):
    T = batch * seq
    blk = min(SB_BLOCK, seq)
    qblocks = min(SB_QBLOCKS, seq // blk)
    tq = qblocks * blk
    nq = seq // tq
    gd = SB_GROUP * SB_HEAD_DIM
    return pl.pallas_call(
        functools.partial(_sb_kernel, blk=blk, heads=SB_GROUP, qblocks=qblocks),
        grid=(batch, SB_HEADS // SB_GROUP, nq),
        in_specs=[
            pl.BlockSpec((gd, tq), lambda b, h, i: (qT_row + h, b * nq + i)),
            pl.BlockSpec((seq, gd), lambda b, h, i: (b, k_col + h)),
            pl.BlockSpec((gd, seq), lambda b, h, i: (vT_row + h, b)),
        ],
        out_specs=pl.BlockSpec((gd, tq), lambda b, h, i: (h, b * nq + i)),
        out_shape=jax.ShapeDtypeStruct((SB_HEADS * SB_HEAD_DIM, T), BF16),
        scratch_shapes=[pltpu.VMEM((gd, tq), F32)],
        compiler_params=_params("parallel", "parallel", "arbitrary"),
        name="stick_breaking",
    )(pt, pn_b, pt)


def _merge_kernel(x_ref, h_ref, hp_ref, wb_ref, wc_ref, wu_ref, wgml_ref, wgsc_ref, wgsb_ref,
                  ymlT_ref, ysbT_ref, cw_ref, wml_ref, wsc_ref, wsb_ref, wout_ref, o_ref, *,
                  tiles_per_seq):
    tm, D = x_ref.shape
    P = hp_ref.shape[0]
    first = (pl.program_id(0) % tiles_per_seq) == 0
    h = h_ref[...]
    h_ext = jnp.concatenate([hp_ref[...], h], axis=0)

    z = _dot_nt(h_ext, wc_ref[...]) * _dot_nt(h_ext, wu_ref[...])
    row = lax.broadcasted_iota(jnp.int32, z.shape, 0)
    z = jnp.where(jnp.logical_and(first, row < P), 0.0, z)
    cw = cw_ref[...]
    conv = (cw[0:1, :] * z[P - 2:P - 2 + tm, :] + cw[1:2, :] * z[P - 1:P - 1 + tm, :]
            + cw[2:3, :] * z[P:P + tm, :])
    y_sc = _dot_nt(h, wb_ref[...]) * conv

    merged = jax.nn.sigmoid(_dot_nt(h, wgml_ref[...])) * _dot_tn(ymlT_ref[...], wml_ref[...])
    merged += jax.nn.sigmoid(_dot_nt(h, wgsc_ref[...])) * _dot(y_sc.astype(BF16), wsc_ref[...])
    merged += jax.nn.sigmoid(_dot_nt(h, wgsb_ref[...])) * _dot_tn(ysbT_ref[...], wsb_ref[...])
    o_ref[...] = x_ref[...] + _dot(merged.astype(BF16), wout_ref[...])


def _merge(x, h, wt, l, y_mlT, y_sbT, conv_w, w_ml, w_sc, w_sb, w_out, *, seq):
    T, D = x.shape
    tm = min(TM_MERGE, seq)
    const = lambda i: (0, 0)
    resident = lambda shape: pl.BlockSpec(shape, const, pipeline_mode=pl.Buffered(1))
    group = lambda g: pl.BlockSpec((None, D, D), lambda i: (l, g, 0), pipeline_mode=pl.Buffered(1))
    layer = lambda shape: pl.BlockSpec((None,) + shape, lambda i: (l, 0, 0), pipeline_mode=pl.Buffered(1))
    return pl.pallas_call(
        functools.partial(_merge_kernel, tiles_per_seq=seq // tm),
        grid=(T // tm,),
        in_specs=[
            pl.BlockSpec((tm, D), lambda i: (i, 0)),
            pl.BlockSpec((tm, D), lambda i: (i, 0)),
            pl.BlockSpec((PREV_ROWS, D), lambda i: (jnp.maximum(i * (tm // PREV_ROWS) - 1, 0), 0)),
            group(G_B), group(G_C), group(G_U), group(G_GML), group(G_GSC), group(G_GSB),
            pl.BlockSpec((D, tm), lambda i: (0, i)),
            pl.BlockSpec((D, tm), lambda i: (0, i)),
            resident((3, D)),
            layer((D, D)), layer((D, D)), layer((D, D)), layer((D, D)),
        ],
        out_specs=pl.BlockSpec((tm, D), lambda i: (i, 0)),
        out_shape=jax.ShapeDtypeStruct((T, D), F32),
        compiler_params=_params("parallel"),
        name="merge",
    )(x, h, h, wt, wt, wt, wt, wt, wt, y_mlT, y_sbT, conv_w, w_ml, w_sc, w_sb, w_out)


def _mlp_kernel(x_ref, g_ref, wup_ref, wdown_ref, gn_ref, wg_hi_ref, wg_lo_ref, bg_ref, *out_refs,
                ff_chunk, last):
    x = x_ref[...]
    h = _rmsnorm_rows(x, g_ref[...]).astype(BF16)
    acc = x
    for c in range(wup_ref.shape[1] // ff_chunk):
        cols = slice(c * ff_chunk, (c + 1) * ff_chunk)
        up = jnp.maximum(_dot(h, wup_ref[:, cols]), 0.0)
        acc = acc + _dot((up * up).astype(BF16), wdown_ref[cols, :])
    if last:
        out_refs[0][...] = _rmsnorm_rows(acc, gn_ref[...])
    else:
        out_refs[0][...] = acc
        out_refs[1][...], out_refs[2][...] = _prenorm_outputs(acc, gn_ref[...], wg_hi_ref[...],
                                                              wg_lo_ref[...], bg_ref[...])


def _mlp(x, g, w_up, w_down, l, g_next, wg_hi, wg_lo, bg, *, last):
    T, D = x.shape
    F = w_up.shape[2]
    G = wg_hi.shape[0]
    tm = min(TM_MLP, T)
    const = lambda i: (0, 0)
    resident = lambda shape: pl.BlockSpec(shape, const, pipeline_mode=pl.Buffered(1))
    layer = lambda shape: pl.BlockSpec((None,) + shape, lambda i: (l, 0, 0), pipeline_mode=pl.Buffered(1))
    out_specs = [pl.BlockSpec((tm, D), lambda i: (i, 0))]
    out_shape = [jax.ShapeDtypeStruct((T, D), F32)]
    if not last:
        out_specs += [pl.BlockSpec((tm, D), lambda i: (i, 0)), pl.BlockSpec((G, tm), lambda i: (0, i))]
        out_shape += [jax.ShapeDtypeStruct((T, D), BF16), jax.ShapeDtypeStruct((G, T), F32)]
    return pl.pallas_call(
        functools.partial(_mlp_kernel, ff_chunk=1024, last=last),
        grid=(T // tm,),
        in_specs=[
            pl.BlockSpec((tm, D), lambda i: (i, 0)),
            resident((1, D)),
            layer((D, F)),
            layer((F, D)),
            resident((1, D)),
            resident((G, D)), resident((G, D)), resident((G, 1)),
        ],
        out_specs=out_specs,
        out_shape=out_shape,
        compiler_params=_params("parallel"),
        name="mlp",
    )(x, g, w_up, w_down, g_next, wg_hi, wg_lo, bg)


def _gate_params(w_in_t, l, b_if):
    off = 4 * ML_HEADS * ML_HEAD_DIM
    wg_hi, wg_lo = _split_bf16(w_in_t[l, off:off + 2 * ML_HEADS, :])
    return wg_hi, wg_lo, b_if.reshape(2 * ML_HEADS, 1)


def _layer(x, h, gates, p, nxt, *, batch, seq):
    D = D_MODEL
    ml_w = ML_HEADS * ML_HEAD_DIM
    wt, l = p["w_in"]
    pn = _proj(h, wt, l, (G_MK, G_SK), (ML_HEAD_DIM ** -0.5, 1.0), BF16)
    pt = _proj_t(h, wt, l, (G_MQ, G_MV, G_SQ, G_SV), (1.0, 1.0, SB_HEAD_DIM ** -0.5 * LOG2E, 1.0), BF16)
    pt_f = _proj_t(h, wt, l, (G_MO,), (1.0,), F32, sigmoid=True)

    gain = jnp.broadcast_to(p["ml_norm_g"].reshape(ml_w, 1), (ml_w, 128))
    y_mlT = _mlstm(pn, pt, pt_f, gates, gain, batch=batch, seq=seq,
                   k_col=0, qT_row=0, vT_row=1, oT_row=0)
    sb_unit = SB_GROUP * SB_HEAD_DIM
    y_sbT = _stick_breaking(pn, pt, batch=batch, seq=seq, k_col=ml_w // sb_unit,
                            qT_row=2 * ml_w // sb_unit, vT_row=(2 * ml_w + D) // sb_unit)
    x = _merge(x, h, wt, l, y_mlT, y_sbT, p["conv_w"],
               p["w_ml_proj"], p["w_sc_proj"], p["w_sb_proj"], p["w_out"], seq=seq)
    return _mlp(x, p["norm_mlp_g"].reshape(1, D), p["w_up"], p["w_down"], l,
                nxt["g"].reshape(1, D), nxt["wg_hi"], nxt["wg_lo"], nxt["bg"], last=nxt["last"])


def kernel(x, norm_mix_g, w_in, b_if, ml_norm_g, conv_w, w_ml_proj, w_sc_proj, w_sb_proj, w_out,
           norm_mlp_g, w_up, w_down, norm_final_g):
    batch, seq, D = x.shape
    depth = w_in.shape[0]
    xt = x.reshape(batch * seq, D)
    w_in_t = jnp.swapaxes(w_in, 1, 2)
    gate_params = [_gate_params(w_in_t, l, b_if[l]) for l in range(depth)]
    h, gates = _prenorm(xt, norm_mix_g[0].reshape(1, D), *gate_params[0])
    wt = _regroup_w_in_t(w_in_t)
    stacked_bf16 = dict(w_ml_proj=w_ml_proj.astype(BF16), w_sc_proj=w_sc_proj.astype(BF16),
                        w_sb_proj=w_sb_proj.astype(BF16), w_out=w_out.astype(BF16),
                        w_up=w_up.astype(BF16), w_down=w_down.astype(BF16))
    for l in range(depth):
        p = dict(w_in=(wt, l), ml_norm_g=ml_norm_g[l], conv_w=conv_w[l], norm_mlp_g=norm_mlp_g[l],
                 **stacked_bf16)
        last = l == depth - 1
        nl = l if last else l + 1
        nxt = dict(g=norm_final_g if last else norm_mix_g[nl], wg_hi=gate_params[nl][0],
                   wg_lo=gate_params[nl][1], bg=gate_params[nl][2], last=last)
        outs = _layer(xt, h, gates, p, nxt, batch=batch, seq=seq)
        if last:
            xt = outs[0]
        else:
            xt, h, gates = outs
    return xt.reshape(batch, seq, D)
```

```python
import functools

import jax
import jax.numpy as jnp
from jax import lax
from jax.experimental import pallas as pl
from jax.experimental.pallas import tpu as pltpu

D_MODEL = 1024
ML_HEADS = 4
ML_HEAD_DIM = 256
SB_HEADS = 16
SB_HEAD_DIM = 64
D_FF = 4 * D_MODEL
EPS = 1e-6

VMEM_LIMIT_BYTES = 56 * 1024 * 1024

ML_CHUNK = 256
SB_BLOCK = 256
SB_GROUP = 4
SB_QBLOCKS = 2
SB_STAGE_LAG = 2
LOG2E = 1.4426950408889634
SB_DEAD_LOG2 = -152.0
SB_ABSENT_LOG2 = -1e30
TM_PROJ = 2048
TM_MERGE = 512
TM_MLP = 1024
PREV_ROWS = 16

BF16 = jnp.bfloat16
F32 = jnp.float32


def _params(*sem):
    return pltpu.CompilerParams(dimension_semantics=sem, vmem_limit_bytes=VMEM_LIMIT_BYTES)


def _dot(a, b):
    return jnp.dot(a, b, preferred_element_type=F32)


def _dot_nt(a, b):
    return lax.dot_general(a, b, (((1,), (1,)), ((), ())), preferred_element_type=F32)


def _dot_tn(a, b):
    return lax.dot_general(a, b, (((0,), (0,)), ((), ())), preferred_element_type=F32)


def _rmsnorm_rows(x, g):
    ms = jnp.mean(x * x, axis=-1, keepdims=True)
    return x * lax.rsqrt(ms + EPS) * g


def _log_sigmoid(x):
    return jnp.minimum(x, 0.0) - jnp.log(1.0 + jnp.exp(-jnp.abs(x)))


def _split_bf16(x):
    hi = x.astype(BF16)
    lo = (x - hi.astype(F32)).astype(BF16)
    return hi, lo


def _prenorm_outputs(x, g, wg_hi, wg_lo, bg):
    h_hi, h_lo = _split_bf16(_rmsnorm_rows(x, g))
    gates = _dot_nt(wg_hi, h_hi) + (_dot_nt(wg_hi, h_lo) + _dot_nt(wg_lo, h_hi)) + bg
    return h_hi, gates


def _prenorm_kernel(x_ref, g_ref, wg_hi_ref, wg_lo_ref, bg_ref, h_ref, gates_ref):
    h_ref[...], gates_ref[...] = _prenorm_outputs(x_ref[...], g_ref[...], wg_hi_ref[...],
                                                  wg_lo_ref[...], bg_ref[...])


def _prenorm(x, g, wg_hi, wg_lo, bg):
    T, D = x.shape
    G = wg_hi.shape[0]
    tm = min(TM_PROJ, T)
    const = lambda i: (0, 0)
    return pl.pallas_call(
        _prenorm_kernel,
        grid=(T // tm,),
        in_specs=[
            pl.BlockSpec((tm, D), lambda i: (i, 0)),
            pl.BlockSpec((1, D), const),
            pl.BlockSpec((G, D), const),
            pl.BlockSpec((G, D), const),
            pl.BlockSpec((G, 1), const),
        ],
        out_specs=[pl.BlockSpec((tm, D), lambda i: (i, 0)), pl.BlockSpec((G, tm), lambda i: (0, i))],
        out_shape=[jax.ShapeDtypeStruct((T, D), BF16), jax.ShapeDtypeStruct((G, T), F32)],
        compiler_params=_params("parallel"),
        name="prenorm",
    )(x, g, wg_hi, wg_lo, bg)


N_GROUPS = 13
ALIGNED_GROUPS = 4
GATE_ROWS = 2 * ML_HEADS
(G_MQ, G_MK, G_MV, G_MO, G_B, G_C, G_U, G_SQ, G_SK, G_SV, G_GML, G_GSC, G_GSB) = range(N_GROUPS)


def _regroup_kernel(a_ref, b_ref, o_ref):
    k = pl.program_id(1)

    @pl.when(k < ALIGNED_GROUPS)
    def _():
        o_ref[...] = a_ref[...].astype(o_ref.dtype)

    @pl.when(k >= ALIGNED_GROUPS)
    def _():
        window = jnp.concatenate([a_ref[GATE_ROWS:, :], b_ref[...]], axis=0)
        o_ref[...] = window.astype(o_ref.dtype)


def _regroup_w_in_t(w_in_t):
    depth, _, D = w_in_t.shape
    return pl.pallas_call(
        _regroup_kernel,
        grid=(depth, N_GROUPS),
        in_specs=[
            pl.BlockSpec((None, D, D), lambda l, k: (l, k, 0)),
            pl.BlockSpec((None, GATE_ROWS, D), lambda l, k: (l, (k + 1) * (D // GATE_ROWS), 0)),
        ],
        out_specs=pl.BlockSpec((None, D, D), lambda l, k: (l, k, 0)),
        out_shape=jax.ShapeDtypeStruct((depth, N_GROUPS * D, D), BF16),
        compiler_params=_params("parallel", "parallel"),
        name="regroup_w_in",
    )(w_in_t, w_in_t)


def _pick(j, values):
    return sum(jnp.where(j == n, v, 0) for n, v in enumerate(values))


def _proj_kernel(scale_ref, h_ref, w_ref, o_ref):
    scale = scale_ref[pl.program_id(1)]
    o_ref[...] = (_dot_nt(h_ref[...], w_ref[...]) * scale).astype(o_ref.dtype)


def _proj(h, wt, l, groups, scales, out_dtype):
    T, D = h.shape
    tm = min(TM_PROJ, T)
    return pl.pallas_call(
        _proj_kernel,
        grid=(T // tm, len(groups)),
        in_specs=[
            pl.BlockSpec(memory_space=pltpu.SMEM),
            pl.BlockSpec((tm, D), lambda i, j: (i, 0)),
            pl.BlockSpec((None, D, D), lambda i, j: (l, _pick(j, groups), 0)),
        ],
        out_specs=pl.BlockSpec((tm, D), lambda i, j: (i, j)),
        out_shape=jax.ShapeDtypeStruct((T, len(groups) * D), out_dtype),
        compiler_params=_params("parallel", "parallel"),
        name="proj",
    )(jnp.array(scales, F32), h, wt)


def _proj_t_kernel(scale_ref, h_ref, w_ref, o_ref, *, sigmoid):
    out = _dot_nt(w_ref[...], h_ref[...]) * scale_ref[pl.program_id(1)]
    o_ref[...] = (jax.nn.sigmoid(out) if sigmoid else out).astype(o_ref.dtype)


def _proj_t(h, wt, l, groups, scales, out_dtype, sigmoid=False):
    T, D = h.shape
    tm = min(TM_PROJ, T)
    return pl.pallas_call(
        functools.partial(_proj_t_kernel, sigmoid=sigmoid),
        grid=(T // tm, len(groups)),
        in_specs=[
            pl.BlockSpec(memory_space=pltpu.SMEM),
            pl.BlockSpec((tm, D), lambda i, j: (i, 0)),
            pl.BlockSpec((None, D, D), lambda i, j: (l, _pick(j, groups), 0)),
        ],
        out_specs=pl.BlockSpec((D, tm), lambda i, j: (j, i)),
        out_shape=jax.ShapeDtypeStruct((len(groups) * D, T), out_dtype),
        compiler_params=_params("parallel", "parallel"),
        name="proj_t",
    )(jnp.array(scales, F32), h, wt)


def _row_to_columns(r):
    L = r.shape[1]
    col = jnp.transpose(jnp.broadcast_to(r, (128, L)))
    return jnp.concatenate([col] * (L // 128), axis=1)


def _mlstm_kernel(k_ref, qT_ref, vT_ref, oT_ref, gates_ref, gain_ref, y_ref, c_ref, n_ref, m_ref):
    @pl.when(pl.program_id(1) == 0)
    def _():
        c_ref[...] = jnp.zeros_like(c_ref)
        n_ref[...] = jnp.zeros_like(n_ref)
        m_ref[...] = jnp.zeros_like(m_ref)

    L = k_ref.shape[0]
    dh = ML_HEAD_DIM
    feat = lambda g: slice(g * dh, (g + 1) * dh)
    s_idx = lax.broadcasted_iota(jnp.int32, (L, L), 0)
    t_idx = lax.broadcasted_iota(jnp.int32, (L, L), 1)
    causal = s_idx <= t_idx

    gates = gates_ref[...]
    log_f = _log_sigmoid(gates)
    st = {}

    def state_products(g):
        k = k_ref[:, feat(g)]
        qT = qT_ref[feat(g), :]
        c_prev, n_prev = c_ref[g], n_ref[g]
        st[g, "kq"] = _dot(k, qT)
        st[g, "cq"] = _dot(c_prev.astype(BF16), qT)
        st[g, "nq"] = _dot(n_prev.astype(BF16), qT)[0:1, :]

    def intra_chunk(g):
        m_prev = m_ref[g]
        lf_col = _row_to_columns(log_f[ML_HEADS + g:ML_HEADS + g + 1, :])
        b_row = jnp.sum(jnp.where(causal, lf_col, 0.0), axis=0, keepdims=True)
        c_col = _row_to_columns(gates[g:g + 1, :] - b_row)
        c_max = jnp.max(jnp.where(causal, c_col, -jnp.inf), axis=0, keepdims=True)
        u_row = jnp.maximum(m_prev, c_max)
        s_w = st.pop((g, "kq")) * jnp.where(causal, jnp.exp(c_col - u_row), 0.0)
        a_inter = jnp.exp(m_prev - u_row)
        st[g, "den"] = jnp.sum(s_w, axis=0, keepdims=True) + a_inter * st.pop((g, "nq"))
        st[g, "num"] = _dot(vT_ref[feat(g), :], s_w.astype(BF16))
        st[g, "b_row"], st[g, "u_row"], st[g, "a_inter"] = b_row, u_row, a_inter

    def next_state(g):
        m_prev, b_row = m_ref[g], st[g, "b_row"]
        b_last = b_row[:, L - 1:L]
        g_row = b_last - b_row + gates[g:g + 1, :]
        m_new = jnp.maximum(b_last + m_prev, jnp.max(g_row, axis=1, keepdims=True))
        decay = jnp.exp(b_last + m_prev - m_new)
        w_state = jnp.exp(g_row - m_new)
        k = k_ref[:, feat(g)]
        vw = (vT_ref[feat(g), :].astype(F32) * w_state).astype(BF16)
        c_ref[g] = decay * c_ref[g] + _dot(vw, k)
        n_ref[g] = decay * n_ref[g] + _dot(jnp.broadcast_to(w_state, (8, L)).astype(BF16), k)
        m_ref[g] = m_new

    def output(g):
        floor = jnp.exp(-(st.pop((g, "b_row")) + st.pop((g, "u_row"))))
        num = st.pop((g, "num")) + st.pop((g, "a_inter")) * st.pop((g, "cq"))
        h = num / jnp.maximum(jnp.abs(st.pop((g, "den"))), floor)
        hg = oT_ref[feat(g), :] * h
        ms = jnp.mean(hg * hg, axis=0, keepdims=True)
        gain = jnp.concatenate([gain_ref[feat(g), :]] * (L // 128), axis=1)
        y_ref[feat(g), :] = (hg * lax.rsqrt(ms + EPS) * gain).astype(y_ref.dtype)

    for stage in (state_products, intra_chunk, next_state, output):
        for g in range(ML_HEADS):
            stage(g)


def _mlstm(pn, pt, pt_f, gates, gain, *, batch, seq, k_col, qT_row, vT_row, oT_row):
    T = batch * seq
    L = min(ML_CHUNK, seq)
    nc = seq // L
    dh = ML_HEAD_DIM
    W = ML_HEADS * dh
    tok = lambda b, c: b * nc + c
    return pl.pallas_call(
        _mlstm_kernel,
        grid=(batch, nc),
        in_specs=[
            pl.BlockSpec((L, W), lambda b, c: (tok(b, c), k_col)),
            pl.BlockSpec((W, L), lambda b, c: (qT_row, tok(b, c))),
            pl.BlockSpec((W, L), lambda b, c: (vT_row, tok(b, c))),
            pl.BlockSpec((W, L), lambda b, c: (oT_row, tok(b, c))),
            pl.BlockSpec((2 * ML_HEADS, L), lambda b, c: (0, tok(b, c))),
            pl.BlockSpec((W, 128), lambda b, c: (0, 0)),
        ],
        out_specs=pl.BlockSpec((W, L), lambda b, c: (0, tok(b, c))),
        out_shape=jax.ShapeDtypeStruct((W, T), BF16),
        scratch_shapes=[
            pltpu.VMEM((ML_HEADS, dh, dh), F32),
            pltpu.VMEM((ML_HEADS, 8, dh), F32),
            pltpu.VMEM((ML_HEADS, 1, 1), F32),
        ],
        compiler_params=_params("parallel", "arbitrary"),
        name="mlstm",
    )(pn, pt, pt, pt_f, gates, gain)


def _sb_kernel(qT_ref, k_ref, vT_ref, o_ref, acc_ref, *, blk, heads, qblocks):
    step = pl.program_id(2)
    d = SB_HEAD_DIM

    s_idx = lax.broadcasted_iota(jnp.int32, (blk, blk), 0)
    t_idx = lax.broadcasted_iota(jnp.int32, (blk, blk), 1)
    later = (t_idx > s_idx)
    from_here = jnp.where(t_idx >= s_idx, 1.0, 0.0).astype(BF16)

    row_head = lax.broadcasted_iota(jnp.int32, (2 * d, blk), 0) // d
    q_pads = []
    for qb in range(qblocks):
        q_pads.append([])
        for g in range(heads):
            q2 = qT_ref[(g // 2) * 2 * d:(g // 2 + 1) * 2 * d, qb * blk:(qb + 1) * blk]
            q_pads[qb].append(jnp.where(row_head == g % 2, q2, jnp.zeros_like(q2)))

    def neg_log2_1m(zs):
        neg_abs = pltpu.bitcast(pltpu.bitcast(zs, jnp.uint32) | jnp.uint32(0x80000000), F32)
        return jnp.maximum(zs, 0.0) + jnp.log(1.0 + jnp.exp2(neg_abs)) * LOG2E

    def visit(chains, rs, init_acc):
        starts = [pl.multiple_of(j * blk, blk) for _, j, _, _ in chains]
        units = [(c, g) for c in range(len(chains)) for g in range(heads)]
        rs = {qb: list(r) for qb, r in rs.items()}
        z, tail, started = {}, {}, set()

        def scores(c, g):
            qb = chains[c][0]
            z[c, g] = _dot(k_ref[pl.ds(starts[c], blk), (g // 2) * 2 * d:(g // 2 + 1) * 2 * d],
                           q_pads[qb][g])

        def tails(c, g):
            nl1 = neg_log2_1m(z[c, g])
            if chains[c][2]:
                nl1 = jnp.where(later, nl1, 0.0)
            tail[c, g] = _dot(from_here, nl1.astype(BF16))

        def weights(c, g):
            qb, _, diagonal, r_gate = chains[c]
            r_in = rs[qb][g] if r_gate is None else jnp.where(r_gate, rs[qb][g], SB_ABSENT_LOG2)
            rs[qb][g] = r_in - tail[c, g][0:1, :]
            a = jnp.exp2(z.pop((c, g)) - tail.pop((c, g)) + r_in)
            if diagonal:
                a = jnp.where(later, a, 0.0)
            contrib = _dot(vT_ref[g * d:(g + 1) * d, pl.ds(starts[c], blk)], a.astype(BF16))
            if init_acc and (qb, g) not in started:
                acc_ref[g * d:(g + 1) * d, qb * blk:(qb + 1) * blk] = contrib
            else:
                acc_ref[g * d:(g + 1) * d, qb * blk:(qb + 1) * blk] += contrib
            started.add((qb, g))

        stages = (scores, tails, weights)
        lag = SB_STAGE_LAG
        for n in range(len(units) + lag * (len(stages) - 1)):
            for k, stage in enumerate(stages):
                if 0 <= n - lag * k < len(units):
                    stage(*units[n - lag * k])
        return {qb: (jnp.max(functools.reduce(jnp.maximum, r)) > SB_DEAD_LOG2, tuple(r))
                for qb, r in rs.items()}

    blocks = [step * qblocks + qb for qb in range(qblocks)]
    chains = []
    for qb, i in enumerate(blocks):
        chains += [(qb, i, True, None), (qb, jnp.maximum(i - 1, 0), False, i > 0)]
    zeros = tuple(jnp.zeros((1, blk), F32) for _ in range(heads))
    first = visit(chains, {qb: zeros for qb in range(qblocks)}, True)

    for qb, i in enumerate(blocks):
        def cond(c, i=i):
            n, live, _ = c
            return jnp.logical_and(n < i - 1, live)

        def body(c, qb=qb, i=i):
            n, _, rs = c
            return (n + 1,) + visit([(qb, i - 2 - n, False, None)], {qb: rs}, False)[qb]

        lax.while_loop(cond, body, (jnp.int32(0),) + first[qb])
    o_ref[...] = acc_ref[...].astype(o_ref.dtype)


def _stick_breaking(pn_b, pt, *, batch, seq, k_col, qT_row, vT_row):
    T = batch * seq
    blk = min(SB_BLOCK, seq)
    qblocks = min(SB_QBLOCKS, seq // blk)
    tq = qblocks * blk
    nq = seq // tq
    gd = SB_GROUP * SB_HEAD_DIM
    return pl.pallas_call(
        functools.partial(_sb_kernel, blk=blk, heads=SB_GROUP, qblocks=qblocks),
        grid=(batch, SB_HEADS // SB_GROUP, nq),
        in_specs=[
            pl.BlockSpec((gd, tq), lambda b, h, i: (qT_row + h, b * nq + i)),
            pl.BlockSpec((seq, gd), lambda b, h, i: (b, k_col + h)),
            pl.BlockSpec((gd, seq), lambda b, h, i: (vT_row + h, b)),
        ],
        out_specs=pl.BlockSpec((gd, tq), lambda b, h, i: (h, b * nq + i)),
        out_shape=jax.ShapeDtypeStruct((SB_HEADS * SB_HEAD_DIM, T), BF16),
        scratch_shapes=[pltpu.VMEM((gd, tq), F32)],
        compiler_params=_params("parallel", "parallel", "arbitrary"),
        name="stick_breaking",
    )(pt, pn_b, pt)


def _merge_kernel(x_ref, h_ref, hp_ref, wb_ref, wc_ref, wu_ref, wgml_ref, wgsc_ref, wgsb_ref,
                  ymlT_ref, ysbT_ref, cw_ref, wml_ref, wsc_ref, wsb_ref, wout_ref, o_ref, *,
                  tiles_per_seq):
    tm, D = x_ref.shape
    P = hp_ref.shape[0]
    first = (pl.program_id(0) % tiles_per_seq) == 0
    h = h_ref[...]
    h_ext = jnp.concatenate([hp_ref[...], h], axis=0)

    z = _dot_nt(h_ext, wc_ref[...]) * _dot_nt(h_ext, wu_ref[...])
    row = lax.broadcasted_iota(jnp.int32, z.shape, 0)
    z = jnp.where(jnp.logical_and(first, row < P), 0.0, z)
    cw = cw_ref[...]
    conv = (cw[0:1, :] * pltpu.roll(z, 2, axis=0)[P:, :] + cw[1:2, :] * pltpu.roll(z, 1, axis=0)[P:, :]
            + cw[2:3, :] * z[P:, :])
    y_sc = _dot_nt(h, wb_ref[...]) * conv

    merged = jax.nn.sigmoid(_dot_nt(h, wgml_ref[...])) * _dot_tn(ymlT_ref[...], wml_ref[...])
    merged += jax.nn.sigmoid(_dot_nt(h, wgsb_ref[...])) * _dot_tn(ysbT_ref[...], wsb_ref[...])
    merged += jax.nn.sigmoid(_dot_nt(h, wgsc_ref[...])) * _dot(y_sc.astype(BF16), wsc_ref[...])
    o_ref[...] = x_ref[...] + _dot(merged.astype(BF16), wout_ref[...])


def _merge(x, h, wt, l, y_mlT, y_sbT, conv_w, w_ml, w_sc, w_sb, w_out, *, seq):
    T, D = x.shape
    tm = min(TM_MERGE, seq)
    const = lambda i: (0, 0)
    resident = lambda shape: pl.BlockSpec(shape, const, pipeline_mode=pl.Buffered(1))
    group = lambda g: pl.BlockSpec((None, D, D), lambda i: (l, g, 0), pipeline_mode=pl.Buffered(1))
    layer = lambda shape: pl.BlockSpec((None,) + shape, lambda i: (l, 0, 0), pipeline_mode=pl.Buffered(1))
    return pl.pallas_call(
        functools.partial(_merge_kernel, tiles_per_seq=seq // tm),
        grid=(T // tm,),
        in_specs=[
            pl.BlockSpec((tm, D), lambda i: (i, 0)),
            pl.BlockSpec((tm, D), lambda i: (i, 0)),
            pl.BlockSpec((PREV_ROWS, D), lambda i: (jnp.maximum(i * (tm // PREV_ROWS) - 1, 0), 0)),
            group(G_B), group(G_C), group(G_U), group(G_GML), group(G_GSC), group(G_GSB),
            pl.BlockSpec((D, tm), lambda i: (0, i)),
            pl.BlockSpec((D, tm), lambda i: (0, i)),
            resident((3, D)),
            layer((D, D)), layer((D, D)), layer((D, D)), layer((D, D)),
        ],
        out_specs=pl.BlockSpec((tm, D), lambda i: (i, 0)),
        out_shape=jax.ShapeDtypeStruct((T, D), F32),
        compiler_params=_params("parallel"),
        name="merge",
    )(x, h, h, wt, wt, wt, wt, wt, wt, y_mlT, y_sbT, conv_w, w_ml, w_sc, w_sb, w_out)


def _mlp_kernel(x_ref, g_ref, wup_ref, wdown_ref, gn_ref, wg_hi_ref, wg_lo_ref, bg_ref, *out_refs,
                ff_chunk, last):
    x = x_ref[...]
    h = _rmsnorm_rows(x, g_ref[...]).astype(BF16)
    acc = x
    for c in range(wup_ref.shape[1] // ff_chunk):
        cols = slice(c * ff_chunk, (c + 1) * ff_chunk)
        up = jnp.maximum(_dot(h, wup_ref[:, cols]), 0.0)
        acc = acc + _dot((up * up).astype(BF16), wdown_ref[cols, :])
    if last:
        out_refs[0][...] = _rmsnorm_rows(acc, gn_ref[...])
    else:
        out_refs[0][...] = acc
        out_refs[1][...], out_refs[2][...] = _prenorm_outputs(acc, gn_ref[...], wg_hi_ref[...],
                                                              wg_lo_ref[...], bg_ref[...])


def _mlp(x, g, w_up, w_down, l, g_next, wg_hi, wg_lo, bg, *, last):
    T, D = x.shape
    F = w_up.shape[2]
    G = wg_hi.shape[0]
    tm = min(TM_MLP, T)
    const = lambda i: (0, 0)
    resident = lambda shape: pl.BlockSpec(shape, const, pipeline_mode=pl.Buffered(1))
    layer = lambda shape: pl.BlockSpec((None,) + shape, lambda i: (l, 0, 0), pipeline_mode=pl.Buffered(1))
    out_specs = [pl.BlockSpec((tm, D), lambda i: (i, 0))]
    out_shape = [jax.ShapeDtypeStruct((T, D), F32)]
    if not last:
        out_specs += [pl.BlockSpec((tm, D), lambda i: (i, 0)), pl.BlockSpec((G, tm), lambda i: (0, i))]
        out_shape += [jax.ShapeDtypeStruct((T, D), BF16), jax.ShapeDtypeStruct((G, T), F32)]
    return pl.pallas_call(
        functools.partial(_mlp_kernel, ff_chunk=1024, last=last),
        grid=(T // tm,),
        in_specs=[
            pl.BlockSpec((tm, D), lambda i: (i, 0)),
            resident((1, D)),
            layer((D, F)),
            layer((F, D)),
            resident((1, D)),
            resident((G, D)), resident((G, D)), resident((G, 1)),
        ],
        out_specs=out_specs,
        out_shape=out_shape,
        compiler_params=_params("parallel"),
        name="mlp",
    )(x, g, w_up, w_down, g_next, wg_hi, wg_lo, bg)


def _gate_params(w_in_t, l, b_if):
    off = 4 * ML_HEADS * ML_HEAD_DIM
    wg_hi, wg_lo = _split_bf16(w_in_t[l, off:off + 2 * ML_HEADS, :])
    return wg_hi, wg_lo, b_if.reshape(2 * ML_HEADS, 1)


def _layer(x, h, gates, p, nxt, *, batch, seq):
    D = D_MODEL
    ml_w = ML_HEADS * ML_HEAD_DIM
    wt, l = p["w_in"]
    pn = _proj(h, wt, l, (G_MK, G_SK), (ML_HEAD_DIM ** -0.5, 1.0), BF16)
    pt = _proj_t(h, wt, l, (G_MQ, G_MV, G_SQ, G_SV), (1.0, 1.0, SB_HEAD_DIM ** -0.5 * LOG2E, 1.0), BF16)
    pt_f = _proj_t(h, wt, l, (G_MO,), (1.0,), F32, sigmoid=True)

    gain = jnp.broadcast_to(p["ml_norm_g"].reshape(ml_w, 1), (ml_w, 128))
    y_mlT = _mlstm(pn, pt, pt_f, gates, gain, batch=batch, seq=seq,
                   k_col=0, qT_row=0, vT_row=1, oT_row=0)
    sb_unit = SB_GROUP * SB_HEAD_DIM
    y_sbT = _stick_breaking(pn, pt, batch=batch, seq=seq, k_col=ml_w // sb_unit,
                            qT_row=2 * ml_w // sb_unit, vT_row=(2 * ml_w + D) // sb_unit)
    x = _merge(x, h, wt, l, y_mlT, y_sbT, p["conv_w"],
               p["w_ml_proj"], p["w_sc_proj"], p["w_sb_proj"], p["w_out"], seq=seq)
    return _mlp(x, p["norm_mlp_g"].reshape(1, D), p["w_up"], p["w_down"], l,
                nxt["g"].reshape(1, D), nxt["wg_hi"], nxt["wg_lo"], nxt["bg"], last=nxt["last"])


def kernel(x, norm_mix_g, w_in, b_if, ml_norm_g, conv_w, w_ml_proj, w_sc_proj, w_sb_proj, w_out,
           norm_mlp_g, w_up, w_down, norm_final_g):
    batch, seq, D = x.shape
    depth = w_in.shape[0]
    xt = x.reshape(batch * seq, D)
    w_in_t = jnp.swapaxes(w_in, 1, 2)
    gate_params = [_gate_params(w_in_t, l, b_if[l]) for l in range(depth)]
    h, gates = _prenorm(xt, norm_mix_g[0].reshape(1, D), *gate_params[0])
    wt = _regroup_w_in_t(w_in_t)
    stacked_bf16 = dict(w_ml_proj=w_ml_proj.astype(BF16), w_sc_proj=w_sc_proj.astype(BF16),
                        w_sb_proj=w_sb_proj.astype(BF16), w_out=w_out.astype(BF16),
                        w_up=w_up.astype(BF16), w_down=w_down.astype(BF16))
    for l in range(depth):
        p = dict(w_in=(wt, l), ml_norm_g=ml_norm_g[l], conv_w=conv_w[l], norm_mlp_g=norm_mlp_g[l],
                 **stacked_bf16)
        last = l == depth - 1
        nl = l if last else l + 1
        nxt = dict(g=norm_final_g if last else norm_mix_g[nl], wg_hi=gate_params[nl][0],
                   wg_lo=gate_params[nl][1], bg=gate_params[nl][2], last=last)
        outs = _layer(xt, h, gates, p, nxt, batch=batch, seq=seq)
        if last:
            xt = outs[0]
        else:
            xt, h, gates = outs
    return xt.reshape(batch, seq, D)
```

```python
import functools

import jax
import jax.numpy as jnp
from jax import lax
from jax.experimental import pallas as pl
from jax.experimental.pallas import tpu as pltpu

D_MODEL = 1024
ML_HEADS = 4
ML_HEAD_DIM = 256
SB_HEADS = 16
SB_HEAD_DIM = 64
EPS = 1e-6

VMEM_LIMIT_BYTES = 56 * 1024 * 1024

ML_CHUNK = 256
SB_BLOCK = 256
SB_GROUP = 4
SB_QBLOCKS = 2
SB_STAGE_LAG = 2
LOG2E = 1.4426950408889634
SB_DEAD_LOG2 = -152.0
SB_ABSENT_LOG2 = -1e30
TM_PROJ = 2048
TM_MERGE = 512
TM_MLP = 1024
PREV_ROWS = 16

BF16 = jnp.bfloat16
F32 = jnp.float32


def _params(*sem):
    return pltpu.CompilerParams(dimension_semantics=sem, vmem_limit_bytes=VMEM_LIMIT_BYTES)


def _dot(a, b):
    return jnp.dot(a, b, preferred_element_type=F32)


def _dot_nt(a, b):
    return lax.dot_general(a, b, (((1,), (1,)), ((), ())), preferred_element_type=F32)


def _dot_tn(a, b):
    return lax.dot_general(a, b, (((0,), (0,)), ((), ())), preferred_element_type=F32)


def _rmsnorm_rows(x, g):
    ms = jnp.mean(x * x, axis=-1, keepdims=True)
    return x * lax.rsqrt(ms + EPS) * g


def _log_sigmoid(x):
    return jnp.minimum(x, 0.0) - jnp.log(1.0 + jnp.exp(-jnp.abs(x)))


def _split_bf16(x):
    hi = x.astype(BF16)
    lo = (x - hi.astype(F32)).astype(BF16)
    return hi, lo


def _prenorm_outputs(x, g, wg_hi, wg_lo, bg):
    h_hi, h_lo = _split_bf16(_rmsnorm_rows(x, g))
    gates = _dot_nt(wg_hi, h_hi) + (_dot_nt(wg_hi, h_lo) + _dot_nt(wg_lo, h_hi)) + bg
    return h_hi, gates


def _prenorm_kernel(x_ref, g_ref, wg_hi_ref, wg_lo_ref, bg_ref, h_ref, gates_ref):
    h_ref[...], gates_ref[...] = _prenorm_outputs(x_ref[...], g_ref[...], wg_hi_ref[...],
                                                  wg_lo_ref[...], bg_ref[...])


def _prenorm(x, g, wg_hi, wg_lo, bg):
    T, D = x.shape
    G = wg_hi.shape[0]
    tm = min(TM_PROJ, T)
    const = lambda i: (0, 0)
    return pl.pallas_call(
        _prenorm_kernel,
        grid=(T // tm,),
        in_specs=[
            pl.BlockSpec((tm, D), lambda i: (i, 0)),
            pl.BlockSpec((1, D), const),
            pl.BlockSpec((G, D), const),
            pl.BlockSpec((G, D), const),
            pl.BlockSpec((G, 1), const),
        ],
        out_specs=[pl.BlockSpec((tm, D), lambda i: (i, 0)), pl.BlockSpec((G, tm), lambda i: (0, i))],
        out_shape=[jax.ShapeDtypeStruct((T, D), BF16), jax.ShapeDtypeStruct((G, T), F32)],
        compiler_params=_params("parallel"),
        name="prenorm",
    )(x, g, wg_hi, wg_lo, bg)


N_GROUPS = 13
ALIGNED_GROUPS = 4
GATE_ROWS = 2 * ML_HEADS
(G_MQ, G_MK, G_MV, G_MO, G_B, G_C, G_U, G_SQ, G_SK, G_SV, G_GML, G_GSC, G_GSB) = range(N_GROUPS)


def _regroup_kernel(a_ref, b_ref, o_ref):
    k = pl.program_id(1)

    @pl.when(k < ALIGNED_GROUPS)
    def _():
        o_ref[...] = a_ref[...].astype(o_ref.dtype)

    @pl.when(k >= ALIGNED_GROUPS)
    def _():
        window = jnp.concatenate([a_ref[GATE_ROWS:, :], b_ref[...]], axis=0)
        o_ref[...] = window.astype(o_ref.dtype)


def _regroup_w_in_t(w_in_t):
    depth, _, D = w_in_t.shape
    return pl.pallas_call(
        _regroup_kernel,
        grid=(depth, N_GROUPS),
        in_specs=[
            pl.BlockSpec((None, D, D), lambda l, k: (l, k, 0)),
            pl.BlockSpec((None, GATE_ROWS, D), lambda l, k: (l, (k + 1) * (D // GATE_ROWS), 0)),
        ],
        out_specs=pl.BlockSpec((None, D, D), lambda l, k: (l, k, 0)),
        out_shape=jax.ShapeDtypeStruct((depth, N_GROUPS * D, D), BF16),
        compiler_params=_params("parallel", "parallel"),
        name="regroup_w_in",
    )(w_in_t, w_in_t)


def _pick(j, values):
    return sum(jnp.where(j == n, v, 0) for n, v in enumerate(values))


def _proj_kernel(scale_ref, h_ref, w_ref, o_ref):
    scale = scale_ref[pl.program_id(1)]
    o_ref[...] = (_dot_nt(h_ref[...], w_ref[...]) * scale).astype(o_ref.dtype)


def _proj(h, wt, l, groups, scales, out_dtype):
    T, D = h.shape
    tm = min(TM_PROJ, T)
    return pl.pallas_call(
        _proj_kernel,
        grid=(T // tm, len(groups)),
        in_specs=[
            pl.BlockSpec(memory_space=pltpu.SMEM),
            pl.BlockSpec((tm, D), lambda i, j: (i, 0)),
            pl.BlockSpec((None, D, D), lambda i, j: (l, _pick(j, groups), 0)),
        ],
        out_specs=pl.BlockSpec((tm, D), lambda i, j: (i, j)),
        out_shape=jax.ShapeDtypeStruct((T, len(groups) * D), out_dtype),
        compiler_params=_params("parallel", "parallel"),
        name="proj",
    )(jnp.array(scales, F32), h, wt)


def _proj_t_kernel(scale_ref, h_ref, w_ref, o_ref, *, sigmoid):
    out = _dot_nt(w_ref[...], h_ref[...]) * scale_ref[pl.program_id(1)]
    o_ref[...] = (jax.nn.sigmoid(out) if sigmoid else out).astype(o_ref.dtype)


def _proj_t(h, wt, l, groups, scales, out_dtype, sigmoid=False):
    T, D = h.shape
    tm = min(TM_PROJ, T)
    return pl.pallas_call(
        functools.partial(_proj_t_kernel, sigmoid=sigmoid),
        grid=(T // tm, len(groups)),
        in_specs=[
            pl.BlockSpec(memory_space=pltpu.SMEM),
            pl.BlockSpec((tm, D), lambda i, j: (i, 0)),
            pl.BlockSpec((None, D, D), lambda i, j: (l, _pick(j, groups), 0)),
        ],
        out_specs=pl.BlockSpec((D, tm), lambda i, j: (j, i)),
        out_shape=jax.ShapeDtypeStruct((len(groups) * D, T), out_dtype),
        compiler_params=_params("parallel", "parallel"),
        name="proj_t",
    )(jnp.array(scales, F32), h, wt)


def _row_to_columns(r):
    L = r.shape[1]
    col = jnp.transpose(jnp.broadcast_to(r, (128, L)))
    return jnp.concatenate([col] * (L // 128), axis=1)


def _mlstm_kernel(k_ref, qT_ref, vT_ref, oT_ref, gates_ref, gain_ref, y_ref, c_ref, n_ref, m_ref):
    @pl.when(pl.program_id(1) == 0)
    def _():
        c_ref[...] = jnp.zeros_like(c_ref)
        n_ref[...] = jnp.zeros_like(n_ref)
        m_ref[...] = jnp.zeros_like(m_ref)

    L = k_ref.shape[0]
    dh = ML_HEAD_DIM
    feat = lambda g: slice(g * dh, (g + 1) * dh)
    s_idx = lax.broadcasted_iota(jnp.int32, (L, L), 0)
    t_idx = lax.broadcasted_iota(jnp.int32, (L, L), 1)
    causal = s_idx <= t_idx

    gates = gates_ref[...]
    log_f = _log_sigmoid(gates)
    st = {}

    def state_products(g):
        k = k_ref[:, feat(g)]
        qT = qT_ref[feat(g), :]
        c_prev, n_prev = c_ref[g], n_ref[g]
        st[g, "kq"] = _dot(k, qT)
        st[g, "cq"] = _dot(c_prev.astype(BF16), qT)
        st[g, "nq"] = _dot(n_prev.astype(BF16), qT)[0:1, :]

    def intra_chunk(g):
        m_prev = m_ref[g]
        lf_col = _row_to_columns(log_f[ML_HEADS + g:ML_HEADS + g + 1, :])
        b_row = jnp.sum(jnp.where(causal, lf_col, 0.0), axis=0, keepdims=True)
        c_col = _row_to_columns(gates[g:g + 1, :] - b_row)
        c_max = jnp.max(jnp.where(causal, c_col, -jnp.inf), axis=0, keepdims=True)
        u_row = jnp.maximum(m_prev, c_max)
        s_w = st.pop((g, "kq")) * jnp.where(causal, jnp.exp(c_col - u_row), 0.0)
        a_inter = jnp.exp(m_prev - u_row)
        st[g, "den"] = jnp.sum(s_w, axis=0, keepdims=True) + a_inter * st.pop((g, "nq"))
        st[g, "num"] = _dot(vT_ref[feat(g), :], s_w.astype(BF16))
        st[g, "b_row"], st[g, "u_row"], st[g, "a_inter"] = b_row, u_row, a_inter

    def next_state(g):
        m_prev, b_row = m_ref[g], st[g, "b_row"]
        b_last = b_row[:, L - 1:L]
        g_row = b_last - b_row + gates[g:g + 1, :]
        m_new = jnp.maximum(b_last + m_prev, jnp.max(g_row, axis=1, keepdims=True))
        decay = jnp.exp(b_last + m_prev - m_new)
        w_state = jnp.exp(g_row - m_new)
        k = k_ref[:, feat(g)]
        vw = (vT_ref[feat(g), :].astype(F32) * w_state).astype(BF16)
        c_ref[g] = decay * c_ref[g] + _dot(vw, k)
        n_ref[g] = decay * n_ref[g] + _dot(jnp.broadcast_to(w_state, (8, L)).astype(BF16), k)
        m_ref[g] = m_new

    def output(g):
        floor = jnp.exp(-(st.pop((g, "b_row")) + st.pop((g, "u_row"))))
        num = st.pop((g, "num")) + st.pop((g, "a_inter")) * st.pop((g, "cq"))
        h = num / jnp.maximum(jnp.abs(st.pop((g, "den"))), floor)
        hg = oT_ref[feat(g), :] * h
        ms = jnp.mean(hg * hg, axis=0, keepdims=True)
        gain = jnp.concatenate([gain_ref[feat(g), :]] * (L // 128), axis=1)
        y_ref[feat(g), :] = (hg * lax.rsqrt(ms + EPS) * gain).astype(y_ref.dtype)

    for stage in (state_products, intra_chunk, next_state, output):
        for g in range(ML_HEADS):
            stage(g)


def _mlstm(pn, pt, pt_f, gates, gain, *, batch, seq, k_col, qT_row, vT_row, oT_row):
    T = batch * seq
    L = min(ML_CHUNK, seq)
    nc = seq // L
    dh = ML_HEAD_DIM
    W = ML_HEADS * dh
    tok = lambda b, c: b * nc + c
    return pl.pallas_call(
        _mlstm_kernel,
        grid=(batch, nc),
        in_specs=[
            pl.BlockSpec((L, W), lambda b, c: (tok(b, c), k_col)),
            pl.BlockSpec((W, L), lambda b, c: (qT_row, tok(b, c))),
            pl.BlockSpec((W, L), lambda b, c: (vT_row, tok(b, c))),
            pl.BlockSpec((W, L), lambda b, c: (oT_row, tok(b, c))),
            pl.BlockSpec((2 * ML_HEADS, L), lambda b, c: (0, tok(b, c))),
            pl.BlockSpec((W, 128), lambda b, c: (0, 0)),
        ],
        out_specs=pl.BlockSpec((W, L), lambda b, c: (0, tok(b, c))),
        out_shape=jax.ShapeDtypeStruct((W, T), BF16),
        scratch_shapes=[
            pltpu.VMEM((ML_HEADS, dh, dh), F32),
            pltpu.VMEM((ML_HEADS, 8, dh), F32),
            pltpu.VMEM((ML_HEADS, 1, 1), F32),
        ],
        compiler_params=_params("parallel", "arbitrary"),
        name="mlstm",
    )(pn, pt, pt, pt_f, gates, gain)


def _sb_kernel(qT_ref, k_ref, vT_ref, o_ref, acc_ref, *, blk, heads, qblocks):
    step = pl.program_id(2)
    d = SB_HEAD_DIM

    s_idx = lax.broadcasted_iota(jnp.int32, (blk, blk), 0)
    t_idx = lax.broadcasted_iota(jnp.int32, (blk, blk), 1)
    later = (t_idx > s_idx)
    from_here = jnp.where(t_idx >= s_idx, 1.0, 0.0).astype(BF16)

    row_head = lax.broadcasted_iota(jnp.int32, (2 * d, blk), 0) // d
    q_pads = []
    for qb in range(qblocks):
        q_pads.append([])
        for g in range(heads):
            q2 = qT_ref[(g // 2) * 2 * d:(g // 2 + 1) * 2 * d, qb * blk:(qb + 1) * blk]
            q_pads[qb].append(jnp.where(row_head == g % 2, q2, jnp.zeros_like(q2)))

    def neg_log2_1m(zs):
        return jnp.maximum(zs, 0.0) + jnp.log(1.0 + jnp.exp2(-jnp.abs(zs))) * LOG2E

    def visit(chains, rs, init_acc):
        starts = [pl.multiple_of(j * blk, blk) for _, j, _, _ in chains]
        units = [(c, g) for c in range(len(chains)) for g in range(heads)]
        rs = {qb: list(r) for qb, r in rs.items()}
        z, tail, started = {}, {}, set()

        def scores(c, g):
            qb = chains[c][0]
            z[c, g] = _dot(k_ref[pl.ds(starts[c], blk), (g // 2) * 2 * d:(g // 2 + 1) * 2 * d],
                           q_pads[qb][g])

        def tails(c, g):
            nl1 = neg_log2_1m(z[c, g])
            if chains[c][2]:
                nl1 = jnp.where(later, nl1, 0.0)
            tail[c, g] = _dot(from_here, nl1.astype(BF16))

        def weights(c, g):
            qb, _, diagonal, r_gate = chains[c]
            r_in = rs[qb][g] if r_gate is None else jnp.where(r_gate, rs[qb][g], SB_ABSENT_LOG2)
            rs[qb][g] = r_in - tail[c, g][0:1, :]
            a = jnp.exp2(z.pop((c, g)) - tail.pop((c, g)) + r_in)
            if diagonal:
                a = jnp.where(later, a, 0.0)
            contrib = _dot(vT_ref[g * d:(g + 1) * d, pl.ds(starts[c], blk)], a.astype(BF16))
            if init_acc and (qb, g) not in started:
                acc_ref[g * d:(g + 1) * d, qb * blk:(qb + 1) * blk] = contrib
            else:
                acc_ref[g * d:(g + 1) * d, qb * blk:(qb + 1) * blk] += contrib
            started.add((qb, g))

        stages = (scores, tails, weights)
        lag = SB_STAGE_LAG
        for n in range(len(units) + lag * (len(stages) - 1)):
            for k, stage in enumerate(stages):
                if 0 <= n - lag * k < len(units):
                    stage(*units[n - lag * k])
        return {qb: (jnp.max(functools.reduce(jnp.maximum, r)) > SB_DEAD_LOG2, tuple(r))
                for qb, r in rs.items()}

    blocks = [step * qblocks + qb for qb in range(qblocks)]
    chains = []
    for qb, i in enumerate(blocks):
        chains += [(qb, i, True, None), (qb, jnp.maximum(i - 1, 0), False, i > 0)]
    zeros = tuple(jnp.zeros((1, blk), F32) for _ in range(heads))
    first = visit(chains, {qb: zeros for qb in range(qblocks)}, True)

    for qb, i in enumerate(blocks):
        def cond(c, i=i):
            n, live, _ = c
            return jnp.logical_and(n < i - 1, live)

        def body(c, qb=qb, i=i):
            n, _, rs = c
            return (n + 1,) + visit([(qb, i - 2 - n, False, None)], {qb: rs}, False)[qb]

        lax.while_loop(cond, body, (jnp.int32(0),) + first[qb])
    o_ref[...] = acc_ref[...].astype(o_ref.dtype)


def _stick_breaking(pn_b, pt, *, batch, seq, k_col, qT_row, vT_row):
    T = batch * seq
    blk = min(SB_BLOCK, seq)
    qblocks = min(SB_QBLOCKS, seq // blk)
    tq = qblocks * blk
    nq = seq // tq
    gd = SB_GROUP * SB_HEAD_DIM
    return pl.pallas_call(
        functools.partial(_sb_kernel, blk=blk, heads=SB_GROUP, qblocks=qblocks),
        grid=(batch, SB_HEADS // SB_GROUP, nq),
        in_specs=[
            pl.BlockSpec((gd, tq), lambda b, h, i: (qT_row + h, b * nq + i)),
            pl.BlockSpec((seq, gd), lambda b, h, i: (b, k_col + h)),
            pl.BlockSpec((gd, seq), lambda b, h, i: (vT_row + h, b)),
        ],
        out_specs=pl.BlockSpec((gd, tq), lambda b, h, i: (h, b * nq + i)),
        out_shape=jax.ShapeDtypeStruct((SB_HEADS * SB_HEAD_DIM, T), BF16),
        scratch_shapes=[pltpu.VMEM((gd, tq), F32)],
        compiler_params=_params("parallel", "parallel", "arbitrary"),
        name="stick_breaking",
    )(pt, pn_b, pt)


def _merge_kernel(x_ref, h_ref, hp_ref, wb_ref, wc_ref, wu_ref, wgml_ref, wgsc_ref, wgsb_ref,
                  ymlT_ref, ysbT_ref, cw_ref, wml_ref, wsc_ref, wsb_ref, wout_ref, o_ref, *,
                  tiles_per_seq):
    tm, D = x_ref.shape
    P = hp_ref.shape[0]
    first = (pl.program_id(0) % tiles_per_seq) == 0
    h = h_ref[...]
    h_ext = jnp.concatenate([hp_ref[...], h], axis=0)

    z = _dot_nt(h_ext, wc_ref[...]) * _dot_nt(h_ext, wu_ref[...])
    row = lax.broadcasted_iota(jnp.int32, z.shape, 0)
    z = jnp.where(jnp.logical_and(first, row < P), 0.0, z)
    cw = cw_ref[...]
    conv = (cw[0:1, :] * pltpu.roll(z, 2, axis=0)[P:, :] + cw[1:2, :] * pltpu.roll(z, 1, axis=0)[P:, :]
            + cw[2:3, :] * z[P:, :])
    y_sc = _dot_nt(h, wb_ref[...]) * conv

    merged = jax.nn.sigmoid(_dot_nt(h, wgml_ref[...])) * _dot_tn(ymlT_ref[...], wml_ref[...])
    merged += jax.nn.sigmoid(_dot_nt(h, wgsb_ref[...])) * _dot_tn(ysbT_ref[...], wsb_ref[...])
    merged += jax.nn.sigmoid(_dot_nt(h, wgsc_ref[...])) * _dot(y_sc.astype(BF16), wsc_ref[...])
    o_ref[...] = x_ref[...] + _dot(merged.astype(BF16), wout_ref[...])


def _merge(x, h, wt, l, y_mlT, y_sbT, conv_w, w_ml, w_sc, w_sb, w_out, *, seq):
    T, D = x.shape
    tm = min(TM_MERGE, seq)
    const = lambda i: (0, 0)
    resident = lambda shape: pl.BlockSpec(shape, const, pipeline_mode=pl.Buffered(1))
    group = lambda g: pl.BlockSpec((None, D, D), lambda i: (l, g, 0), pipeline_mode=pl.Buffered(1))
    layer = lambda shape: pl.BlockSpec((None,) + shape, lambda i: (l, 0, 0), pipeline_mode=pl.Buffered(1))
    return pl.pallas_call(
        functools.partial(_merge_kernel, tiles_per_seq=seq // tm),
        grid=(T // tm,),
        in_specs=[
            pl.BlockSpec((tm, D), lambda i: (i, 0)),
            pl.BlockSpec((tm, D), lambda i: (i, 0)),
            pl.BlockSpec((PREV_ROWS, D), lambda i: (jnp.maximum(i * (tm // PREV_ROWS) - 1, 0), 0)),
            group(G_B), group(G_C), group(G_U), group(G_GML), group(G_GSC), group(G_GSB),
            pl.BlockSpec((D, tm), lambda i: (0, i)),
            pl.BlockSpec((D, tm), lambda i: (0, i)),
            resident((3, D)),
            layer((D, D)), layer((D, D)), layer((D, D)), layer((D, D)),
        ],
        out_specs=pl.BlockSpec((tm, D), lambda i: (i, 0)),
        out_shape=jax.ShapeDtypeStruct((T, D), F32),
        compiler_params=_params("parallel"),
        name="merge",
    )(x, h, h, wt, wt, wt, wt, wt, wt, y_mlT, y_sbT, conv_w, w_ml, w_sc, w_sb, w_out)


def _mlp_kernel(x_ref, g_ref, wup_ref, wdown_ref, gn_ref, wg_hi_ref, wg_lo_ref, bg_ref, *out_refs,
                ff_chunk, last):
    x = x_ref[...]
    h = _rmsnorm_rows(x, g_ref[...]).astype(BF16)
    acc = x
    for c in range(wup_ref.shape[1] // ff_chunk):
        cols = slice(c * ff_chunk, (c + 1) * ff_chunk)
        up = jnp.maximum(_dot(h, wup_ref[:, cols]), 0.0)
        acc = acc + _dot((up * up).astype(BF16), wdown_ref[cols, :])
    if last:
        out_refs[0][...] = _rmsnorm_rows(acc, gn_ref[...])
    else:
        out_refs[0][...] = acc
        out_refs[1][...], out_refs[2][...] = _prenorm_outputs(acc, gn_ref[...], wg_hi_ref[...],
                                                              wg_lo_ref[...], bg_ref[...])


def _mlp(x, g, w_up, w_down, l, g_next, wg_hi, wg_lo, bg, *, last):
    T, D = x.shape
    F = w_up.shape[2]
    G = wg_hi.shape[0]
    tm = min(TM_MLP, T)
    const = lambda i: (0, 0)
    resident = lambda shape: pl.BlockSpec(shape, const, pipeline_mode=pl.Buffered(1))
    layer = lambda shape: pl.BlockSpec((None,) + shape, lambda i: (l, 0, 0), pipeline_mode=pl.Buffered(1))
    out_specs = [pl.BlockSpec((tm, D), lambda i: (i, 0))]
    out_shape = [jax.ShapeDtypeStruct((T, D), F32)]
    if not last:
        out_specs += [pl.BlockSpec((tm, D), lambda i: (i, 0)), pl.BlockSpec((G, tm), lambda i: (0, i))]
        out_shape += [jax.ShapeDtypeStruct((T, D), BF16), jax.ShapeDtypeStruct((G, T), F32)]
    return pl.pallas_call(
        functools.partial(_mlp_kernel, ff_chunk=1024, last=last),
        grid=(T // tm,),
        in_specs=[
            pl.BlockSpec((tm, D), lambda i: (i, 0)),
            resident((1, D)),
            layer((D, F)),
            layer((F, D)),
            resident((1, D)),
            resident((G, D)), resident((G, D)), resident((G, 1)),
        ],
        out_specs=out_specs,
        out_shape=out_shape,
        compiler_params=_params("parallel"),
        name="mlp",
    )(x, g, w_up, w_down, g_next, wg_hi, wg_lo, bg)


def _gate_params(w_in_t, l, b_if):
    off = 4 * ML_HEADS * ML_HEAD_DIM
    wg_hi, wg_lo = _split_bf16(w_in_t[l, off:off + 2 * ML_HEADS, :])
    return wg_hi, wg_lo, b_if.reshape(2 * ML_HEADS, 1)


def _layer(x, h, gates, p, nxt, *, batch, seq):
    D = D_MODEL
    ml_w = ML_HEADS * ML_HEAD_DIM
    wt, l = p["w_in"]
    pn = _proj(h, wt, l, (G_MK, G_SK), (ML_HEAD_DIM ** -0.5, 1.0), BF16)
    pt = _proj_t(h, wt, l, (G_MQ, G_MV, G_SQ, G_SV), (1.0, 1.0, SB_HEAD_DIM ** -0.5 * LOG2E, 1.0), BF16)
    pt_f = _proj_t(h, wt, l, (G_MO,), (1.0,), F32, sigmoid=True)

    gain = jnp.broadcast_to(p["ml_norm_g"].reshape(ml_w, 1), (ml_w, 128))
    y_mlT = _mlstm(pn, pt, pt_f, gates, gain, batch=batch, seq=seq,
                   k_col=0, qT_row=0, vT_row=1, oT_row=0)
    sb_unit = SB_GROUP * SB_HEAD_DIM
    y_sbT = _stick_breaking(pn, pt, batch=batch, seq=seq, k_col=ml_w // sb_unit,
                            qT_row=2 * ml_w // sb_unit, vT_row=(2 * ml_w + D) // sb_unit)
    x = _merge(x, h, wt, l, y_mlT, y_sbT, p["conv_w"],
               p["w_ml_proj"], p["w_sc_proj"], p["w_sb_proj"], p["w_out"], seq=seq)
    return _mlp(x, p["norm_mlp_g"].reshape(1, D), p["w_up"], p["w_down"], l,
                nxt["g"].reshape(1, D), nxt["wg_hi"], nxt["wg_lo"], nxt["bg"], last=nxt["last"])


def kernel(x, norm_mix_g, w_in, b_if, ml_norm_g, conv_w, w_ml_proj, w_sc_proj, w_sb_proj, w_out,
           norm_mlp_g, w_up, w_down, norm_final_g):
    batch, seq, D = x.shape
    depth = w_in.shape[0]
    assert D == D_MODEL and w_in.shape[2] == N_GROUPS * D + GATE_ROWS, (x.shape, w_in.shape)
    sb_block = min(SB_BLOCK, seq)
    seq_tiles = (min(ML_CHUNK, seq), sb_block * min(SB_QBLOCKS, seq // sb_block), min(TM_MERGE, seq))
    token_tiles = (min(TM_MLP, batch * seq), min(TM_PROJ, batch * seq))
    assert all(seq % t == 0 for t in seq_tiles) and all((batch * seq) % t == 0 for t in token_tiles)
    xt = x.reshape(batch * seq, D)
    w_in_t = jnp.swapaxes(w_in, 1, 2)
    gate_params = [_gate_params(w_in_t, l, b_if[l]) for l in range(depth)]
    h, gates = _prenorm(xt, norm_mix_g[0].reshape(1, D), *gate_params[0])
    wt = _regroup_w_in_t(w_in_t)
    stacked_bf16 = dict(w_ml_proj=w_ml_proj.astype(BF16), w_sc_proj=w_sc_proj.astype(BF16),
                        w_sb_proj=w_sb_proj.astype(BF16), w_out=w_out.astype(BF16),
                        w_up=w_up.astype(BF16), w_down=w_down.astype(BF16))
    for l in range(depth):
        p = dict(w_in=(wt, l), ml_norm_g=ml_norm_g[l], conv_w=conv_w[l], norm_mlp_g=norm_mlp_g[l],
                 **stacked_bf16)
        last = l == depth - 1
        nl = l if last else l + 1
        nxt = dict(g=norm_final_g if last else norm_mix_g[nl], wg_hi=gate_params[nl][0],
                   wg_lo=gate_params[nl][1], bg=gate_params[nl][2], last=last)
        outs = _layer(xt, h, gates, p, nxt, batch=batch, seq=seq)
        if last:
            xt = outs[0]
        else:
            xt, h, gates = outs
    return xt.reshape(batch, seq, D)
```

```python
import functools

import jax
import jax.numpy as jnp
from jax import lax
from jax.experimental import pallas as pl
from jax.experimental.pallas import tpu as pltpu

D_MODEL = 1024
ML_HEADS = 4
ML_HEAD_DIM = 256
SB_HEADS = 16
SB_HEAD_DIM = 64
EPS = 1e-6

VMEM_LIMIT_BYTES = 56 * 1024 * 1024

ML_CHUNK = 256
SB_BLOCK = 128
SB_GROUP = 4
SB_QBLOCKS = 4
SB_FUSED = 3
SB_STAGE_LAG = 6
LOG2E = 1.4426950408889634
SB_DEAD_LOG2 = -152.0
SB_ABSENT_LOG2 = -1e30
TM_PROJ = 2048
TM_MERGE = 512
TM_MLP = 1024
PREV_ROWS = 16

BF16 = jnp.bfloat16
F32 = jnp.float32


def _params(*sem):
    return pltpu.CompilerParams(dimension_semantics=sem, vmem_limit_bytes=VMEM_LIMIT_BYTES)


def _dot(a, b):
    return jnp.dot(a, b, preferred_element_type=F32)


def _dot_nt(a, b):
    return lax.dot_general(a, b, (((1,), (1,)), ((), ())), preferred_element_type=F32)


def _dot_tn(a, b):
    return lax.dot_general(a, b, (((0,), (0,)), ((), ())), preferred_element_type=F32)


def _rmsnorm_rows(x, g):
    ms = jnp.mean(x * x, axis=-1, keepdims=True)
    return x * lax.rsqrt(ms + EPS) * g


def _log_sigmoid(x):
    return jnp.minimum(x, 0.0) - jnp.log(1.0 + jnp.exp(-jnp.abs(x)))


def _split_bf16(x):
    hi = x.astype(BF16)
    lo = (x - hi.astype(F32)).astype(BF16)
    return hi, lo


def _prenorm_outputs(x, g, wg_hi, wg_lo, bg):
    h_hi, h_lo = _split_bf16(_rmsnorm_rows(x, g))
    gates = _dot_nt(wg_hi, h_hi) + (_dot_nt(wg_hi, h_lo) + _dot_nt(wg_lo, h_hi)) + bg
    return h_hi, gates


def _prenorm_kernel(x_ref, g_ref, wg_hi_ref, wg_lo_ref, bg_ref, h_ref, gates_ref):
    h_ref[...], gates_ref[...] = _prenorm_outputs(x_ref[...], g_ref[...], wg_hi_ref[...],
                                                  wg_lo_ref[...], bg_ref[...])


def _prenorm(x, g, wg_hi, wg_lo, bg):
    T, D = x.shape
    G = wg_hi.shape[0]
    tm = min(TM_PROJ, T)
    const = lambda i: (0, 0)
    return pl.pallas_call(
        _prenorm_kernel,
        grid=(T // tm,),
        in_specs=[
            pl.BlockSpec((tm, D), lambda i: (i, 0)),
            pl.BlockSpec((1, D), const),
            pl.BlockSpec((G, D), const),
            pl.BlockSpec((G, D), const),
            pl.BlockSpec((G, 1), const),
        ],
        out_specs=[pl.BlockSpec((tm, D), lambda i: (i, 0)), pl.BlockSpec((G, tm), lambda i: (0, i))],
        out_shape=[jax.ShapeDtypeStruct((T, D), BF16), jax.ShapeDtypeStruct((G, T), F32)],
        compiler_params=_params("parallel"),
        name="prenorm",
    )(x, g, wg_hi, wg_lo, bg)


N_GROUPS = 13
ALIGNED_GROUPS = 4
GATE_ROWS = 2 * ML_HEADS
(G_MQ, G_MK, G_MV, G_MO, G_B, G_C, G_U, G_SQ, G_SK, G_SV, G_GML, G_GSC, G_GSB) = range(N_GROUPS)


def _regroup_kernel(a_ref, b_ref, o_ref):
    k = pl.program_id(1)

    @pl.when(k < ALIGNED_GROUPS)
    def _():
        o_ref[...] = a_ref[...].astype(o_ref.dtype)

    @pl.when(k >= ALIGNED_GROUPS)
    def _():
        window = jnp.concatenate([a_ref[GATE_ROWS:, :], b_ref[...]], axis=0)
        o_ref[...] = window.astype(o_ref.dtype)


def _regroup_w_in_t(w_in_t):
    depth, _, D = w_in_t.shape
    return pl.pallas_call(
        _regroup_kernel,
        grid=(depth, N_GROUPS),
        in_specs=[
            pl.BlockSpec((None, D, D), lambda l, k: (l, k, 0)),
            pl.BlockSpec((None, GATE_ROWS, D), lambda l, k: (l, (k + 1) * (D // GATE_ROWS), 0)),
        ],
        out_specs=pl.BlockSpec((None, D, D), lambda l, k: (l, k, 0)),
        out_shape=jax.ShapeDtypeStruct((depth, N_GROUPS * D, D), BF16),
        compiler_params=_params("parallel", "parallel"),
        name="regroup_w_in",
    )(w_in_t, w_in_t)


def _pick(j, values):
    return sum(jnp.where(j == n, v, 0) for n, v in enumerate(values))


def _proj_kernel(scale_ref, h_ref, w_ref, o_ref):
    scale = scale_ref[pl.program_id(1)]
    o_ref[...] = (_dot_nt(h_ref[...], w_ref[...]) * scale).astype(o_ref.dtype)


def _proj(h, wt, l, groups, scales, out_dtype):
    T, D = h.shape
    tm = min(TM_PROJ, T)
    return pl.pallas_call(
        _proj_kernel,
        grid=(T // tm, len(groups)),
        in_specs=[
            pl.BlockSpec(memory_space=pltpu.SMEM),
            pl.BlockSpec((tm, D), lambda i, j: (i, 0)),
            pl.BlockSpec((None, D, D), lambda i, j: (l, _pick(j, groups), 0)),
        ],
        out_specs=pl.BlockSpec((tm, D), lambda i, j: (i, j)),
        out_shape=jax.ShapeDtypeStruct((T, len(groups) * D), out_dtype),
        compiler_params=_params("parallel", "parallel"),
        name="proj",
    )(jnp.array(scales, F32), h, wt)


def _proj_t_kernel(scale_ref, h_ref, w_ref, o_ref, *, sigmoid):
    out = _dot_nt(w_ref[...], h_ref[...]) * scale_ref[pl.program_id(1)]
    o_ref[...] = (jax.nn.sigmoid(out) if sigmoid else out).astype(o_ref.dtype)


def _proj_t(h, wt, l, groups, scales, out_dtype, sigmoid=False):
    T, D = h.shape
    tm = min(TM_PROJ, T)
    return pl.pallas_call(
        functools.partial(_proj_t_kernel, sigmoid=sigmoid),
        grid=(T // tm, len(groups)),
        in_specs=[
            pl.BlockSpec(memory_space=pltpu.SMEM),
            pl.BlockSpec((tm, D), lambda i, j: (i, 0)),
            pl.BlockSpec((None, D, D), lambda i, j: (l, _pick(j, groups), 0)),
        ],
        out_specs=pl.BlockSpec((D, tm), lambda i, j: (j, i)),
        out_shape=jax.ShapeDtypeStruct((len(groups) * D, T), out_dtype),
        compiler_params=_params("parallel", "parallel"),
        name="proj_t",
    )(jnp.array(scales, F32), h, wt)


def _row_to_columns(r):
    L = r.shape[1]
    col = jnp.transpose(jnp.broadcast_to(r, (128, L)))
    return jnp.concatenate([col] * (L // 128), axis=1)


def _mlstm_kernel(k_ref, qT_ref, vT_ref, oT_ref, gates_ref, gain_ref, y_ref, c_ref, n_ref, m_ref):
    @pl.when(pl.program_id(1) == 0)
    def _():
        c_ref[...] = jnp.zeros_like(c_ref)
        n_ref[...] = jnp.zeros_like(n_ref)
        m_ref[...] = jnp.zeros_like(m_ref)

    L = k_ref.shape[0]
    dh = ML_HEAD_DIM
    feat = lambda g: slice(g * dh, (g + 1) * dh)
    s_idx = lax.broadcasted_iota(jnp.int32, (L, L), 0)
    t_idx = lax.broadcasted_iota(jnp.int32, (L, L), 1)
    causal = s_idx <= t_idx

    gates = gates_ref[...]
    log_f = _log_sigmoid(gates)
    st = {}

    def state_products(g):
        k = k_ref[:, feat(g)]
        qT = qT_ref[feat(g), :]
        c_prev, n_prev = c_ref[g], n_ref[g]
        st[g, "kq"] = _dot(k, qT)
        st[g, "cq"] = _dot(c_prev.astype(BF16), qT)
        st[g, "nq"] = _dot(n_prev.astype(BF16), qT)[0:1, :]

    def intra_chunk(g):
        m_prev = m_ref[g]
        lf_col = _row_to_columns(log_f[ML_HEADS + g:ML_HEADS + g + 1, :])
        b_row = jnp.sum(jnp.where(causal, lf_col, 0.0), axis=0, keepdims=True)
        c_col = _row_to_columns(gates[g:g + 1, :] - b_row)
        c_max = jnp.max(jnp.where(causal, c_col, -jnp.inf), axis=0, keepdims=True)
        u_row = jnp.maximum(m_prev, c_max)
        s_w = st.pop((g, "kq")) * jnp.where(causal, jnp.exp(c_col - u_row), 0.0)
        a_inter = jnp.exp(m_prev - u_row)
        st[g, "den"] = jnp.sum(s_w, axis=0, keepdims=True) + a_inter * st.pop((g, "nq"))
        st[g, "num"] = _dot(vT_ref[feat(g), :], s_w.astype(BF16))
        st[g, "b_row"], st[g, "u_row"], st[g, "a_inter"] = b_row, u_row, a_inter

    def next_state(g):
        m_prev, b_row = m_ref[g], st[g, "b_row"]
        b_last = b_row[:, L - 1:L]
        g_row = b_last - b_row + gates[g:g + 1, :]
        m_new = jnp.maximum(b_last + m_prev, jnp.max(g_row, axis=1, keepdims=True))
        decay = jnp.exp(b_last + m_prev - m_new)
        w_state = jnp.exp(g_row - m_new)
        k = k_ref[:, feat(g)]
        vw = (vT_ref[feat(g), :].astype(F32) * w_state).astype(BF16)
        c_ref[g] = decay * c_ref[g] + _dot(vw, k)
        n_ref[g] = decay * n_ref[g] + _dot(jnp.broadcast_to(w_state, (8, L)).astype(BF16), k)
        m_ref[g] = m_new

    def output(g):
        floor = jnp.exp(-(st.pop((g, "b_row")) + st.pop((g, "u_row"))))
        num = st.pop((g, "num")) + st.pop((g, "a_inter")) * st.pop((g, "cq"))
        h = num / jnp.maximum(jnp.abs(st.pop((g, "den"))), floor)
        hg = oT_ref[feat(g), :] * h
        ms = jnp.mean(hg * hg, axis=0, keepdims=True)
        gain = jnp.concatenate([gain_ref[feat(g), :]] * (L // 128), axis=1)
        y_ref[feat(g), :] = (hg * lax.rsqrt(ms + EPS) * gain).astype(y_ref.dtype)

    for stage in (state_products, intra_chunk, next_state, output):
        for g in range(ML_HEADS):
            stage(g)


def _mlstm(pn, pt, pt_f, gates, gain, *, batch, seq, k_col, qT_row, vT_row, oT_row):
    T = batch * seq
    L = min(ML_CHUNK, seq)
    nc = seq // L
    dh = ML_HEAD_DIM
    W = ML_HEADS * dh
    tok = lambda b, c: b * nc + c
    return pl.pallas_call(
        _mlstm_kernel,
        grid=(batch, nc),
        in_specs=[
            pl.BlockSpec((L, W), lambda b, c: (tok(b, c), k_col)),
            pl.BlockSpec((W, L), lambda b, c: (qT_row, tok(b, c))),
            pl.BlockSpec((W, L), lambda b, c: (vT_row, tok(b, c))),
            pl.BlockSpec((W, L), lambda b, c: (oT_row, tok(b, c))),
            pl.BlockSpec((2 * ML_HEADS, L), lambda b, c: (0, tok(b, c))),
            pl.BlockSpec((W, 128), lambda b, c: (0, 0)),
        ],
        out_specs=pl.BlockSpec((W, L), lambda b, c: (0, tok(b, c))),
        out_shape=jax.ShapeDtypeStruct((W, T), BF16),
        scratch_shapes=[
            pltpu.VMEM((ML_HEADS, dh, dh), F32),
            pltpu.VMEM((ML_HEADS, 8, dh), F32),
            pltpu.VMEM((ML_HEADS, 1, 1), F32),
        ],
        compiler_params=_params("parallel", "arbitrary"),
        name="mlstm",
    )(pn, pt, pt, pt_f, gates, gain)


def _sb_kernel(qT_ref, k_ref, vT_ref, o_ref, acc_ref, *, blk, heads, qblocks):
    step = pl.program_id(2)
    d = SB_HEAD_DIM

    s_idx = lax.broadcasted_iota(jnp.int32, (blk, blk), 0)
    t_idx = lax.broadcasted_iota(jnp.int32, (blk, blk), 1)
    later = (t_idx > s_idx)
    from_here = jnp.where(t_idx >= s_idx, 1.0, 0.0).astype(BF16)

    row_head = lax.broadcasted_iota(jnp.int32, (2 * d, blk), 0) // d
    q_pads = []
    for qb in range(qblocks):
        q_pads.append([])
        for g in range(heads):
            q2 = qT_ref[(g // 2) * 2 * d:(g // 2 + 1) * 2 * d, qb * blk:(qb + 1) * blk]
            q_pads[qb].append(jnp.where(row_head == g % 2, q2, jnp.zeros_like(q2)))

    def neg_log2_1m(zs):
        return jnp.maximum(zs, 0.0) + jnp.log(1.0 + jnp.exp2(-jnp.abs(zs))) * LOG2E

    def visit(chains, rs, init_acc):
        starts = [pl.multiple_of(j * blk, blk) for _, j, _, _ in chains]
        units = [(c, g) for c in range(len(chains)) for g in range(heads)]
        rs = {qb: list(r) for qb, r in rs.items()}
        z, tail, started = {}, {}, set()

        def scores(c, g):
            qb = chains[c][0]
            z[c, g] = _dot(k_ref[pl.ds(starts[c], blk), (g // 2) * 2 * d:(g // 2 + 1) * 2 * d],
                           q_pads[qb][g])

        def tails(c, g):
            nl1 = neg_log2_1m(z[c, g])
            if chains[c][2]:
                nl1 = jnp.where(later, nl1, 0.0)
            tail[c, g] = _dot(from_here, nl1.astype(BF16))

        def weights(c, g):
            qb, _, diagonal, r_gate = chains[c]
            r_in = rs[qb][g] if r_gate is None else jnp.where(r_gate, rs[qb][g], SB_ABSENT_LOG2)
            rs[qb][g] = r_in - tail[c, g][0:1, :]
            a = jnp.exp2(z.pop((c, g)) - tail.pop((c, g)) + r_in)
            if diagonal:
                a = jnp.where(later, a, 0.0)
            contrib = _dot(vT_ref[g * d:(g + 1) * d, pl.ds(starts[c], blk)], a.astype(BF16))
            if init_acc and (qb, g) not in started:
                acc_ref[g * d:(g + 1) * d, qb * blk:(qb + 1) * blk] = contrib
            else:
                acc_ref[g * d:(g + 1) * d, qb * blk:(qb + 1) * blk] += contrib
            started.add((qb, g))

        stages = (scores, tails, weights)
        lag = SB_STAGE_LAG
        for n in range(len(units) + lag * (len(stages) - 1)):
            for k, stage in enumerate(stages):
                if 0 <= n - lag * k < len(units):
                    stage(*units[n - lag * k])
        return {qb: (jnp.max(functools.reduce(jnp.maximum, r)) > SB_DEAD_LOG2, tuple(r))
                for qb, r in rs.items()}

    blocks = [step * qblocks + qb for qb in range(qblocks)]
    chains = []
    for qb, i in enumerate(blocks):
        chains.append((qb, i, True, None))
        chains += [(qb, jnp.maximum(i - back, 0), False, i >= back) for back in range(1, SB_FUSED)]
    zeros = tuple(jnp.zeros((1, blk), F32) for _ in range(heads))
    first = visit(chains, {qb: zeros for qb in range(qblocks)}, True)

    for qb, i in enumerate(blocks):
        def cond(c, i=i):
            n, live, _ = c
            return jnp.logical_and(n < i - (SB_FUSED - 1), live)

        def body(c, qb=qb, i=i):
            n, _, rs = c
            return (n + 1,) + visit([(qb, i - SB_FUSED - n, False, None)], {qb: rs}, False)[qb]

        lax.while_loop(cond, body, (jnp.int32(0),) + first[qb])
    o_ref[...] = acc_ref[...].astype(o_ref.dtype)


def _stick_breaking(pn_b, pt, *, batch, seq, k_col, qT_row, vT_row):
    T = batch * seq
    blk = min(SB_BLOCK, seq)
    qblocks = min(SB_QBLOCKS, seq // blk)
    tq = qblocks * blk
    nq = seq // tq
    gd = SB_GROUP * SB_HEAD_DIM
    return pl.pallas_call(
        functools.partial(_sb_kernel, blk=blk, heads=SB_GROUP, qblocks=qblocks),
        grid=(batch, SB_HEADS // SB_GROUP, nq),
        in_specs=[
            pl.BlockSpec((gd, tq), lambda b, h, i: (qT_row + h, b * nq + i)),
            pl.BlockSpec((seq, gd), lambda b, h, i: (b, k_col + h)),
            pl.BlockSpec((gd, seq), lambda b, h, i: (vT_row + h, b)),
        ],
        out_specs=pl.BlockSpec((gd, tq), lambda b, h, i: (h, b * nq + i)),
        out_shape=jax.ShapeDtypeStruct((SB_HEADS * SB_HEAD_DIM, T), BF16),
        scratch_shapes=[pltpu.VMEM((gd, tq), F32)],
        compiler_params=_params("parallel", "parallel", "arbitrary"),
        name="stick_breaking",
    )(pt, pn_b, pt)


def _merge_kernel(x_ref, h_ref, hp_ref, wb_ref, wc_ref, wu_ref, wgml_ref, wgsc_ref, wgsb_ref,
                  ymlT_ref, ysbT_ref, cw_ref, wml_ref, wsc_ref, wsb_ref, wout_ref, o_ref, *,
                  tiles_per_seq):
    tm, D = x_ref.shape
    P = hp_ref.shape[0]
    first = (pl.program_id(0) % tiles_per_seq) == 0
    h = h_ref[...]
    h_ext = jnp.concatenate([hp_ref[...], h], axis=0)

    z = _dot_nt(h_ext, wc_ref[...]) * _dot_nt(h_ext, wu_ref[...])
    row = lax.broadcasted_iota(jnp.int32, z.shape, 0)
    z = jnp.where(jnp.logical_and(first, row < P), 0.0, z)
    cw = cw_ref[...]
    conv = (cw[0:1, :] * pltpu.roll(z, 2, axis=0)[P:, :] + cw[1:2, :] * pltpu.roll(z, 1, axis=0)[P:, :]
            + cw[2:3, :] * z[P:, :])
    y_sc = _dot_nt(h, wb_ref[...]) * conv

    merged = jax.nn.sigmoid(_dot_nt(h, wgml_ref[...])) * _dot_tn(ymlT_ref[...], wml_ref[...])
    merged += jax.nn.sigmoid(_dot_nt(h, wgsb_ref[...])) * _dot_tn(ysbT_ref[...], wsb_ref[...])
    merged += jax.nn.sigmoid(_dot_nt(h, wgsc_ref[...])) * _dot(y_sc.astype(BF16), wsc_ref[...])
    o_ref[...] = x_ref[...] + _dot(merged.astype(BF16), wout_ref[...])


def _merge(x, h, wt, l, y_mlT, y_sbT, conv_w, w_ml, w_sc, w_sb, w_out, *, seq):
    T, D = x.shape
    tm = min(TM_MERGE, seq)
    const = lambda i: (0, 0)
    resident = lambda shape: pl.BlockSpec(shape, const, pipeline_mode=pl.Buffered(1))
    group = lambda g: pl.BlockSpec((None, D, D), lambda i: (l, g, 0), pipeline_mode=pl.Buffered(1))
    layer = lambda shape: pl.BlockSpec((None,) + shape, lambda i: (l, 0, 0), pipeline_mode=pl.Buffered(1))
    return pl.pallas_call(
        functools.partial(_merge_kernel, tiles_per_seq=seq // tm),
        grid=(T // tm,),
        in_specs=[
            pl.BlockSpec((tm, D), lambda i: (i, 0)),
            pl.BlockSpec((tm, D), lambda i: (i, 0)),
            pl.BlockSpec((PREV_ROWS, D), lambda i: (jnp.maximum(i * (tm // PREV_ROWS) - 1, 0), 0)),
            group(G_B), group(G_C), group(G_U), group(G_GML), group(G_GSC), group(G_GSB),
            pl.BlockSpec((D, tm), lambda i: (0, i)),
            pl.BlockSpec((D, tm), lambda i: (0, i)),
            resident((3, D)),
            layer((D, D)), layer((D, D)), layer((D, D)), layer((D, D)),
        ],
        out_specs=pl.BlockSpec((tm, D), lambda i: (i, 0)),
        out_shape=jax.ShapeDtypeStruct((T, D), F32),
        compiler_params=_params("parallel"),
        name="merge",
    )(x, h, h, wt, wt, wt, wt, wt, wt, y_mlT, y_sbT, conv_w, w_ml, w_sc, w_sb, w_out)


def _mlp_kernel(x_ref, g_ref, wup_ref, wdown_ref, gn_ref, wg_hi_ref, wg_lo_ref, bg_ref, *out_refs,
                ff_chunk, last):
    x = x_ref[...]
    h = _rmsnorm_rows(x, g_ref[...]).astype(BF16)
    acc = x
    for c in range(wup_ref.shape[1] // ff_chunk):
        cols = slice(c * ff_chunk, (c + 1) * ff_chunk)
        up = jnp.maximum(_dot(h, wup_ref[:, cols]), 0.0)
        acc = acc + _dot((up * up).astype(BF16), wdown_ref[cols, :])
    if last:
        out_refs[0][...] = _rmsnorm_rows(acc, gn_ref[...])
    else:
        out_refs[0][...] = acc
        out_refs[1][...], out_refs[2][...] = _prenorm_outputs(acc, gn_ref[...], wg_hi_ref[...],
                                                              wg_lo_ref[...], bg_ref[...])


def _mlp(x, g, w_up, w_down, l, g_next, wg_hi, wg_lo, bg, *, last):
    T, D = x.shape
    F = w_up.shape[2]
    G = wg_hi.shape[0]
    tm = min(TM_MLP, T)
    const = lambda i: (0, 0)
    resident = lambda shape: pl.BlockSpec(shape, const, pipeline_mode=pl.Buffered(1))
    layer = lambda shape: pl.BlockSpec((None,) + shape, lambda i: (l, 0, 0), pipeline_mode=pl.Buffered(1))
    out_specs = [pl.BlockSpec((tm, D), lambda i: (i, 0))]
    out_shape = [jax.ShapeDtypeStruct((T, D), F32)]
    if not last:
        out_specs += [pl.BlockSpec((tm, D), lambda i: (i, 0)), pl.BlockSpec((G, tm), lambda i: (0, i))]
        out_shape += [jax.ShapeDtypeStruct((T, D), BF16), jax.ShapeDtypeStruct((G, T), F32)]
    return pl.pallas_call(
        functools.partial(_mlp_kernel, ff_chunk=1024, last=last),
        grid=(T // tm,),
        in_specs=[
            pl.BlockSpec((tm, D), lambda i: (i, 0)),
            resident((1, D)),
            layer((D, F)),
            layer((F, D)),
            resident((1, D)),
            resident((G, D)), resident((G, D)), resident((G, 1)),
        ],
        out_specs=out_specs,
        out_shape=out_shape,
        compiler_params=_params("parallel"),
        name="mlp",
    )(x, g, w_up, w_down, g_next, wg_hi, wg_lo, bg)


def _gate_params(w_in_t, l, b_if):
    off = 4 * ML_HEADS * ML_HEAD_DIM
    wg_hi, wg_lo = _split_bf16(w_in_t[l, off:off + 2 * ML_HEADS, :])
    return wg_hi, wg_lo, b_if.reshape(2 * ML_HEADS, 1)


def _layer(x, h, gates, p, nxt, *, batch, seq):
    D = D_MODEL
    ml_w = ML_HEADS * ML_HEAD_DIM
    wt, l = p["w_in"]
    pn = _proj(h, wt, l, (G_MK, G_SK), (ML_HEAD_DIM ** -0.5, 1.0), BF16)
    pt = _proj_t(h, wt, l, (G_MQ, G_MV, G_SQ, G_SV), (1.0, 1.0, SB_HEAD_DIM ** -0.5 * LOG2E, 1.0), BF16)
    pt_f = _proj_t(h, wt, l, (G_MO,), (1.0,), F32, sigmoid=True)

    gain = jnp.broadcast_to(p["ml_norm_g"].reshape(ml_w, 1), (ml_w, 128))
    y_mlT = _mlstm(pn, pt, pt_f, gates, gain, batch=batch, seq=seq,
                   k_col=0, qT_row=0, vT_row=1, oT_row=0)
    sb_unit = SB_GROUP * SB_HEAD_DIM
    y_sbT = _stick_breaking(pn, pt, batch=batch, seq=seq, k_col=ml_w // sb_unit,
                            qT_row=2 * ml_w // sb_unit, vT_row=(2 * ml_w + D) // sb_unit)
    x = _merge(x, h, wt, l, y_mlT, y_sbT, p["conv_w"],
               p["w_ml_proj"], p["w_sc_proj"], p["w_sb_proj"], p["w_out"], seq=seq)
    return _mlp(x, p["norm_mlp_g"].reshape(1, D), p["w_up"], p["w_down"], l,
                nxt["g"].reshape(1, D), nxt["wg_hi"], nxt["wg_lo"], nxt["bg"], last=nxt["last"])


def kernel(x, norm_mix_g, w_in, b_if, ml_norm_g, conv_w, w_ml_proj, w_sc_proj, w_sb_proj, w_out,
           norm_mlp_g, w_up, w_down, norm_final_g):
    batch, seq, D = x.shape
    depth = w_in.shape[0]
    assert D == D_MODEL and w_in.shape[2] == N_GROUPS * D + GATE_ROWS, (x.shape, w_in.shape)
    sb_block = min(SB_BLOCK, seq)
    seq_tiles = (min(ML_CHUNK, seq), sb_block * min(SB_QBLOCKS, seq // sb_block), min(TM_MERGE, seq))
    token_tiles = (min(TM_MLP, batch * seq), min(TM_PROJ, batch * seq))
    assert all(seq % t == 0 for t in seq_tiles) and all((batch * seq) % t == 0 for t in token_tiles)
    xt = x.reshape(batch * seq, D)
    w_in_t = jnp.swapaxes(w_in, 1, 2)
    gate_params = [_gate_params(w_in_t, l, b_if[l]) for l in range(depth)]
    h, gates = _prenorm(xt, norm_mix_g[0].reshape(1, D), *gate_params[0])
    wt = _regroup_w_in_t(w_in_t)
    stacked_bf16 = dict(w_ml_proj=w_ml_proj.astype(BF16), w_sc_proj=w_sc_proj.astype(BF16),
                        w_sb_proj=w_sb_proj.astype(BF16), w_out=w_out.astype(BF16),
                        w_up=w_up.astype(BF16), w_down=w_down.astype(BF16))
    for l in range(depth):
        p = dict(w_in=(wt, l), ml_norm_g=ml_norm_g[l], conv_w=conv_w[l], norm_mlp_g=norm_mlp_g[l],
                 **stacked_bf16)
        last = l == depth - 1
        nl = l if last else l + 1
        nxt = dict(g=norm_final_g if last else norm_mix_g[nl], wg_hi=gate_params[nl][0],
                   wg_lo=gate_params[nl][1], bg=gate_params[nl][2], last=last)
        outs = _layer(xt, h, gates, p, nxt, batch=batch, seq=seq)
        if last:
            xt = outs[0]
        else:
            xt, h, gates = outs
    return xt.reshape(batch, seq, D)
```

```python
import functools

import jax
import jax.numpy as jnp
from jax import lax
from jax.experimental import pallas as pl
from jax.experimental.pallas import tpu as pltpu

D_MODEL = 1024
ML_HEADS = 4
ML_HEAD_DIM = 256
SB_HEADS = 16
SB_HEAD_DIM = 64
EPS = 1e-6

VMEM_LIMIT_BYTES = 56 * 1024 * 1024

ML_CHUNK = 256
SB_BLOCK = 128
SB_GROUP = 4
SB_QBLOCKS = 8
SB_FUSED = 3
SB_STAGE_LAG = 6
LOG2E = 1.4426950408889634
SB_DEAD_LOG2 = -152.0
SB_ABSENT_LOG2 = -1e30
TM_PROJ = 2048
TM_MERGE = 512
TM_MLP = 1024
PREV_ROWS = 16

BF16 = jnp.bfloat16
F32 = jnp.float32


def _params(*sem):
    return pltpu.CompilerParams(dimension_semantics=sem, vmem_limit_bytes=VMEM_LIMIT_BYTES)


def _dot(a, b):
    return jnp.dot(a, b, preferred_element_type=F32)


def _dot_nt(a, b):
    return lax.dot_general(a, b, (((1,), (1,)), ((), ())), preferred_element_type=F32)


def _dot_tn(a, b):
    return lax.dot_general(a, b, (((0,), (0,)), ((), ())), preferred_element_type=F32)


def _rmsnorm_rows(x, g):
    ms = jnp.mean(x * x, axis=-1, keepdims=True)
    return x * lax.rsqrt(ms + EPS) * g


def _log_sigmoid(x):
    return jnp.minimum(x, 0.0) - jnp.log(1.0 + jnp.exp(-jnp.abs(x)))


def _split_bf16(x):
    hi = x.astype(BF16)
    lo = (x - hi.astype(F32)).astype(BF16)
    return hi, lo


def _prenorm_outputs(x, g, wg_hi, wg_lo, bg):
    h_hi, h_lo = _split_bf16(_rmsnorm_rows(x, g))
    gates = _dot_nt(wg_hi, h_hi) + (_dot_nt(wg_hi, h_lo) + _dot_nt(wg_lo, h_hi)) + bg
    return h_hi, gates


def _prenorm_kernel(x_ref, g_ref, wg_hi_ref, wg_lo_ref, bg_ref, h_ref, gates_ref):
    h_ref[...], gates_ref[...] = _prenorm_outputs(x_ref[...], g_ref[...], wg_hi_ref[...],
                                                  wg_lo_ref[...], bg_ref[...])


def _prenorm(x, g, wg_hi, wg_lo, bg):
    T, D = x.shape
    G = wg_hi.shape[0]
    tm = min(TM_PROJ, T)
    const = lambda i: (0, 0)
    return pl.pallas_call(
        _prenorm_kernel,
        grid=(T // tm,),
        in_specs=[
            pl.BlockSpec((tm, D), lambda i: (i, 0)),
            pl.BlockSpec((1, D), const),
            pl.BlockSpec((G, D), const),
            pl.BlockSpec((G, D), const),
            pl.BlockSpec((G, 1), const),
        ],
        out_specs=[pl.BlockSpec((tm, D), lambda i: (i, 0)), pl.BlockSpec((G, tm), lambda i: (0, i))],
        out_shape=[jax.ShapeDtypeStruct((T, D), BF16), jax.ShapeDtypeStruct((G, T), F32)],
        compiler_params=_params("parallel"),
        name="prenorm",
    )(x, g, wg_hi, wg_lo, bg)


N_GROUPS = 13
ALIGNED_GROUPS = 4
GATE_ROWS = 2 * ML_HEADS
(G_MQ, G_MK, G_MV, G_MO, G_B, G_C, G_U, G_SQ, G_SK, G_SV, G_GML, G_GSC, G_GSB) = range(N_GROUPS)


def _regroup_kernel(a_ref, b_ref, o_ref):
    k = pl.program_id(1)

    @pl.when(k < ALIGNED_GROUPS)
    def _():
        o_ref[...] = a_ref[...].astype(o_ref.dtype)

    @pl.when(k >= ALIGNED_GROUPS)
    def _():
        window = jnp.concatenate([a_ref[GATE_ROWS:, :], b_ref[...]], axis=0)
        o_ref[...] = window.astype(o_ref.dtype)


def _regroup_w_in_t(w_in_t):
    depth, _, D = w_in_t.shape
    return pl.pallas_call(
        _regroup_kernel,
        grid=(depth, N_GROUPS),
        in_specs=[
            pl.BlockSpec((None, D, D), lambda l, k: (l, k, 0)),
            pl.BlockSpec((None, GATE_ROWS, D), lambda l, k: (l, (k + 1) * (D // GATE_ROWS), 0)),
        ],
        out_specs=pl.BlockSpec((None, D, D), lambda l, k: (l, k, 0)),
        out_shape=jax.ShapeDtypeStruct((depth, N_GROUPS * D, D), BF16),
        compiler_params=_params("parallel", "parallel"),
        name="regroup_w_in",
    )(w_in_t, w_in_t)


def _pick(j, values):
    return sum(jnp.where(j == n, v, 0) for n, v in enumerate(values))


def _proj_kernel(scale_ref, h_ref, w_ref, o_ref):
    scale = scale_ref[pl.program_id(1)]
    o_ref[...] = (_dot_nt(h_ref[...], w_ref[...]) * scale).astype(o_ref.dtype)


def _proj(h, wt, l, groups, scales, out_dtype):
    T, D = h.shape
    tm = min(TM_PROJ, T)
    return pl.pallas_call(
        _proj_kernel,
        grid=(T // tm, len(groups)),
        in_specs=[
            pl.BlockSpec(memory_space=pltpu.SMEM),
            pl.BlockSpec((tm, D), lambda i, j: (i, 0)),
            pl.BlockSpec((None, D, D), lambda i, j: (l, _pick(j, groups), 0)),
        ],
        out_specs=pl.BlockSpec((tm, D), lambda i, j: (i, j)),
        out_shape=jax.ShapeDtypeStruct((T, len(groups) * D), out_dtype),
        compiler_params=_params("parallel", "parallel"),
        name="proj",
    )(jnp.array(scales, F32), h, wt)


def _proj_t_kernel(scale_ref, h_ref, w_ref, o_ref, *, sigmoid):
    out = _dot_nt(w_ref[...], h_ref[...]) * scale_ref[pl.program_id(1)]
    o_ref[...] = (jax.nn.sigmoid(out) if sigmoid else out).astype(o_ref.dtype)


def _proj_t(h, wt, l, groups, scales, out_dtype, sigmoid=False):
    T, D = h.shape
    tm = min(TM_PROJ, T)
    return pl.pallas_call(
        functools.partial(_proj_t_kernel, sigmoid=sigmoid),
        grid=(T // tm, len(groups)),
        in_specs=[
            pl.BlockSpec(memory_space=pltpu.SMEM),
            pl.BlockSpec((tm, D), lambda i, j: (i, 0)),
            pl.BlockSpec((None, D, D), lambda i, j: (l, _pick(j, groups), 0)),
        ],
        out_specs=pl.BlockSpec((D, tm), lambda i, j: (j, i)),
        out_shape=jax.ShapeDtypeStruct((len(groups) * D, T), out_dtype),
        compiler_params=_params("parallel", "parallel"),
        name="proj_t",
    )(jnp.array(scales, F32), h, wt)


def _row_to_columns(r):
    L = r.shape[1]
    col = jnp.transpose(jnp.broadcast_to(r, (128, L)))
    return jnp.concatenate([col] * (L // 128), axis=1)


def _mlstm_kernel(k_ref, qT_ref, vT_ref, oT_ref, gates_ref, gain_ref, y_ref, c_ref, n_ref, m_ref):
    @pl.when(pl.program_id(1) == 0)
    def _():
        c_ref[...] = jnp.zeros_like(c_ref)
        n_ref[...] = jnp.zeros_like(n_ref)
        m_ref[...] = jnp.zeros_like(m_ref)

    L = k_ref.shape[0]
    dh = ML_HEAD_DIM
    feat = lambda g: slice(g * dh, (g + 1) * dh)
    s_idx = lax.broadcasted_iota(jnp.int32, (L, L), 0)
    t_idx = lax.broadcasted_iota(jnp.int32, (L, L), 1)
    causal = s_idx <= t_idx

    gates = gates_ref[...]
    log_f = _log_sigmoid(gates)
    st = {}

    def state_products(g):
        k = k_ref[:, feat(g)]
        qT = qT_ref[feat(g), :]
        c_prev, n_prev = c_ref[g], n_ref[g]
        st[g, "kq"] = _dot(k, qT)
        st[g, "cq"] = _dot(c_prev.astype(BF16), qT)
        st[g, "nq"] = _dot(n_prev.astype(BF16), qT)[0:1, :]

    def intra_chunk(g):
        m_prev = m_ref[g]
        lf_col = _row_to_columns(log_f[ML_HEADS + g:ML_HEADS + g + 1, :])
        b_row = jnp.sum(jnp.where(causal, lf_col, 0.0), axis=0, keepdims=True)
        c_col = _row_to_columns(gates[g:g + 1, :] - b_row)
        c_max = jnp.max(jnp.where(causal, c_col, -jnp.inf), axis=0, keepdims=True)
        u_row = jnp.maximum(m_prev, c_max)
        s_w = st.pop((g, "kq")) * jnp.where(causal, jnp.exp(c_col - u_row), 0.0)
        a_inter = jnp.exp(m_prev - u_row)
        st[g, "den"] = jnp.sum(s_w, axis=0, keepdims=True) + a_inter * st.pop((g, "nq"))
        st[g, "num"] = _dot(vT_ref[feat(g), :], s_w.astype(BF16))
        st[g, "b_row"], st[g, "u_row"], st[g, "a_inter"] = b_row, u_row, a_inter

    def next_state(g):
        m_prev, b_row = m_ref[g], st[g, "b_row"]
        b_last = b_row[:, L - 1:L]
        g_row = b_last - b_row + gates[g:g + 1, :]
        m_new = jnp.maximum(b_last + m_prev, jnp.max(g_row, axis=1, keepdims=True))
        decay = jnp.exp(b_last + m_prev - m_new)
        w_state = jnp.exp(g_row - m_new)
        k = k_ref[:, feat(g)]
        vw = (vT_ref[feat(g), :].astype(F32) * w_state).astype(BF16)
        c_ref[g] = decay * c_ref[g] + _dot(vw, k)
        n_ref[g] = decay * n_ref[g] + _dot(jnp.broadcast_to(w_state, (8, L)).astype(BF16), k)
        m_ref[g] = m_new

    def output(g):
        floor = jnp.exp(-(st.pop((g, "b_row")) + st.pop((g, "u_row"))))
        num = st.pop((g, "num")) + st.pop((g, "a_inter")) * st.pop((g, "cq"))
        h = num / jnp.maximum(jnp.abs(st.pop((g, "den"))), floor)
        hg = oT_ref[feat(g), :] * h
        ms = jnp.mean(hg * hg, axis=0, keepdims=True)
        gain = jnp.concatenate([gain_ref[feat(g), :]] * (L // 128), axis=1)
        y_ref[feat(g), :] = (hg * lax.rsqrt(ms + EPS) * gain).astype(y_ref.dtype)

    for stage in (state_products, intra_chunk, next_state, output):
        for g in range(ML_HEADS):
            stage(g)


def _mlstm(pn, pt, pt_f, gates, gain, *, batch, seq, k_col, qT_row, vT_row, oT_row):
    T = batch * seq
    L = min(ML_CHUNK, seq)
    nc = seq // L
    dh = ML_HEAD_DIM
    W = ML_HEADS * dh
    tok = lambda b, c: b * nc + c
    return pl.pallas_call(
        _mlstm_kernel,
        grid=(batch, nc),
        in_specs=[
            pl.BlockSpec((L, W), lambda b, c: (tok(b, c), k_col)),
            pl.BlockSpec((W, L), lambda b, c: (qT_row, tok(b, c))),
            pl.BlockSpec((W, L), lambda b, c: (vT_row, tok(b, c))),
            pl.BlockSpec((W, L), lambda b, c: (oT_row, tok(b, c))),
            pl.BlockSpec((2 * ML_HEADS, L), lambda b, c: (0, tok(b, c))),
            pl.BlockSpec((W, 128), lambda b, c: (0, 0)),
        ],
        out_specs=pl.BlockSpec((W, L), lambda b, c: (0, tok(b, c))),
        out_shape=jax.ShapeDtypeStruct((W, T), BF16),
        scratch_shapes=[
            pltpu.VMEM((ML_HEADS, dh, dh), F32),
            pltpu.VMEM((ML_HEADS, 8, dh), F32),
            pltpu.VMEM((ML_HEADS, 1, 1), F32),
        ],
        compiler_params=_params("parallel", "arbitrary"),
        name="mlstm",
    )(pn, pt, pt, pt_f, gates, gain)


def _sb_kernel(qT_ref, k_ref, vT_ref, o_ref, acc_ref, *, blk, heads, qblocks):
    step = pl.program_id(2)
    d = SB_HEAD_DIM

    s_idx = lax.broadcasted_iota(jnp.int32, (blk, blk), 0)
    t_idx = lax.broadcasted_iota(jnp.int32, (blk, blk), 1)
    later = (t_idx > s_idx)
    from_here = jnp.where(t_idx >= s_idx, 1.0, 0.0).astype(BF16)

    row_head = lax.broadcasted_iota(jnp.int32, (2 * d, blk), 0) // d
    q_pads = []
    for qb in range(qblocks):
        q_pads.append([])
        for g in range(heads):
            q2 = qT_ref[(g // 2) * 2 * d:(g // 2 + 1) * 2 * d, qb * blk:(qb + 1) * blk]
            q_pads[qb].append(jnp.where(row_head == g % 2, q2, jnp.zeros_like(q2)))

    def neg_log2_1m(zs):
        return jnp.maximum(zs, 0.0) + jnp.log(1.0 + jnp.exp2(-jnp.abs(zs))) * LOG2E

    def visit(chains, rs, init_acc):
        starts = [pl.multiple_of(j * blk, blk) for _, j, _, _ in chains]
        units = [(c, g) for c in range(len(chains)) for g in range(heads)]
        rs = {qb: list(r) for qb, r in rs.items()}
        z, tail, started = {}, {}, set()

        def scores(c, g):
            qb = chains[c][0]
            z[c, g] = _dot(k_ref[pl.ds(starts[c], blk), (g // 2) * 2 * d:(g // 2 + 1) * 2 * d],
                           q_pads[qb][g])

        def tails(c, g):
            nl1 = neg_log2_1m(z[c, g])
            if chains[c][2]:
                nl1 = jnp.where(later, nl1, 0.0)
            tail[c, g] = _dot(from_here, nl1.astype(BF16))

        def weights(c, g):
            qb, _, diagonal, r_gate = chains[c]
            r_in = rs[qb][g] if r_gate is None else jnp.where(r_gate, rs[qb][g], SB_ABSENT_LOG2)
            rs[qb][g] = r_in - tail[c, g][0:1, :]
            a = jnp.exp2(z.pop((c, g)) - tail.pop((c, g)) + r_in)
            if diagonal:
                a = jnp.where(later, a, 0.0)
            contrib = _dot(vT_ref[g * d:(g + 1) * d, pl.ds(starts[c], blk)], a.astype(BF16))
            if init_acc and (qb, g) not in started:
                acc_ref[g * d:(g + 1) * d, qb * blk:(qb + 1) * blk] = contrib
            else:
                acc_ref[g * d:(g + 1) * d, qb * blk:(qb + 1) * blk] += contrib
            started.add((qb, g))

        stages = (scores, tails, weights)
        lag = SB_STAGE_LAG
        for n in range(len(units) + lag * (len(stages) - 1)):
            for k, stage in enumerate(stages):
                if 0 <= n - lag * k < len(units):
                    stage(*units[n - lag * k])
        return {qb: (jnp.max(functools.reduce(jnp.maximum, r)) > SB_DEAD_LOG2, tuple(r))
                for qb, r in rs.items()}

    blocks = [step * qblocks + qb for qb in range(qblocks)]
    chains = []
    for qb, i in enumerate(blocks):
        chains.append((qb, i, True, None))
        chains += [(qb, jnp.maximum(i - back, 0), False, i >= back) for back in range(1, SB_FUSED)]
    zeros = tuple(jnp.zeros((1, blk), F32) for _ in range(heads))
    first = visit(chains, {qb: zeros for qb in range(qblocks)}, True)

    for qb, i in enumerate(blocks):
        def cond(c, i=i):
            n, live, _ = c
            return jnp.logical_and(n < i - (SB_FUSED - 1), live)

        def body(c, qb=qb, i=i):
            n, _, rs = c
            return (n + 1,) + visit([(qb, i - SB_FUSED - n, False, None)], {qb: rs}, False)[qb]

        lax.while_loop(cond, body, (jnp.int32(0),) + first[qb])
    o_ref[...] = acc_ref[...].astype(o_ref.dtype)


def _stick_breaking(pn_b, pt, *, batch, seq, k_col, qT_row, vT_row):
    T = batch * seq
    blk = min(SB_BLOCK, seq)
    qblocks = min(SB_QBLOCKS, seq // blk)
    tq = qblocks * blk
    nq = seq // tq
    gd = SB_GROUP * SB_HEAD_DIM
    return pl.pallas_call(
        functools.partial(_sb_kernel, blk=blk, heads=SB_GROUP, qblocks=qblocks),
        grid=(batch, SB_HEADS // SB_GROUP, nq),
        in_specs=[
            pl.BlockSpec((gd, tq), lambda b, h, i: (qT_row + h, b * nq + i)),
            pl.BlockSpec((seq, gd), lambda b, h, i: (b, k_col + h)),
            pl.BlockSpec((gd, seq), lambda b, h, i: (vT_row + h, b)),
        ],
        out_specs=pl.BlockSpec((gd, tq), lambda b, h, i: (h, b * nq + i)),
        out_shape=jax.ShapeDtypeStruct((SB_HEADS * SB_HEAD_DIM, T), BF16),
        scratch_shapes=[pltpu.VMEM((gd, tq), F32)],
        compiler_params=_params("parallel", "parallel", "arbitrary"),
        name="stick_breaking",
    )(pt, pn_b, pt)


def _merge_kernel(x_ref, h_ref, hp_ref, wb_ref, wc_ref, wu_ref, wgml_ref, wgsc_ref, wgsb_ref,
                  ymlT_ref, ysbT_ref, cw_ref, wml_ref, wsc_ref, wsb_ref, wout_ref, o_ref, *,
                  tiles_per_seq):
    tm, D = x_ref.shape
    P = hp_ref.shape[0]
    first = (pl.program_id(0) % tiles_per_seq) == 0
    h = h_ref[...]
    h_ext = jnp.concatenate([hp_ref[...], h], axis=0)

    z = _dot_nt(h_ext, wc_ref[...]) * _dot_nt(h_ext, wu_ref[...])
    row = lax.broadcasted_iota(jnp.int32, z.shape, 0)
    z = jnp.where(jnp.logical_and(first, row < P), 0.0, z)
    cw = cw_ref[...]
    conv = (cw[0:1, :] * pltpu.roll(z, 2, axis=0)[P:, :] + cw[1:2, :] * pltpu.roll(z, 1, axis=0)[P:, :]
            + cw[2:3, :] * z[P:, :])
    y_sc = _dot_nt(h, wb_ref[...]) * conv

    merged = jax.nn.sigmoid(_dot_nt(h, wgml_ref[...])) * _dot_tn(ymlT_ref[...], wml_ref[...])
    merged += jax.nn.sigmoid(_dot_nt(h, wgsb_ref[...])) * _dot_tn(ysbT_ref[...], wsb_ref[...])
    merged += jax.nn.sigmoid(_dot_nt(h, wgsc_ref[...])) * _dot(y_sc.astype(BF16), wsc_ref[...])
    o_ref[...] = x_ref[...] + _dot(merged.astype(BF16), wout_ref[...])


def _merge(x, h, wt, l, y_mlT, y_sbT, conv_w, w_ml, w_sc, w_sb, w_out, *, seq):
    T, D = x.shape
    tm = min(TM_MERGE, seq)
    const = lambda i: (0, 0)
    resident = lambda shape: pl.BlockSpec(shape, const, pipeline_mode=pl.Buffered(1))
    group = lambda g: pl.BlockSpec((None, D, D), lambda i: (l, g, 0), pipeline_mode=pl.Buffered(1))
    layer = lambda shape: pl.BlockSpec((None,) + shape, lambda i: (l, 0, 0), pipeline_mode=pl.Buffered(1))
    return pl.pallas_call(
        functools.partial(_merge_kernel, tiles_per_seq=seq // tm),
        grid=(T // tm,),
        in_specs=[
            pl.BlockSpec((tm, D), lambda i: (i, 0)),
            pl.BlockSpec((tm, D), lambda i: (i, 0)),
            pl.BlockSpec((PREV_ROWS, D), lambda i: (jnp.maximum(i * (tm // PREV_ROWS) - 1, 0), 0)),
            group(G_B), group(G_C), group(G_U), group(G_GML), group(G_GSC), group(G_GSB),
            pl.BlockSpec((D, tm), lambda i: (0, i)),
            pl.BlockSpec((D, tm), lambda i: (0, i)),
            resident((3, D)),
            layer((D, D)), layer((D, D)), layer((D, D)), layer((D, D)),
        ],
        out_specs=pl.BlockSpec((tm, D), lambda i: (i, 0)),
        out_shape=jax.ShapeDtypeStruct((T, D), F32),
        compiler_params=_params("parallel"),
        name="merge",
    )(x, h, h, wt, wt, wt, wt, wt, wt, y_mlT, y_sbT, conv_w, w_ml, w_sc, w_sb, w_out)


def _mlp_kernel(x_ref, g_ref, wup_ref, wdown_ref, gn_ref, wg_hi_ref, wg_lo_ref, bg_ref, *out_refs,
                ff_chunk, last):
    x = x_ref[...]
    h = _rmsnorm_rows(x, g_ref[...]).astype(BF16)
    acc = x
    for c in range(wup_ref.shape[1] // ff_chunk):
        cols = slice(c * ff_chunk, (c + 1) * ff_chunk)
        up = jnp.maximum(_dot(h, wup_ref[:, cols]), 0.0)
        acc = acc + _dot((up * up).astype(BF16), wdown_ref[cols, :])
    if last:
        out_refs[0][...] = _rmsnorm_rows(acc, gn_ref[...])
    else:
        out_refs[0][...] = acc
        out_refs[1][...], out_refs[2][...] = _prenorm_outputs(acc, gn_ref[...], wg_hi_ref[...],
                                                              wg_lo_ref[...], bg_ref[...])


def _mlp(x, g, w_up, w_down, l, g_next, wg_hi, wg_lo, bg, *, last):
    T, D = x.shape
    F = w_up.shape[2]
    G = wg_hi.shape[0]
    tm = min(TM_MLP, T)
    const = lambda i: (0, 0)
    resident = lambda shape: pl.BlockSpec(shape, const, pipeline_mode=pl.Buffered(1))
    layer = lambda shape: pl.BlockSpec((None,) + shape, lambda i: (l, 0, 0), pipeline_mode=pl.Buffered(1))
    out_specs = [pl.BlockSpec((tm, D), lambda i: (i, 0))]
    out_shape = [jax.ShapeDtypeStruct((T, D), F32)]
    if not last:
        out_specs += [pl.BlockSpec((tm, D), lambda i: (i, 0)), pl.BlockSpec((G, tm), lambda i: (0, i))]
        out_shape += [jax.ShapeDtypeStruct((T, D), BF16), jax.ShapeDtypeStruct((G, T), F32)]
    return pl.pallas_call(
        functools.partial(_mlp_kernel, ff_chunk=1024, last=last),
        grid=(T // tm,),
        in_specs=[
            pl.BlockSpec((tm, D), lambda i: (i, 0)),
            resident((1, D)),
            layer((D, F)),
            layer((F, D)),
            resident((1, D)),
            resident((G, D)), resident((G, D)), resident((G, 1)),
        ],
        out_specs=out_specs,
        out_shape=out_shape,
        compiler_params=_params("parallel"),
        name="mlp",
    )(x, g, w_up, w_down, g_next, wg_hi, wg_lo, bg)


def _gate_params(w_in_t, l, b_if):
    off = 4 * ML_HEADS * ML_HEAD_DIM
    wg_hi, wg_lo = _split_bf16(w_in_t[l, off:off + 2 * ML_HEADS, :])
    return wg_hi, wg_lo, b_if.reshape(2 * ML_HEADS, 1)


def _layer(x, h, gates, p, nxt, *, batch, seq):
    D = D_MODEL
    ml_w = ML_HEADS * ML_HEAD_DIM
    wt, l = p["w_in"]
    pn = _proj(h, wt, l, (G_MK, G_SK), (ML_HEAD_DIM ** -0.5, 1.0), BF16)
    pt = _proj_t(h, wt, l, (G_MQ, G_MV, G_SQ, G_SV), (1.0, 1.0, SB_HEAD_DIM ** -0.5 * LOG2E, 1.0), BF16)
    pt_f = _proj_t(h, wt, l, (G_MO,), (1.0,), F32, sigmoid=True)

    gain = jnp.broadcast_to(p["ml_norm_g"].reshape(ml_w, 1), (ml_w, 128))
    y_mlT = _mlstm(pn, pt, pt_f, gates, gain, batch=batch, seq=seq,
                   k_col=0, qT_row=0, vT_row=1, oT_row=0)
    sb_unit = SB_GROUP * SB_HEAD_DIM
    y_sbT = _stick_breaking(pn, pt, batch=batch, seq=seq, k_col=ml_w // sb_unit,
                            qT_row=2 * ml_w // sb_unit, vT_row=(2 * ml_w + D) // sb_unit)
    x = _merge(x, h, wt, l, y_mlT, y_sbT, p["conv_w"],
               p["w_ml_proj"], p["w_sc_proj"], p["w_sb_proj"], p["w_out"], seq=seq)
    return _mlp(x, p["norm_mlp_g"].reshape(1, D), p["w_up"], p["w_down"], l,
                nxt["g"].reshape(1, D), nxt["wg_hi"], nxt["wg_lo"], nxt["bg"], last=nxt["last"])


def kernel(x, norm_mix_g, w_in, b_if, ml_norm_g, conv_w, w_ml_proj, w_sc_proj, w_sb_proj, w_out,
           norm_mlp_g, w_up, w_down, norm_final_g):
    batch, seq, D = x.shape
    depth = w_in.shape[0]
    assert D == D_MODEL and w_in.shape[2] == N_GROUPS * D + GATE_ROWS, (x.shape, w_in.shape)
    sb_block = min(SB_BLOCK, seq)
    seq_tiles = (min(ML_CHUNK, seq), sb_block * min(SB_QBLOCKS, seq // sb_block), min(TM_MERGE, seq))
    token_tiles = (min(TM_MLP, batch * seq), min(TM_PROJ, batch * seq))
    assert all(seq % t == 0 for t in seq_tiles) and all((batch * seq) % t == 0 for t in token_tiles)
    xt = x.reshape(batch * seq, D)
    w_in_t = jnp.swapaxes(w_in, 1, 2)
    gate_params = [_gate_params(w_in_t, l, b_if[l]) for l in range(depth)]
    h, gates = _prenorm(xt, norm_mix_g[0].reshape(1, D), *gate_params[0])
    wt = _regroup_w_in_t(w_in_t)
    stacked_bf16 = dict(w_ml_proj=w_ml_proj.astype(BF16), w_sc_proj=w_sc_proj.astype(BF16),
                        w_sb_proj=w_sb_proj.astype(BF16), w_out=w_out.astype(BF16),
                        w_up=w_up.astype(BF16), w_down=w_down.astype(BF16))
    for l in range(depth):
        p = dict(w_in=(wt, l), ml_norm_g=ml_norm_g[l], conv_w=conv_w[l], norm_mlp_g=norm_mlp_g[l],
                 **stacked_bf16)
        last = l == depth - 1
        nl = l if last else l + 1
        nxt = dict(g=norm_final_g if last else norm_mix_g[nl], wg_hi=gate_params[nl][0],
                   wg_lo=gate_params[nl][1], bg=gate_params[nl][2], last=last)
        outs = _layer(xt, h, gates, p, nxt, batch=batch, seq=seq)
        if last:
            xt = outs[0]
        else:
            xt, h, gates = outs
    return xt.reshape(batch, seq, D)
```

```python
import functools

import jax
import jax.numpy as jnp
from jax import lax
from jax.experimental import pallas as pl
from jax.experimental.pallas import tpu as pltpu

D_MODEL = 1024
ML_HEADS = 4
ML_HEAD_DIM = 256
SB_HEADS = 16
SB_HEAD_DIM = 64
EPS = 1e-6

VMEM_LIMIT_BYTES = 56 * 1024 * 1024

ML_CHUNK = 512
SB_BLOCK = 128
SB_GROUP = 4
SB_QBLOCKS = 16
SB_FUSED = 3
SB_STAGE_LAG = 6
LOG2E = 1.4426950408889634
SB_DEAD_LOG2 = -152.0
SB_ABSENT_LOG2 = -1e30
TM_PROJ = 2048
TM_MERGE = 512
TM_MLP = 1024
PREV_ROWS = 16

BF16 = jnp.bfloat16
F32 = jnp.float32


def _params(*sem):
    return pltpu.CompilerParams(dimension_semantics=sem, vmem_limit_bytes=VMEM_LIMIT_BYTES)


def _dot(a, b):
    return jnp.dot(a, b, preferred_element_type=F32)


def _dot_nt(a, b):
    return lax.dot_general(a, b, (((1,), (1,)), ((), ())), preferred_element_type=F32)


def _dot_tn(a, b):
    return lax.dot_general(a, b, (((0,), (0,)), ((), ())), preferred_element_type=F32)


def _rmsnorm_rows(x, g):
    ms = jnp.mean(x * x, axis=-1, keepdims=True)
    return x * lax.rsqrt(ms + EPS) * g


def _log_sigmoid(x):
    return jnp.minimum(x, 0.0) - jnp.log(1.0 + jnp.exp(-jnp.abs(x)))


def _split_bf16(x):
    hi = x.astype(BF16)
    lo = (x - hi.astype(F32)).astype(BF16)
    return hi, lo


def _prenorm_outputs(x, g, wg_hi, wg_lo, bg):
    h_hi, h_lo = _split_bf16(_rmsnorm_rows(x, g))
    gates = _dot_nt(wg_hi, h_hi) + (_dot_nt(wg_hi, h_lo) + _dot_nt(wg_lo, h_hi)) + bg
    return h_hi, gates


def _prenorm_kernel(x_ref, g_ref, wg_hi_ref, wg_lo_ref, bg_ref, h_ref, gates_ref):
    h_ref[...], gates_ref[...] = _prenorm_outputs(x_ref[...], g_ref[...], wg_hi_ref[...],
                                                  wg_lo_ref[...], bg_ref[...])


def _prenorm(x, g, wg_hi, wg_lo, bg):
    T, D = x.shape
    G = wg_hi.shape[0]
    tm = min(TM_PROJ, T)
    const = lambda i: (0, 0)
    return pl.pallas_call(
        _prenorm_kernel,
        grid=(T // tm,),
        in_specs=[
            pl.BlockSpec((tm, D), lambda i: (i, 0)),
            pl.BlockSpec((1, D), const),
            pl.BlockSpec((G, D), const),
            pl.BlockSpec((G, D), const),
            pl.BlockSpec((G, 1), const),
        ],
        out_specs=[pl.BlockSpec((tm, D), lambda i: (i, 0)), pl.BlockSpec((G, tm), lambda i: (0, i))],
        out_shape=[jax.ShapeDtypeStruct((T, D), BF16), jax.ShapeDtypeStruct((G, T), F32)],
        compiler_params=_params("parallel"),
        name="prenorm",
    )(x, g, wg_hi, wg_lo, bg)


N_GROUPS = 13
ALIGNED_GROUPS = 4
GATE_ROWS = 2 * ML_HEADS
(G_MQ, G_MK, G_MV, G_MO, G_B, G_C, G_U, G_SQ, G_SK, G_SV, G_GML, G_GSC, G_GSB) = range(N_GROUPS)


def _regroup_kernel(a_ref, b_ref, o_ref):
    k = pl.program_id(1)

    @pl.when(k < ALIGNED_GROUPS)
    def _():
        o_ref[...] = a_ref[...].astype(o_ref.dtype)

    @pl.when(k >= ALIGNED_GROUPS)
    def _():
        window = jnp.concatenate([a_ref[GATE_ROWS:, :], b_ref[...]], axis=0)
        o_ref[...] = window.astype(o_ref.dtype)


def _regroup_w_in_t(w_in_t):
    depth, _, D = w_in_t.shape
    return pl.pallas_call(
        _regroup_kernel,
        grid=(depth, N_GROUPS),
        in_specs=[
            pl.BlockSpec((None, D, D), lambda l, k: (l, k, 0)),
            pl.BlockSpec((None, GATE_ROWS, D), lambda l, k: (l, (k + 1) * (D // GATE_ROWS), 0)),
        ],
        out_specs=pl.BlockSpec((None, D, D), lambda l, k: (l, k, 0)),
        out_shape=jax.ShapeDtypeStruct((depth, N_GROUPS * D, D), BF16),
        compiler_params=_params("parallel", "parallel"),
        name="regroup_w_in",
    )(w_in_t, w_in_t)


def _pick(j, values):
    return sum(jnp.where(j == n, v, 0) for n, v in enumerate(values))


def _proj_kernel(scale_ref, h_ref, w_ref, o_ref):
    scale = scale_ref[pl.program_id(1)]
    o_ref[...] = (_dot_nt(h_ref[...], w_ref[...]) * scale).astype(o_ref.dtype)


def _proj(h, wt, l, groups, scales, out_dtype):
    T, D = h.shape
    tm = min(TM_PROJ, T)
    return pl.pallas_call(
        _proj_kernel,
        grid=(T // tm, len(groups)),
        in_specs=[
            pl.BlockSpec(memory_space=pltpu.SMEM),
            pl.BlockSpec((tm, D), lambda i, j: (i, 0)),
            pl.BlockSpec((None, D, D), lambda i, j: (l, _pick(j, groups), 0)),
        ],
        out_specs=pl.BlockSpec((tm, D), lambda i, j: (i, j)),
        out_shape=jax.ShapeDtypeStruct((T, len(groups) * D), out_dtype),
        compiler_params=_params("parallel", "parallel"),
        name="proj",
    )(jnp.array(scales, F32), h, wt)


def _proj_t_kernel(scale_ref, h_ref, w_ref, o_ref, *, sigmoid):
    out = _dot_nt(w_ref[...], h_ref[...]) * scale_ref[pl.program_id(1)]
    o_ref[...] = (jax.nn.sigmoid(out) if sigmoid else out).astype(o_ref.dtype)


def _proj_t(h, wt, l, groups, scales, out_dtype, sigmoid=False):
    T, D = h.shape
    tm = min(TM_PROJ, T)
    return pl.pallas_call(
        functools.partial(_proj_t_kernel, sigmoid=sigmoid),
        grid=(T // tm, len(groups)),
        in_specs=[
            pl.BlockSpec(memory_space=pltpu.SMEM),
            pl.BlockSpec((tm, D), lambda i, j: (i, 0)),
            pl.BlockSpec((None, D, D), lambda i, j: (l, _pick(j, groups), 0)),
        ],
        out_specs=pl.BlockSpec((D, tm), lambda i, j: (j, i)),
        out_shape=jax.ShapeDtypeStruct((len(groups) * D, T), out_dtype),
        compiler_params=_params("parallel", "parallel"),
        name="proj_t",
    )(jnp.array(scales, F32), h, wt)


def _row_to_columns(r):
    L = r.shape[1]
    col = jnp.transpose(jnp.broadcast_to(r, (128, L)))
    return jnp.concatenate([col] * (L // 128), axis=1)


def _mlstm_kernel(k_ref, qT_ref, vT_ref, oT_ref, gates_ref, gain_ref, y_ref, c_ref, n_ref, m_ref):
    @pl.when(pl.program_id(1) == 0)
    def _():
        c_ref[...] = jnp.zeros_like(c_ref)
        n_ref[...] = jnp.zeros_like(n_ref)
        m_ref[...] = jnp.zeros_like(m_ref)

    L = k_ref.shape[0]
    dh = ML_HEAD_DIM
    feat = lambda g: slice(g * dh, (g + 1) * dh)
    s_idx = lax.broadcasted_iota(jnp.int32, (L, L), 0)
    t_idx = lax.broadcasted_iota(jnp.int32, (L, L), 1)
    causal = s_idx <= t_idx

    gates = gates_ref[...]
    log_f = _log_sigmoid(gates)
    st = {}

    def state_products(g):
        k = k_ref[:, feat(g)]
        qT = qT_ref[feat(g), :]
        c_prev, n_prev = c_ref[g], n_ref[g]
        st[g, "kq"] = _dot(k, qT)
        st[g, "cq"] = _dot(c_prev.astype(BF16), qT)
        st[g, "nq"] = _dot(n_prev.astype(BF16), qT)[0:1, :]

    def intra_chunk(g):
        m_prev = m_ref[g]
        lf_col = _row_to_columns(log_f[ML_HEADS + g:ML_HEADS + g + 1, :])
        b_row = jnp.sum(jnp.where(causal, lf_col, 0.0), axis=0, keepdims=True)
        c_col = _row_to_columns(gates[g:g + 1, :] - b_row)
        c_max = jnp.max(jnp.where(causal, c_col, -jnp.inf), axis=0, keepdims=True)
        u_row = jnp.maximum(m_prev, c_max)
        s_w = st.pop((g, "kq")) * jnp.where(causal, jnp.exp(c_col - u_row), 0.0)
        a_inter = jnp.exp(m_prev - u_row)
        st[g, "den"] = jnp.sum(s_w, axis=0, keepdims=True) + a_inter * st.pop((g, "nq"))
        st[g, "num"] = _dot(vT_ref[feat(g), :], s_w.astype(BF16))
        st[g, "b_row"], st[g, "u_row"], st[g, "a_inter"] = b_row, u_row, a_inter

    def next_state(g):
        m_prev, b_row = m_ref[g], st[g, "b_row"]
        b_last = b_row[:, L - 1:L]
        g_row = b_last - b_row + gates[g:g + 1, :]
        m_new = jnp.maximum(b_last + m_prev, jnp.max(g_row, axis=1, keepdims=True))
        decay = jnp.exp(b_last + m_prev - m_new)
        w_state = jnp.exp(g_row - m_new)
        k = k_ref[:, feat(g)]
        vw = (vT_ref[feat(g), :].astype(F32) * w_state).astype(BF16)
        c_ref[g] = decay * c_ref[g] + _dot(vw, k)
        n_ref[g] = decay * n_ref[g] + _dot(jnp.broadcast_to(w_state, (8, L)).astype(BF16), k)
        m_ref[g] = m_new

    def output(g):
        floor = jnp.exp(-(st.pop((g, "b_row")) + st.pop((g, "u_row"))))
        num = st.pop((g, "num")) + st.pop((g, "a_inter")) * st.pop((g, "cq"))
        h = num / jnp.maximum(jnp.abs(st.pop((g, "den"))), floor)
        hg = oT_ref[feat(g), :] * h
        ms = jnp.mean(hg * hg, axis=0, keepdims=True)
        gain = jnp.concatenate([gain_ref[feat(g), :]] * (L // 128), axis=1)
        y_ref[feat(g), :] = (hg * lax.rsqrt(ms + EPS) * gain).astype(y_ref.dtype)

    for stage in (state_products, intra_chunk, next_state, output):
        for g in range(ML_HEADS):
            stage(g)


def _mlstm(pn, pt, pt_f, gates, gain, *, batch, seq, k_col, qT_row, vT_row, oT_row):
    T = batch * seq
    L = min(ML_CHUNK, seq)
    nc = seq // L
    dh = ML_HEAD_DIM
    W = ML_HEADS * dh
    tok = lambda b, c: b * nc + c
    return pl.pallas_call(
        _mlstm_kernel,
        grid=(batch, nc),
        in_specs=[
            pl.BlockSpec((L, W), lambda b, c: (tok(b, c), k_col)),
            pl.BlockSpec((W, L), lambda b, c: (qT_row, tok(b, c))),
            pl.BlockSpec((W, L), lambda b, c: (vT_row, tok(b, c))),
            pl.BlockSpec((W, L), lambda b, c: (oT_row, tok(b, c))),
            pl.BlockSpec((2 * ML_HEADS, L), lambda b, c: (0, tok(b, c))),
            pl.BlockSpec((W, 128), lambda b, c: (0, 0)),
        ],
        out_specs=pl.BlockSpec((W, L), lambda b, c: (0, tok(b, c))),
        out_shape=jax.ShapeDtypeStruct((W, T), BF16),
        scratch_shapes=[
            pltpu.VMEM((ML_HEADS, dh, dh), F32),
            pltpu.VMEM((ML_HEADS, 8, dh), F32),
            pltpu.VMEM((ML_HEADS, 1, 1), F32),
        ],
        compiler_params=_params("parallel", "arbitrary"),
        name="mlstm",
    )(pn, pt, pt, pt_f, gates, gain)


def _sb_kernel(qT_ref, k_ref, vT_ref, o_ref, acc_ref, *, blk, heads, qblocks):
    step = pl.program_id(2)
    d = SB_HEAD_DIM

    s_idx = lax.broadcasted_iota(jnp.int32, (blk, blk), 0)
    t_idx = lax.broadcasted_iota(jnp.int32, (blk, blk), 1)
    later = (t_idx > s_idx)
    from_here = jnp.where(t_idx >= s_idx, 1.0, 0.0).astype(BF16)

    row_head = lax.broadcasted_iota(jnp.int32, (2 * d, blk), 0) // d
    q_pads = []
    for qb in range(qblocks):
        q_pads.append([])
        for g in range(heads):
            q2 = qT_ref[(g // 2) * 2 * d:(g // 2 + 1) * 2 * d, qb * blk:(qb + 1) * blk]
            q_pads[qb].append(jnp.where(row_head == g % 2, q2, jnp.zeros_like(q2)))

    def neg_log2_1m(zs):
        return jnp.maximum(zs, 0.0) + jnp.log(1.0 + jnp.exp2(-jnp.abs(zs))) * LOG2E

    def visit(chains, rs, init_acc):
        starts = [pl.multiple_of(j * blk, blk) for _, j, _, _ in chains]
        units = [(c, g) for c in range(len(chains)) for g in range(heads)]
        rs = {qb: list(r) for qb, r in rs.items()}
        z, tail, started = {}, {}, set()

        def scores(c, g):
            qb = chains[c][0]
            z[c, g] = _dot(k_ref[pl.ds(starts[c], blk), (g // 2) * 2 * d:(g // 2 + 1) * 2 * d],
                           q_pads[qb][g])

        def tails(c, g):
            nl1 = neg_log2_1m(z[c, g])
            if chains[c][2]:
                nl1 = jnp.where(later, nl1, 0.0)
            tail[c, g] = _dot(from_here, nl1.astype(BF16))

        def weights(c, g):
            qb, _, diagonal, r_gate = chains[c]
            r_in = rs[qb][g] if r_gate is None else jnp.where(r_gate, rs[qb][g], SB_ABSENT_LOG2)
            rs[qb][g] = r_in - tail[c, g][0:1, :]
            a = jnp.exp2(z.pop((c, g)) - tail.pop((c, g)) + r_in)
            if diagonal:
                a = jnp.where(later, a, 0.0)
            contrib = _dot(vT_ref[g * d:(g + 1) * d, pl.ds(starts[c], blk)], a.astype(BF16))
            if init_acc and (qb, g) not in started:
                acc_ref[g * d:(g + 1) * d, qb * blk:(qb + 1) * blk] = contrib
            else:
                acc_ref[g * d:(g + 1) * d, qb * blk:(qb + 1) * blk] += contrib
            started.add((qb, g))

        stages = (scores, tails, weights)
        lag = SB_STAGE_LAG
        for n in range(len(units) + lag * (len(stages) - 1)):
            for k, stage in enumerate(stages):
                if 0 <= n - lag * k < len(units):
                    stage(*units[n - lag * k])
        return {qb: (jnp.max(functools.reduce(jnp.maximum, r)) > SB_DEAD_LOG2, tuple(r))
                for qb, r in rs.items()}

    blocks = [step * qblocks + qb for qb in range(qblocks)]
    chains = []
    for qb, i in enumerate(blocks):
        chains.append((qb, i, True, None))
        chains += [(qb, jnp.maximum(i - back, 0), False, i >= back) for back in range(1, SB_FUSED)]
    zeros = tuple(jnp.zeros((1, blk), F32) for _ in range(heads))
    first = visit(chains, {qb: zeros for qb in range(qblocks)}, True)

    for qb, i in enumerate(blocks):
        def cond(c, i=i):
            n, live, _ = c
            return jnp.logical_and(n < i - (SB_FUSED - 1), live)

        def body(c, qb=qb, i=i):
            n, _, rs = c
            return (n + 1,) + visit([(qb, i - SB_FUSED - n, False, None)], {qb: rs}, False)[qb]

        lax.while_loop(cond, body, (jnp.int32(0),) + first[qb])
    o_ref[...] = acc_ref[...].astype(o_ref.dtype)


def _stick_breaking(pn_b, pt, *, batch, seq, k_col, qT_row, vT_row):
    T = batch * seq
    blk = min(SB_BLOCK, seq)
    qblocks = min(SB_QBLOCKS, seq // blk)
    tq = qblocks * blk
    nq = seq // tq
    gd = SB_GROUP * SB_HEAD_DIM
    return pl.pallas_call(
        functools.partial(_sb_kernel, blk=blk, heads=SB_GROUP, qblocks=qblocks),
        grid=(batch, SB_HEADS // SB_GROUP, nq),
        in_specs=[
            pl.BlockSpec((gd, tq), lambda b, h, i: (qT_row + h, b * nq + i)),
            pl.BlockSpec((seq, gd), lambda b, h, i: (b, k_col + h)),
            pl.BlockSpec((gd, seq), lambda b, h, i: (vT_row + h, b)),
        ],
        out_specs=pl.BlockSpec((gd, tq), lambda b, h, i: (h, b * nq + i)),
        out_shape=jax.ShapeDtypeStruct((SB_HEADS * SB_HEAD_DIM, T), BF16),
        scratch_shapes=[pltpu.VMEM((gd, tq), F32)],
        compiler_params=_params("parallel", "parallel", "arbitrary"),
        name="stick_breaking",
    )(pt, pn_b, pt)


def _merge_kernel(x_ref, h_ref, hp_ref, wb_ref, wc_ref, wu_ref, wgml_ref, wgsc_ref, wgsb_ref,
                  ymlT_ref, ysbT_ref, cw_ref, wml_ref, wsc_ref, wsb_ref, wout_ref, o_ref, *,
                  tiles_per_seq):
    tm, D = x_ref.shape
    P = hp_ref.shape[0]
    first = (pl.program_id(0) % tiles_per_seq) == 0
    h = h_ref[...]
    h_ext = jnp.concatenate([hp_ref[...], h], axis=0)

    z = _dot_nt(h_ext, wc_ref[...]) * _dot_nt(h_ext, wu_ref[...])
    row = lax.broadcasted_iota(jnp.int32, z.shape, 0)
    z = jnp.where(jnp.logical_and(first, row < P), 0.0, z)
    cw = cw_ref[...]
    conv = (cw[0:1, :] * pltpu.roll(z, 2, axis=0)[P:, :] + cw[1:2, :] * pltpu.roll(z, 1, axis=0)[P:, :]
            + cw[2:3, :] * z[P:, :])
    y_sc = _dot_nt(h, wb_ref[...]) * conv

    merged = jax.nn.sigmoid(_dot_nt(h, wgml_ref[...])) * _dot_tn(ymlT_ref[...], wml_ref[...])
    merged += jax.nn.sigmoid(_dot_nt(h, wgsb_ref[...])) * _dot_tn(ysbT_ref[...], wsb_ref[...])
    merged += jax.nn.sigmoid(_dot_nt(h, wgsc_ref[...])) * _dot(y_sc.astype(BF16), wsc_ref[...])
    o_ref[...] = x_ref[...] + _dot(merged.astype(BF16), wout_ref[...])


def _merge(x, h, wt, l, y_mlT, y_sbT, conv_w, w_ml, w_sc, w_sb, w_out, *, seq):
    T, D = x.shape
    tm = min(TM_MERGE, seq)
    const = lambda i: (0, 0)
    resident = lambda shape: pl.BlockSpec(shape, const, pipeline_mode=pl.Buffered(1))
    group = lambda g: pl.BlockSpec((None, D, D), lambda i: (l, g, 0), pipeline_mode=pl.Buffered(1))
    layer = lambda shape: pl.BlockSpec((None,) + shape, lambda i: (l, 0, 0), pipeline_mode=pl.Buffered(1))
    return pl.pallas_call(
        functools.partial(_merge_kernel, tiles_per_seq=seq // tm),
        grid=(T // tm,),
        in_specs=[
            pl.BlockSpec((tm, D), lambda i: (i, 0)),
            pl.BlockSpec((tm, D), lambda i: (i, 0)),
            pl.BlockSpec((PREV_ROWS, D), lambda i: (jnp.maximum(i * (tm // PREV_ROWS) - 1, 0), 0)),
            group(G_B), group(G_C), group(G_U), group(G_GML), group(G_GSC), group(G_GSB),
            pl.BlockSpec((D, tm), lambda i: (0, i)),
            pl.BlockSpec((D, tm), lambda i: (0, i)),
            resident((3, D)),
            layer((D, D)), layer((D, D)), layer((D, D)), layer((D, D)),
        ],
        out_specs=pl.BlockSpec((tm, D), lambda i: (i, 0)),
        out_shape=jax.ShapeDtypeStruct((T, D), F32),
        compiler_params=_params("parallel"),
        name="merge",
    )(x, h, h, wt, wt, wt, wt, wt, wt, y_mlT, y_sbT, conv_w, w_ml, w_sc, w_sb, w_out)


def _mlp_kernel(x_ref, g_ref, wup_ref, wdown_ref, gn_ref, wg_hi_ref, wg_lo_ref, bg_ref, *out_refs,
                ff_chunk, last):
    x = x_ref[...]
    h = _rmsnorm_rows(x, g_ref[...]).astype(BF16)
    acc = x
    for c in range(wup_ref.shape[1] // ff_chunk):
        cols = slice(c * ff_chunk, (c + 1) * ff_chunk)
        up = jnp.maximum(_dot(h, wup_ref[:, cols]), 0.0)
        acc = acc + _dot((up * up).astype(BF16), wdown_ref[cols, :])
    if last:
        out_refs[0][...] = _rmsnorm_rows(acc, gn_ref[...])
    else:
        out_refs[0][...] = acc
        out_refs[1][...], out_refs[2][...] = _prenorm_outputs(acc, gn_ref[...], wg_hi_ref[...],
                                                              wg_lo_ref[...], bg_ref[...])


def _mlp(x, g, w_up, w_down, l, g_next, wg_hi, wg_lo, bg, *, last):
    T, D = x.shape
    F = w_up.shape[2]
    G = wg_hi.shape[0]
    tm = min(TM_MLP, T)
    const = lambda i: (0, 0)
    resident = lambda shape: pl.BlockSpec(shape, const, pipeline_mode=pl.Buffered(1))
    layer = lambda shape: pl.BlockSpec((None,) + shape, lambda i: (l, 0, 0), pipeline_mode=pl.Buffered(1))
    out_specs = [pl.BlockSpec((tm, D), lambda i: (i, 0))]
    out_shape = [jax.ShapeDtypeStruct((T, D), F32)]
    if not last:
        out_specs += [pl.BlockSpec((tm, D), lambda i: (i, 0)), pl.BlockSpec((G, tm), lambda i: (0, i))]
        out_shape += [jax.ShapeDtypeStruct((T, D), BF16), jax.ShapeDtypeStruct((G, T), F32)]
    return pl.pallas_call(
        functools.partial(_mlp_kernel, ff_chunk=1024, last=last),
        grid=(T // tm,),
        in_specs=[
            pl.BlockSpec((tm, D), lambda i: (i, 0)),
            resident((1, D)),
            layer((D, F)),
            layer((F, D)),
            resident((1, D)),
            resident((G, D)), resident((G, D)), resident((G, 1)),
        ],
        out_specs=out_specs,
        out_shape=out_shape,
        compiler_params=_params("parallel"),
        name="mlp",
    )(x, g, w_up, w_down, g_next, wg_hi, wg_lo, bg)


def _gate_params(w_in_t, l, b_if):
    off = 4 * ML_HEADS * ML_HEAD_DIM
    wg_hi, wg_lo = _split_bf16(w_in_t[l, off:off + 2 * ML_HEADS, :])
    return wg_hi, wg_lo, b_if.reshape(2 * ML_HEADS, 1)


def _layer(x, h, gates, p, nxt, *, batch, seq):
    D = D_MODEL
    ml_w = ML_HEADS * ML_HEAD_DIM
    wt, l = p["w_in"]
    pn = _proj(h, wt, l, (G_MK, G_SK), (ML_HEAD_DIM ** -0.5, 1.0), BF16)
    pt = _proj_t(h, wt, l, (G_MQ, G_MV, G_SQ, G_SV), (1.0, 1.0, SB_HEAD_DIM ** -0.5 * LOG2E, 1.0), BF16)
    pt_f = _proj_t(h, wt, l, (G_MO,), (1.0,), F32, sigmoid=True)

    gain = jnp.broadcast_to(p["ml_norm_g"].reshape(ml_w, 1), (ml_w, 128))
    y_mlT = _mlstm(pn, pt, pt_f, gates, gain, batch=batch, seq=seq,
                   k_col=0, qT_row=0, vT_row=1, oT_row=0)
    sb_unit = SB_GROUP * SB_HEAD_DIM
    y_sbT = _stick_breaking(pn, pt, batch=batch, seq=seq, k_col=ml_w // sb_unit,
                            qT_row=2 * ml_w // sb_unit, vT_row=(2 * ml_w + D) // sb_unit)
    x = _merge(x, h, wt, l, y_mlT, y_sbT, p["conv_w"],
               p["w_ml_proj"], p["w_sc_proj"], p["w_sb_proj"], p["w_out"], seq=seq)
    return _mlp(x, p["norm_mlp_g"].reshape(1, D), p["w_up"], p["w_down"], l,
                nxt["g"].reshape(1, D), nxt["wg_hi"], nxt["wg_lo"], nxt["bg"], last=nxt["last"])


def kernel(x, norm_mix_g, w_in, b_if, ml_norm_g, conv_w, w_ml_proj, w_sc_proj, w_sb_proj, w_out,
           norm_mlp_g, w_up, w_down, norm_final_g):
    batch, seq, D = x.shape
    depth = w_in.shape[0]
    assert D == D_MODEL and w_in.shape[2] == N_GROUPS * D + GATE_ROWS, (x.shape, w_in.shape)
    sb_block = min(SB_BLOCK, seq)
    seq_tiles = (min(ML_CHUNK, seq), sb_block * min(SB_QBLOCKS, seq // sb_block), min(TM_MERGE, seq))
    token_tiles = (min(TM_MLP, batch * seq), min(TM_PROJ, batch * seq))
    assert all(seq % t == 0 for t in seq_tiles) and all((batch * seq) % t == 0 for t in token_tiles)
    xt = x.reshape(batch * seq, D)
    w_in_t = jnp.swapaxes(w_in, 1, 2)
    gate_params = [_gate_params(w_in_t, l, b_if[l]) for l in range(depth)]
    h, gates = _prenorm(xt, norm_mix_g[0].reshape(1, D), *gate_params[0])
    wt = _regroup_w_in_t(w_in_t)
    stacked_bf16 = dict(w_ml_proj=w_ml_proj.astype(BF16), w_sc_proj=w_sc_proj.astype(BF16),
                        w_sb_proj=w_sb_proj.astype(BF16), w_out=w_out.astype(BF16),
                        w_up=w_up.astype(BF16), w_down=w_down.astype(BF16))
    for l in range(depth):
        p = dict(w_in=(wt, l), ml_norm_g=ml_norm_g[l], conv_w=conv_w[l], norm_mlp_g=norm_mlp_g[l],
                 **stacked_bf16)
        last = l == depth - 1
        nl = l if last else l + 1
        nxt = dict(g=norm_final_g if last else norm_mix_g[nl], wg_hi=gate_params[nl][0],
                   wg_lo=gate_params[nl][1], bg=gate_params[nl][2], last=last)
        outs = _layer(xt, h, gates, p, nxt, batch=batch, seq=seq)
        if last:
            xt = outs[0]
        else:
            xt, h, gates = outs
    return xt.reshape(batch, seq, D)
```

```python
import functools

import jax
import jax.numpy as jnp
from jax import lax
from jax.experimental import pallas as pl
from jax.experimental.pallas import tpu as pltpu

D_MODEL = 1024
ML_HEADS = 4
ML_HEAD_DIM = 256
SB_HEADS = 16
SB_HEAD_DIM = 64
EPS = 1e-6

VMEM_LIMIT_BYTES = 56 * 1024 * 1024

ML_CHUNK = 512
SB_BLOCK = 128
SB_GROUP = 4
SB_QBLOCKS = 16
SB_FUSED = 3
SB_STAGE_LAG = 6
LOG2E = 1.4426950408889634
SB_DEAD_LOG2 = -152.0
SB_ABSENT_LOG2 = -1e30
TM_PROJ = 2048
TM_MERGE = 512
MERGE_COL_SPLIT = 2
TM_MLP = 1024
PREV_ROWS = 16

BF16 = jnp.bfloat16
F32 = jnp.float32


def _params(*sem):
    return pltpu.CompilerParams(dimension_semantics=sem, vmem_limit_bytes=VMEM_LIMIT_BYTES)


def _dot(a, b):
    return jnp.dot(a, b, preferred_element_type=F32)


def _dot_nt(a, b):
    return lax.dot_general(a, b, (((1,), (1,)), ((), ())), preferred_element_type=F32)


def _dot_tn(a, b):
    return lax.dot_general(a, b, (((0,), (0,)), ((), ())), preferred_element_type=F32)


def _rmsnorm_rows(x, g):
    ms = jnp.mean(x * x, axis=-1, keepdims=True)
    return x * lax.rsqrt(ms + EPS) * g


def _log_sigmoid(x):
    return jnp.minimum(x, 0.0) - jnp.log(1.0 + jnp.exp(-jnp.abs(x)))


def _split_bf16(x):
    hi = x.astype(BF16)
    lo = (x - hi.astype(F32)).astype(BF16)
    return hi, lo


def _prenorm_outputs(x, g, wg_hi, wg_lo, bg):
    h_hi, h_lo = _split_bf16(_rmsnorm_rows(x, g))
    gates = _dot_nt(wg_hi, h_hi) + (_dot_nt(wg_hi, h_lo) + _dot_nt(wg_lo, h_hi)) + bg
    return h_hi, gates


def _prenorm_kernel(x_ref, g_ref, wg_hi_ref, wg_lo_ref, bg_ref, h_ref, gates_ref):
    h_ref[...], gates_ref[...] = _prenorm_outputs(x_ref[...], g_ref[...], wg_hi_ref[...],
                                                  wg_lo_ref[...], bg_ref[...])


def _prenorm(x, g, wg_hi, wg_lo, bg):
    T, D = x.shape
    G = wg_hi.shape[0]
    tm = min(TM_PROJ, T)
    const = lambda i: (0, 0)
    return pl.pallas_call(
        _prenorm_kernel,
        grid=(T // tm,),
        in_specs=[
            pl.BlockSpec((tm, D), lambda i: (i, 0)),
            pl.BlockSpec((1, D), const),
            pl.BlockSpec((G, D), const),
            pl.BlockSpec((G, D), const),
            pl.BlockSpec((G, 1), const),
        ],
        out_specs=[pl.BlockSpec((tm, D), lambda i: (i, 0)), pl.BlockSpec((G, tm), lambda i: (0, i))],
        out_shape=[jax.ShapeDtypeStruct((T, D), BF16), jax.ShapeDtypeStruct((G, T), F32)],
        compiler_params=_params("parallel"),
        name="prenorm",
    )(x, g, wg_hi, wg_lo, bg)


N_GROUPS = 13
ALIGNED_GROUPS = 4
GATE_ROWS = 2 * ML_HEADS
(G_MQ, G_MK, G_MV, G_MO, G_B, G_C, G_U, G_SQ, G_SK, G_SV, G_GML, G_GSC, G_GSB) = range(N_GROUPS)


def _regroup_kernel(a_ref, b_ref, o_ref):
    k = pl.program_id(1)

    @pl.when(k < ALIGNED_GROUPS)
    def _():
        o_ref[...] = a_ref[...].astype(o_ref.dtype)

    @pl.when(k >= ALIGNED_GROUPS)
    def _():
        window = jnp.concatenate([a_ref[GATE_ROWS:, :], b_ref[...]], axis=0)
        o_ref[...] = window.astype(o_ref.dtype)


def _regroup_w_in_t(w_in_t):
    depth, _, D = w_in_t.shape
    return pl.pallas_call(
        _regroup_kernel,
        grid=(depth, N_GROUPS),
        in_specs=[
            pl.BlockSpec((None, D, D), lambda l, k: (l, k, 0)),
            pl.BlockSpec((None, GATE_ROWS, D), lambda l, k: (l, (k + 1) * (D // GATE_ROWS), 0)),
        ],
        out_specs=pl.BlockSpec((None, D, D), lambda l, k: (l, k, 0)),
        out_shape=jax.ShapeDtypeStruct((depth, N_GROUPS * D, D), BF16),
        compiler_params=_params("parallel", "parallel"),
        name="regroup_w_in",
    )(w_in_t, w_in_t)


def _pick(j, values):
    return sum(jnp.where(j == n, v, 0) for n, v in enumerate(values))


def _proj_kernel(scale_ref, h_ref, w_ref, o_ref):
    scale = scale_ref[pl.program_id(1)]
    o_ref[...] = (_dot_nt(h_ref[...], w_ref[...]) * scale).astype(o_ref.dtype)


def _proj(h, wt, l, groups, scales, out_dtype):
    T, D = h.shape
    tm = min(TM_PROJ, T)
    return pl.pallas_call(
        _proj_kernel,
        grid=(T // tm, len(groups)),
        in_specs=[
            pl.BlockSpec(memory_space=pltpu.SMEM),
            pl.BlockSpec((tm, D), lambda i, j: (i, 0)),
            pl.BlockSpec((None, D, D), lambda i, j: (l, _pick(j, groups), 0)),
        ],
        out_specs=pl.BlockSpec((tm, D), lambda i, j: (i, j)),
        out_shape=jax.ShapeDtypeStruct((T, len(groups) * D), out_dtype),
        compiler_params=_params("parallel", "parallel"),
        name="proj",
    )(jnp.array(scales, F32), h, wt)


def _proj_t_kernel(scale_ref, h_ref, w_ref, o_ref, *, sigmoid):
    out = _dot_nt(w_ref[...], h_ref[...]) * scale_ref[pl.program_id(1)]
    o_ref[...] = (jax.nn.sigmoid(out) if sigmoid else out).astype(o_ref.dtype)


def _proj_t(h, wt, l, groups, scales, out_dtype, sigmoid=False):
    T, D = h.shape
    tm = min(TM_PROJ, T)
    return pl.pallas_call(
        functools.partial(_proj_t_kernel, sigmoid=sigmoid),
        grid=(T // tm, len(groups)),
        in_specs=[
            pl.BlockSpec(memory_space=pltpu.SMEM),
            pl.BlockSpec((tm, D), lambda i, j: (i, 0)),
            pl.BlockSpec((None, D, D), lambda i, j: (l, _pick(j, groups), 0)),
        ],
        out_specs=pl.BlockSpec((D, tm), lambda i, j: (j, i)),
        out_shape=jax.ShapeDtypeStruct((len(groups) * D, T), out_dtype),
        compiler_params=_params("parallel", "parallel"),
        name="proj_t",
    )(jnp.array(scales, F32), h, wt)


def _row_to_columns(r):
    L = r.shape[1]
    col = jnp.transpose(jnp.broadcast_to(r, (128, L)))
    return jnp.concatenate([col] * (L // 128), axis=1)


def _mlstm_kernel(k_ref, qT_ref, vT_ref, oT_ref, gates_ref, gain_ref, y_ref, c_ref, n_ref, m_ref):
    @pl.when(pl.program_id(1) == 0)
    def _():
        c_ref[...] = jnp.zeros_like(c_ref)
        n_ref[...] = jnp.zeros_like(n_ref)
        m_ref[...] = jnp.zeros_like(m_ref)

    L = k_ref.shape[0]
    dh = ML_HEAD_DIM
    feat = lambda g: slice(g * dh, (g + 1) * dh)
    s_idx = lax.broadcasted_iota(jnp.int32, (L, L), 0)
    t_idx = lax.broadcasted_iota(jnp.int32, (L, L), 1)
    causal = s_idx <= t_idx

    gates = gates_ref[...]
    log_f = _log_sigmoid(gates)
    st = {}

    def state_products(g):
        k = k_ref[:, feat(g)]
        qT = qT_ref[feat(g), :]
        c_prev, n_prev = c_ref[g], n_ref[g]
        st[g, "kq"] = _dot(k, qT)
        st[g, "cq"] = _dot(c_prev.astype(BF16), qT)
        st[g, "nq"] = _dot(n_prev.astype(BF16), qT)[0:1, :]

    def intra_chunk(g):
        m_prev = m_ref[g]
        lf_col = _row_to_columns(log_f[ML_HEADS + g:ML_HEADS + g + 1, :])
        b_row = jnp.sum(jnp.where(causal, lf_col, 0.0), axis=0, keepdims=True)
        c_col = _row_to_columns(gates[g:g + 1, :] - b_row)
        c_max = jnp.max(jnp.where(causal, c_col, -jnp.inf), axis=0, keepdims=True)
        u_row = jnp.maximum(m_prev, c_max)
        s_w = st.pop((g, "kq")) * jnp.where(causal, jnp.exp(c_col - u_row), 0.0)
        a_inter = jnp.exp(m_prev - u_row)
        st[g, "den"] = jnp.sum(s_w, axis=0, keepdims=True) + a_inter * st.pop((g, "nq"))
        st[g, "num"] = _dot(vT_ref[feat(g), :], s_w.astype(BF16))
        st[g, "b_row"], st[g, "u_row"], st[g, "a_inter"] = b_row, u_row, a_inter

    def next_state(g):
        m_prev, b_row = m_ref[g], st[g, "b_row"]
        b_last = b_row[:, L - 1:L]
        g_row = b_last - b_row + gates[g:g + 1, :]
        m_new = jnp.maximum(b_last + m_prev, jnp.max(g_row, axis=1, keepdims=True))
        decay = jnp.exp(b_last + m_prev - m_new)
        w_state = jnp.exp(g_row - m_new)
        k = k_ref[:, feat(g)]
        vw = (vT_ref[feat(g), :].astype(F32) * w_state).astype(BF16)
        c_ref[g] = decay * c_ref[g] + _dot(vw, k)
        n_ref[g] = decay * n_ref[g] + _dot(jnp.broadcast_to(w_state, (8, L)).astype(BF16), k)
        m_ref[g] = m_new

    def output(g):
        floor = jnp.exp(-(st.pop((g, "b_row")) + st.pop((g, "u_row"))))
        num = st.pop((g, "num")) + st.pop((g, "a_inter")) * st.pop((g, "cq"))
        h = num / jnp.maximum(jnp.abs(st.pop((g, "den"))), floor)
        hg = oT_ref[feat(g), :] * h
        ms = jnp.mean(hg * hg, axis=0, keepdims=True)
        gain = jnp.concatenate([gain_ref[feat(g), :]] * (L // 128), axis=1)
        y_ref[feat(g), :] = (hg * lax.rsqrt(ms + EPS) * gain).astype(y_ref.dtype)

    for stage in (state_products, intra_chunk, next_state, output):
        for g in range(ML_HEADS):
            stage(g)


def _mlstm(pn, pt, pt_f, gates, gain, *, batch, seq, k_col, qT_row, vT_row, oT_row):
    T = batch * seq
    L = min(ML_CHUNK, seq)
    nc = seq // L
    dh = ML_HEAD_DIM
    W = ML_HEADS * dh
    tok = lambda b, c: b * nc + c
    return pl.pallas_call(
        _mlstm_kernel,
        grid=(batch, nc),
        in_specs=[
            pl.BlockSpec((L, W), lambda b, c: (tok(b, c), k_col)),
            pl.BlockSpec((W, L), lambda b, c: (qT_row, tok(b, c))),
            pl.BlockSpec((W, L), lambda b, c: (vT_row, tok(b, c))),
            pl.BlockSpec((W, L), lambda b, c: (oT_row, tok(b, c))),
            pl.BlockSpec((2 * ML_HEADS, L), lambda b, c: (0, tok(b, c))),
            pl.BlockSpec((W, 128), lambda b, c: (0, 0)),
        ],
        out_specs=pl.BlockSpec((W, L), lambda b, c: (0, tok(b, c))),
        out_shape=jax.ShapeDtypeStruct((W, T), BF16),
        scratch_shapes=[
            pltpu.VMEM((ML_HEADS, dh, dh), F32),
            pltpu.VMEM((ML_HEADS, 8, dh), F32),
            pltpu.VMEM((ML_HEADS, 1, 1), F32),
        ],
        compiler_params=_params("parallel", "arbitrary"),
        name="mlstm",
    )(pn, pt, pt, pt_f, gates, gain)


def _sb_kernel(qT_ref, k_ref, vT_ref, o_ref, acc_ref, *, blk, heads, qblocks):
    step = pl.program_id(2)
    d = SB_HEAD_DIM

    s_idx = lax.broadcasted_iota(jnp.int32, (blk, blk), 0)
    t_idx = lax.broadcasted_iota(jnp.int32, (blk, blk), 1)
    later = (t_idx > s_idx)
    from_here = jnp.where(t_idx >= s_idx, 1.0, 0.0).astype(BF16)

    row_head = lax.broadcasted_iota(jnp.int32, (2 * d, blk), 0) // d
    q_pads = []
    for qb in range(qblocks):
        q_pads.append([])
        for g in range(heads):
            q2 = qT_ref[(g // 2) * 2 * d:(g // 2 + 1) * 2 * d, qb * blk:(qb + 1) * blk]
            q_pads[qb].append(jnp.where(row_head == g % 2, q2, jnp.zeros_like(q2)))

    def neg_log2_1m(zs):
        return jnp.maximum(zs, 0.0) + jnp.log(1.0 + jnp.exp2(-jnp.abs(zs))) * LOG2E

    def visit(chains, rs, init_acc):
        starts = [pl.multiple_of(j * blk, blk) for _, j, _, _ in chains]
        units = [(c, g) for c in range(len(chains)) for g in range(heads)]
        rs = {qb: list(r) for qb, r in rs.items()}
        z, tail, started = {}, {}, set()

        def scores(c, g):
            qb = chains[c][0]
            z[c, g] = _dot(k_ref[pl.ds(starts[c], blk), (g // 2) * 2 * d:(g // 2 + 1) * 2 * d],
                           q_pads[qb][g])

        def tails(c, g):
            nl1 = neg_log2_1m(z[c, g])
            if chains[c][2]:
                nl1 = jnp.where(later, nl1, 0.0)
            tail[c, g] = _dot(from_here, nl1.astype(BF16))

        def weights(c, g):
            qb, _, diagonal, r_gate = chains[c]
            r_in = rs[qb][g] if r_gate is None else jnp.where(r_gate, rs[qb][g], SB_ABSENT_LOG2)
            rs[qb][g] = r_in - tail[c, g][0:1, :]
            a = jnp.exp2(z.pop((c, g)) - tail.pop((c, g)) + r_in)
            if diagonal:
                a = jnp.where(later, a, 0.0)
            contrib = _dot(vT_ref[g * d:(g + 1) * d, pl.ds(starts[c], blk)], a.astype(BF16))
            if init_acc and (qb, g) not in started:
                acc_ref[g * d:(g + 1) * d, qb * blk:(qb + 1) * blk] = contrib
            else:
                acc_ref[g * d:(g + 1) * d, qb * blk:(qb + 1) * blk] += contrib
            started.add((qb, g))

        stages = (scores, tails, weights)
        lag = SB_STAGE_LAG
        for n in range(len(units) + lag * (len(stages) - 1)):
            for k, stage in enumerate(stages):
                if 0 <= n - lag * k < len(units):
                    stage(*units[n - lag * k])
        return {qb: (jnp.max(functools.reduce(jnp.maximum, r)) > SB_DEAD_LOG2, tuple(r))
                for qb, r in rs.items()}

    blocks = [step * qblocks + qb for qb in range(qblocks)]
    chains = []
    for qb, i in enumerate(blocks):
        chains.append((qb, i, True, None))
        chains += [(qb, jnp.maximum(i - back, 0), False, i >= back) for back in range(1, SB_FUSED)]
    zeros = tuple(jnp.zeros((1, blk), F32) for _ in range(heads))
    first = visit(chains, {qb: zeros for qb in range(qblocks)}, True)

    for qb, i in enumerate(blocks):
        def cond(c, i=i):
            n, live, _ = c
            return jnp.logical_and(n < i - (SB_FUSED - 1), live)

        def body(c, qb=qb, i=i):
            n, _, rs = c
            return (n + 1,) + visit([(qb, i - SB_FUSED - n, False, None)], {qb: rs}, False)[qb]

        lax.while_loop(cond, body, (jnp.int32(0),) + first[qb])
    o_ref[...] = acc_ref[...].astype(o_ref.dtype)


def _stick_breaking(pn_b, pt, *, batch, seq, k_col, qT_row, vT_row):
    T = batch * seq
    blk = min(SB_BLOCK, seq)
    qblocks = min(SB_QBLOCKS, seq // blk)
    tq = qblocks * blk
    nq = seq // tq
    gd = SB_GROUP * SB_HEAD_DIM
    return pl.pallas_call(
        functools.partial(_sb_kernel, blk=blk, heads=SB_GROUP, qblocks=qblocks),
        grid=(batch, SB_HEADS // SB_GROUP, nq),
        in_specs=[
            pl.BlockSpec((gd, tq), lambda b, h, i: (qT_row + h, b * nq + i)),
            pl.BlockSpec((seq, gd), lambda b, h, i: (b, k_col + h)),
            pl.BlockSpec((gd, seq), lambda b, h, i: (vT_row + h, b)),
        ],
        out_specs=pl.BlockSpec((gd, tq), lambda b, h, i: (h, b * nq + i)),
        out_shape=jax.ShapeDtypeStruct((SB_HEADS * SB_HEAD_DIM, T), BF16),
        scratch_shapes=[pltpu.VMEM((gd, tq), F32)],
        compiler_params=_params("parallel", "parallel", "arbitrary"),
        name="stick_breaking",
    )(pt, pn_b, pt)


def _merge_kernel(x_ref, h_ref, hp_ref, wb_ref, wc_ref, wu_ref, wgml_ref, wgsc_ref, wgsb_ref,
                  ymlT_ref, ysbT_ref, cw_ref, wml_ref, wsc_ref, wsb_ref, wout_ref, o_ref, *,
                  tiles_per_seq):
    tm, D = x_ref.shape
    P = hp_ref.shape[0]
    first = (pl.program_id(0) % tiles_per_seq) == 0
    h = h_ref[...]
    h_ext = jnp.concatenate([hp_ref[...], h], axis=0)

    z = _dot_nt(h_ext, wc_ref[...]) * _dot_nt(h_ext, wu_ref[...])
    row = lax.broadcasted_iota(jnp.int32, z.shape, 0)
    z = jnp.where(jnp.logical_and(first, row < P), 0.0, z)
    cw = cw_ref[...]
    conv = (cw[0:1, :] * pltpu.roll(z, 2, axis=0)[P:, :] + cw[1:2, :] * pltpu.roll(z, 1, axis=0)[P:, :]
            + cw[2:3, :] * z[P:, :])
    y_sc = _dot_nt(h, wb_ref[...]) * conv

    y_sc = y_sc.astype(BF16)
    halves = []
    for c in range(MERGE_COL_SPLIT):
        cols = slice(c * (D // MERGE_COL_SPLIT), (c + 1) * (D // MERGE_COL_SPLIT))
        m = jax.nn.sigmoid(_dot_nt(h, wgml_ref[cols, :])) * _dot_tn(ymlT_ref[...], wml_ref[:, cols])
        m += jax.nn.sigmoid(_dot_nt(h, wgsb_ref[cols, :])) * _dot_tn(ysbT_ref[...], wsb_ref[:, cols])
        m += jax.nn.sigmoid(_dot_nt(h, wgsc_ref[cols, :])) * _dot(y_sc, wsc_ref[:, cols])
        halves.append(m.astype(BF16))
    o_ref[...] = x_ref[...] + _dot(jnp.concatenate(halves, axis=1), wout_ref[...])


def _merge(x, h, wt, l, y_mlT, y_sbT, conv_w, w_ml, w_sc, w_sb, w_out, *, seq):
    T, D = x.shape
    tm = min(TM_MERGE, seq)
    const = lambda i: (0, 0)
    resident = lambda shape: pl.BlockSpec(shape, const, pipeline_mode=pl.Buffered(1))
    group = lambda g: pl.BlockSpec((None, D, D), lambda i: (l, g, 0), pipeline_mode=pl.Buffered(1))
    layer = lambda shape: pl.BlockSpec((None,) + shape, lambda i: (l, 0, 0), pipeline_mode=pl.Buffered(1))
    return pl.pallas_call(
        functools.partial(_merge_kernel, tiles_per_seq=seq // tm),
        grid=(T // tm,),
        in_specs=[
            pl.BlockSpec((tm, D), lambda i: (i, 0)),
            pl.BlockSpec((tm, D), lambda i: (i, 0)),
            pl.BlockSpec((PREV_ROWS, D), lambda i: (jnp.maximum(i * (tm // PREV_ROWS) - 1, 0), 0)),
            group(G_B), group(G_C), group(G_U), group(G_GML), group(G_GSC), group(G_GSB),
            pl.BlockSpec((D, tm), lambda i: (0, i)),
            pl.BlockSpec((D, tm), lambda i: (0, i)),
            resident((3, D)),
            layer((D, D)), layer((D, D)), layer((D, D)), layer((D, D)),
        ],
        out_specs=pl.BlockSpec((tm, D), lambda i: (i, 0)),
        out_shape=jax.ShapeDtypeStruct((T, D), F32),
        compiler_params=_params("parallel"),
        name="merge",
    )(x, h, h, wt, wt, wt, wt, wt, wt, y_mlT, y_sbT, conv_w, w_ml, w_sc, w_sb, w_out)


def _mlp_kernel(x_ref, g_ref, wup_ref, wdown_ref, gn_ref, wg_hi_ref, wg_lo_ref, bg_ref, *out_refs,
                ff_chunk, last):
    x = x_ref[...]
    h = _rmsnorm_rows(x, g_ref[...]).astype(BF16)
    acc = x
    for c in range(wup_ref.shape[1] // ff_chunk):
        cols = slice(c * ff_chunk, (c + 1) * ff_chunk)
        up = jnp.maximum(_dot(h, wup_ref[:, cols]), 0.0)
        acc = acc + _dot((up * up).astype(BF16), wdown_ref[cols, :])
    if last:
        out_refs[0][...] = _rmsnorm_rows(acc, gn_ref[...])
    else:
        out_refs[0][...] = acc
        out_refs[1][...], out_refs[2][...] = _prenorm_outputs(acc, gn_ref[...], wg_hi_ref[...],
                                                              wg_lo_ref[...], bg_ref[...])


def _mlp(x, g, w_up, w_down, l, g_next, wg_hi, wg_lo, bg, *, last):
    T, D = x.shape
    F = w_up.shape[2]
    G = wg_hi.shape[0]
    tm = min(TM_MLP, T)
    const = lambda i: (0, 0)
    resident = lambda shape: pl.BlockSpec(shape, const, pipeline_mode=pl.Buffered(1))
    layer = lambda shape: pl.BlockSpec((None,) + shape, lambda i: (l, 0, 0), pipeline_mode=pl.Buffered(1))
    out_specs = [pl.BlockSpec((tm, D), lambda i: (i, 0))]
    out_shape = [jax.ShapeDtypeStruct((T, D), F32)]
    if not last:
        out_specs += [pl.BlockSpec((tm, D), lambda i: (i, 0)), pl.BlockSpec((G, tm), lambda i: (0, i))]
        out_shape += [jax.ShapeDtypeStruct((T, D), BF16), jax.ShapeDtypeStruct((G, T), F32)]
    return pl.pallas_call(
        functools.partial(_mlp_kernel, ff_chunk=1024, last=last),
        grid=(T // tm,),
        in_specs=[
            pl.BlockSpec((tm, D), lambda i: (i, 0)),
            resident((1, D)),
            layer((D, F)),
            layer((F, D)),
            resident((1, D)),
            resident((G, D)), resident((G, D)), resident((G, 1)),
        ],
        out_specs=out_specs,
        out_shape=out_shape,
        compiler_params=_params("parallel"),
        name="mlp",
    )(x, g, w_up, w_down, g_next, wg_hi, wg_lo, bg)


def _gate_params(w_in_t, l, b_if):
    off = 4 * ML_HEADS * ML_HEAD_DIM
    wg_hi, wg_lo = _split_bf16(w_in_t[l, off:off + 2 * ML_HEADS, :])
    return wg_hi, wg_lo, b_if.reshape(2 * ML_HEADS, 1)


def _layer(x, h, gates, p, nxt, *, batch, seq):
    D = D_MODEL
    ml_w = ML_HEADS * ML_HEAD_DIM
    wt, l = p["w_in"]
    pn = _proj(h, wt, l, (G_MK, G_SK), (ML_HEAD_DIM ** -0.5, 1.0), BF16)
    pt = _proj_t(h, wt, l, (G_MQ, G_MV, G_SQ, G_SV), (1.0, 1.0, SB_HEAD_DIM ** -0.5 * LOG2E, 1.0), BF16)
    pt_f = _proj_t(h, wt, l, (G_MO,), (1.0,), F32, sigmoid=True)

    gain = jnp.broadcast_to(p["ml_norm_g"].reshape(ml_w, 1), (ml_w, 128))
    y_mlT = _mlstm(pn, pt, pt_f, gates, gain, batch=batch, seq=seq,
                   k_col=0, qT_row=0, vT_row=1, oT_row=0)
    sb_unit = SB_GROUP * SB_HEAD_DIM
    y_sbT = _stick_breaking(pn, pt, batch=batch, seq=seq, k_col=ml_w // sb_unit,
                            qT_row=2 * ml_w // sb_unit, vT_row=(2 * ml_w + D) // sb_unit)
    x = _merge(x, h, wt, l, y_mlT, y_sbT, p["conv_w"],
               p["w_ml_proj"], p["w_sc_proj"], p["w_sb_proj"], p["w_out"], seq=seq)
    return _mlp(x, p["norm_mlp_g"].reshape(1, D), p["w_up"], p["w_down"], l,
                nxt["g"].reshape(1, D), nxt["wg_hi"], nxt["wg_lo"], nxt["bg"], last=nxt["last"])


def kernel(x, norm_mix_g, w_in, b_if, ml_norm_g, conv_w, w_ml_proj, w_sc_proj, w_sb_proj, w_out,
           norm_mlp_g, w_up, w_down, norm_final_g):
    batch, seq, D = x.shape
    depth = w_in.shape[0]
    assert D == D_MODEL and w_in.shape[2] == N_GROUPS * D + GATE_ROWS, (x.shape, w_in.shape)
    sb_block = min(SB_BLOCK, seq)
    seq_tiles = (min(ML_CHUNK, seq), sb_block * min(SB_QBLOCKS, seq // sb_block), min(TM_MERGE, seq))
    token_tiles = (min(TM_MLP, batch * seq), min(TM_PROJ, batch * seq))
    assert all(seq % t == 0 for t in seq_tiles) and all((batch * seq) % t == 0 for t in token_tiles)
    xt = x.reshape(batch * seq, D)
    w_in_t = jnp.swapaxes(w_in, 1, 2)
    gate_params = [_gate_params(w_in_t, l, b_if[l]) for l in range(depth)]
    h, gates = _prenorm(xt, norm_mix_g[0].reshape(1, D), *gate_params[0])
    wt = _regroup_w_in_t(w_in_t)
    stacked_bf16 = dict(w_ml_proj=w_ml_proj.astype(BF16), w_sc_proj=w_sc_proj.astype(BF16),
                        w_sb_proj=w_sb_proj.astype(BF16), w_out=w_out.astype(BF16),
                        w_up=w_up.astype(BF16), w_down=w_down.astype(BF16))
    for l in range(depth):
        p = dict(w_in=(wt, l), ml_norm_g=ml_norm_g[l], conv_w=conv_w[l], norm_mlp_g=norm_mlp_g[l],
                 **stacked_bf16)
        last = l == depth - 1
        nl = l if last else l + 1
        nxt = dict(g=norm_final_g if last else norm_mix_g[nl], wg_hi=gate_params[nl][0],
                   wg_lo=gate_params[nl][1], bg=gate_params[nl][2], last=last)
        outs = _layer(xt, h, gates, p, nxt, batch=batch, seq=seq)
        if last:
            xt = outs[0]
        else:
            xt, h, gates = outs
    return xt.reshape(batch, seq, D)
```
